```python
import math
import jax, jax.numpy as jnp
from jax import lax
import numpy as np

D_MODEL = 2048
BATCH = 4
SEQ = 2048
DEPTH = 2
DEC_BATCH = 128
DEC_SEQ = 8
PAST_LEN = 2048
PAGE_SIZE = 128

N_EVEN = (DEPTH + 1) // 2
N_ODD = DEPTH // 2
HEAD_DIM = 128
MIX_WIDTH = D_MODEL
GDN_HEADS = MIX_WIDTH // (2 * HEAD_DIM)
GDN_WIDTH = GDN_HEADS * HEAD_DIM
CONV_W = 4
GDN_CHUNK = 64
MLSTM_HEADS = MIX_WIDTH // (2 * HEAD_DIM)
MLSTM_WIDTH = MLSTM_HEADS * HEAD_DIM
MLSTM_CHUNK = 64
MLSTM_EPS = 1e-6
SB_HEADS = MIX_WIDTH // (2 * HEAD_DIM)
SB_WIDTH = SB_HEADS * HEAD_DIM
NSA_HEADS = MIX_WIDTH // (2 * HEAD_DIM)
NSA_KV_HEADS = 2
NSA_GROUP = NSA_HEADS // NSA_KV_HEADS
NSA_WIDTH = NSA_HEADS * HEAD_DIM
NSA_KV_WIDTH = NSA_KV_HEADS * HEAD_DIM
CMP_BLOCK = 32
SEL_BLOCK = 64
TOP_N = 8
WINDOW = 512
FORCE_SCORE = 1.0e4
Q_BLOCK = 128
N_BUCKETS = 32
MAX_DISTANCE = 128
FFN_DIM = 5632
RMS_EPS = 1e-6
EVEN_SPLITS = (3 * GDN_WIDTH, GDN_HEADS, GDN_HEADS, GDN_WIDTH,
               MLSTM_WIDTH, MLSTM_WIDTH, MLSTM_WIDTH, MLSTM_HEADS, MLSTM_HEADS, MLSTM_WIDTH)
ODD_SPLITS = (SB_WIDTH, SB_WIDTH, SB_WIDTH, NSA_WIDTH) + (NSA_KV_WIDTH,) * 6 + (3 * NSA_HEADS,)
EVEN_PROJ = sum(EVEN_SPLITS)
ODD_PROJ = sum(ODD_SPLITS)

kernel_name = 'hybrid_gdn_mlstm_stickbreak_nsa_decoder_step'


def _block_len(t, blk):
    return blk if t % blk == 0 else t


def _split(z, sizes):
    return jnp.split(z, np.cumsum(sizes)[:-1].tolist(), axis=-1)


def _rms_norm(x, g):
    xf = x.astype(jnp.float32)
    y = xf * lax.rsqrt(jnp.mean(xf * xf, axis=-1, keepdims=True) + RMS_EPS)
    return (y * g.astype(jnp.float32)).astype(x.dtype)


def _l2norm(x):
    return x * lax.rsqrt(jnp.sum(x * x, axis=-1, keepdims=True) + 1e-6)


def _swiglu(h, wg, wu, wd):
    return (jax.nn.silu(h @ wg) * (h @ wu)) @ wd


def _t5_bucket(dist):
    n = jnp.maximum(dist, 0)
    exact = N_BUCKETS // 2
    nf = jnp.maximum(n, 1).astype(jnp.float32)
    large = exact + (jnp.log(nf / exact) / math.log(MAX_DISTANCE / exact) * (N_BUCKETS - exact)).astype(jnp.int32)
    return jnp.where(n < exact, n, jnp.minimum(large, N_BUCKETS - 1))


def _masked_softmax(logits, mask):
    logits = jnp.where(mask, logits, -1e30)
    m = jnp.max(logits, axis=-1, keepdims=True)
    e = jnp.where(mask, jnp.exp(logits - m), 0.0)
    return e / jnp.maximum(jnp.sum(e, axis=-1, keepdims=True), 1e-30)


def _to_chunks(x, c):
    b, t = x.shape[:2]
    x = x.reshape(b, t // c, c, *x.shape[2:])
    return jnp.moveaxis(jnp.swapaxes(x, 2, 3), 1, 0)


def _from_chunks(x):
    n, b, h, c = x.shape[:4]
    x = jnp.swapaxes(jnp.moveaxis(x, 0, 1), 2, 3)
    return x.reshape(b, n * c, h, *x.shape[4:])


def _causal_conv(u, buf, w):
    full = jnp.concatenate([buf.astype(u.dtype), u], axis=1)
    t = u.shape[1]
    out = full[:, 0:t] * w[0]
    for i in range(1, CONV_W):
        out = out + full[:, i:i + t] * w[i]
    return jax.nn.silu(out), full[:, t:]


def _gated_delta(q, k, v, g, beta, s0):
    t = q.shape[1]
    c = _block_len(t, GDN_CHUNK)
    dv = v.shape[-1]
    causal = jnp.tril(jnp.ones((c, c), bool))
    strict = jnp.tril(jnp.ones((c, c), bool), -1)

    def step(s, inp):
        qc, kc, vc, gcn, bc = inp
        gc = jnp.cumsum(gcn, axis=-1)
        decay = jnp.exp(jnp.where(causal, gc[..., :, None] - gc[..., None, :], -jnp.inf))
        kb = kc * bc[..., None]
        lower = jnp.where(strict, jnp.einsum('bhid,bhjd->bhij', kb, kc) * decay, 0.0)
        rhs = jnp.concatenate([vc * bc[..., None], kb * jnp.exp(gc)[..., None]], axis=-1)
        sol = lax.linalg.triangular_solve(lower, rhs, left_side=True, lower=True, unit_diagonal=True)
        u = sol[..., :dv] - jnp.einsum('bhcd,bhde->bhce', sol[..., dv:], s)
        attn = jnp.einsum('bhid,bhjd->bhij', qc, kc) * decay
        o = jnp.einsum('bhcd,bhde->bhce', qc * jnp.exp(gc)[..., None], s) + jnp.einsum('bhij,bhje->bhie', attn, u)
        g_end = gc[..., -1]
        s = s * jnp.exp(g_end)[..., None, None] + jnp.einsum(
            'bhcd,bhce->bhde', kc * jnp.exp(g_end[..., None] - gc)[..., None], u)
        return s, o

    s, o = lax.scan(step, s0, tuple(_to_chunks(a, c) for a in (q, k, v, g, beta)))
    return _from_chunks(o), s


def _mlstm(q, k, v, log_i, log_f, c0, n0, m0):
    t = q.shape[1]
    c = _block_len(t, MLSTM_CHUNK)
    causal = jnp.tril(jnp.ones((c, c), bool))

    def step(carry, inp):
        cs, ns, ms = carry
        qc, kc, vc, ic, fc = inp
        b = jnp.cumsum(fc, axis=-1)
        a = ic - b
        m = jnp.maximum(b + ms[..., None], b + lax.cummax(a, axis=a.ndim - 1))
        w_inter = jnp.exp(b + ms[..., None] - m)
        dlog = jnp.where(causal, b[..., :, None] + a[..., None, :] - m[..., None], -jnp.inf)
        sc = jnp.einsum('bhtd,bhsd->bhts', qc, kc) * jnp.exp(dlog)
        num = w_inter[..., None] * jnp.einsum('bhtd,bhde->bhte', qc, cs) + jnp.einsum('bhts,bhse->bhte', sc, vc)
        den = w_inter * jnp.einsum('bhtd,bhd->bht', qc, ns) + jnp.sum(sc, axis=-1)
        hc = num / (jnp.maximum(jnp.abs(den), jnp.exp(-m)) + MLSTM_EPS)[..., None]
        m_end = m[..., -1]
        w_end = jnp.exp(b[..., -1:] + a - m_end[..., None])
        d_end = jnp.exp(b[..., -1] + ms - m_end)
        kw = kc * w_end[..., None]
        cs = d_end[..., None, None] * cs + jnp.einsum('bhsd,bhse->bhde', kw, vc)
        ns = d_end[..., None] * ns + jnp.sum(kw, axis=2)
        return (cs, ns, m_end), hc

    (cn, nn, mn), h = lax.scan(step, (c0, n0, m0), tuple(_to_chunks(a, c) for a in (q, k, v, log_i, log_f)))
    return _from_chunks(h), cn, nn, mn


def _stick_breaking(q, k, v, q0):
    b, t, h, d = q.shape
    tk = k.shape[1]
    qb = _block_len(t, Q_BLOCK)
    nb = t // qb
    kpos = jnp.arange(tk)
    kf = k.astype(jnp.float32)
    vf = v.astype(jnp.float32)

    def blk(inp):
        qi, i = inp
        qpos = q0 + i * qb + jnp.arange(qb)
        z = jnp.einsum('bqhd,bkhd->bhqk', qi, kf) * d ** -0.5
        mask = kpos[None, :] < qpos[:, None]
        log_beta = jax.nn.log_sigmoid(z)
        log_rest = jnp.where(mask, jax.nn.log_sigmoid(-z), 0.0)
        after = lax.cumsum(log_rest, axis=3, reverse=True) - log_rest
        att = jnp.where(mask, jnp.exp(log_beta + after), 0.0)
        return jnp.einsum('bhqk,bkhd->bqhd', att, vf)

    qs = jnp.moveaxis(q.astype(jnp.float32).reshape(b, nb, qb, h, d), 1, 0)
    out = lax.map(blk, (qs, jnp.arange(nb)))
    return jnp.moveaxis(out, 0, 1).reshape(b, t, h, d)


def _nsa(q, ck_all, cv_all, sk_all, sv_all, wk_ext, wv_ext, gates, q0, cmp_w, cmp_b, cmp_k_g, rel_bias):
    f32 = jnp.float32
    b, t, h, d = q.shape
    g_, r_ = NSA_KV_HEADS, NSA_GROUP
    scale = d ** -0.5
    tk = ck_all.shape[1]
    qpos = q0 + jnp.arange(t)
    qg = q.astype(f32).reshape(b, t, g_, r_, d)
    bias_g = rel_bias.astype(f32).reshape(N_BUCKETS, g_, r_)

    n_cmp = tk // CMP_BLOCK

    def compress(x, w, bb):
        xb = x[:, :n_cmp * CMP_BLOCK].astype(f32).reshape(b, n_cmp, CMP_BLOCK, g_, d)
        xb = jnp.swapaxes(xb, 2, 3).reshape(b, n_cmp, g_, CMP_BLOCK * d)
        return xb @ w.astype(f32) + bb.astype(f32)

    kc = _rms_norm(compress(ck_all, cmp_w[0], cmp_b[0]), cmp_k_g)
    vc = compress(cv_all, cmp_w[1], cmp_b[1])
    cend = jnp.arange(n_cmp) * CMP_BLOCK + (CMP_BLOCK - 1)
    dist_c = qpos[:, None] - cend[None, :]
    bias_c = jnp.transpose(bias_g[_t5_bucket(dist_c)], (0, 2, 3, 1))
    lc = jnp.einsum('btgrd,bcgd->btgrc', qg, kc) * scale + bias_c
    p_c = _masked_softmax(lc, (dist_c >= 0)[:, None, None, :])
    o_c = jnp.einsum('btgrc,bcgd->btgrd', p_c, vc)

    ratio = SEL_BLOCK // CMP_BLOCK
    n_sel = -(-tk // SEL_BLOCK)
    imp = jnp.sum(p_c, axis=3)
    imp = jnp.pad(imp, ((0, 0), (0, 0), (0, 0), (0, n_sel * ratio - n_cmp)))
    imp = jnp.sum(imp.reshape(b, t, g_, n_sel, ratio), axis=-1)
    blk_id = jnp.arange(n_sel)[None, :]
    cur = (qpos // SEL_BLOCK)[:, None]
    forced = jnp.where(blk_id == cur, 2.0 * FORCE_SCORE, jnp.where(blk_id == 0, FORCE_SCORE, -FORCE_SCORE))
    score = jnp.where(((blk_id < cur) & (blk_id > 0))[:, None, :], imp, forced[:, None, :])
    k_sel = min(TOP_N, n_sel)
    _, idx = lax.top_k(score, k_sel)

    pad_len = n_sel * SEL_BLOCK - tk

    def to_blocks(x):
        x = jnp.pad(x.astype(f32), ((0, 0), (0, pad_len), (0, 0), (0, 0))).reshape(b, n_sel, SEL_BLOCK, g_, d)
        return jnp.transpose(x, (0, 3, 1, 2, 4))

    kb_, vb_ = to_blocks(sk_all), to_blocks(sv_all)
    qb = _block_len(t, Q_BLOCK)
    nb = t // qb
    bi = jnp.arange(b)[:, None, None, None]
    gi = jnp.arange(g_)[None, None, :, None]
    off = jnp.arange(SEL_BLOCK)

    def sel_block(inp):
        q_i, idx_i, qp = inp
        kg = kb_[bi, gi, idx_i]
        vg = vb_[bi, gi, idx_i]
        kpos = idx_i[..., None] * SEL_BLOCK + off
        dist = qp[None, :, None, None, None] - kpos
        bias = jnp.moveaxis(bias_g[_t5_bucket(dist), gi[..., None]], -1, 3)
        logit = jnp.einsum('bqgrd,bqgksd->bqgrks', q_i, kg) * scale + bias
        logit = logit.reshape(b, qb, g_, r_, k_sel * SEL_BLOCK)
        p = _masked_softmax(logit, (dist >= 0).reshape(b, qb, g_, 1, k_sel * SEL_BLOCK))
        return jnp.einsum('bqgrn,bqgnd->bqgrd', p, vg.reshape(b, qb, g_, k_sel * SEL_BLOCK, d))

    q_blocks = jnp.moveaxis(qg.reshape(b, nb, qb, g_, r_, d), 1, 0)
    idx_blocks = jnp.moveaxis(idx.reshape(b, nb, qb, g_, k_sel), 1, 0)
    o_s = lax.map(sel_block, (q_blocks, idx_blocks, qpos.reshape(nb, qb)))
    o_s = jnp.moveaxis(o_s, 0, 1).reshape(b, t, g_, r_, d)

    wb = wk_ext.shape[1] - t
    kw = jnp.pad(wk_ext.astype(f32), ((0, 0), (WINDOW, 0), (0, 0), (0, 0)))
    vw = jnp.pad(wv_ext.astype(f32), ((0, 0), (WINDOW, 0), (0, 0), (0, 0)))
    span = WINDOW + qb

    def win_block(inp):
        q_i, i = inp
        j0 = wb + i * qb
        kk = lax.dynamic_slice_in_dim(kw, j0, span, axis=1)
        vv = lax.dynamic_slice_in_dim(vw, j0, span, axis=1)
        qp = q0 + i * qb + jnp.arange(qb)
        kp = q0 + i * qb - WINDOW + jnp.arange(span)
        dist = qp[:, None] - kp[None, :]
        mask = (dist >= 0) & (dist < WINDOW) & (kp[None, :] >= 0)
        bias = jnp.transpose(bias_g[_t5_bucket(dist)], (0, 2, 3, 1))
        logit = jnp.einsum('bqgrd,bkgd->bqgrk', q_i, kk) * scale + bias
        p = _masked_softmax(logit, mask[:, None, None, :])
        return jnp.einsum('bqgrk,bkgd->bqgrd', p, vv)

    o_w = lax.map(win_block, (q_blocks, jnp.arange(nb)))
    o_w = jnp.moveaxis(o_w, 0, 1).reshape(b, t, g_, r_, d)

    gt = jax.nn.sigmoid(gates.astype(f32)).reshape(b, t, g_, r_, 3)
    o = gt[..., 0:1] * o_c + gt[..., 1:2] * o_s + gt[..., 2:3] * o_w
    return o.reshape(b, t, h * d)


def _even_mixer(h, s0, conv0, c0, n0, m0, w_in, conv_w, a_log, dt_bias, gdn_g, gate_bias, mlstm_g, w_out):
    f32 = jnp.float32
    b, t, _ = h.shape
    (qkv_pre, a_pre, b_pre, gdn_gate, mq, mk, mv, mi, mf, mo) = _split(h @ w_in, EVEN_SPLITS)
    qkv, conv_new = _causal_conv(qkv_pre, conv0, conv_w)
    gq, gk, gv = [a.astype(f32).reshape(b, t, GDN_HEADS, HEAD_DIM) for a in jnp.split(qkv, 3, axis=-1)]
    gq = _l2norm(gq) * HEAD_DIM ** -0.5
    gk = _l2norm(gk)
    beta = jax.nn.sigmoid(b_pre.astype(f32))
    g = -jnp.exp(a_log.astype(f32)) * jax.nn.softplus(a_pre.astype(f32) + dt_bias.astype(f32))
    go, s_new = _gated_delta(gq, gk, gv, g, beta, s0.astype(f32))
    go = _rms_norm(go, gdn_g) * jax.nn.silu(gdn_gate.astype(f32).reshape(b, t, GDN_HEADS, HEAD_DIM))
    def heads(a):
        return a.astype(f32).reshape(b, t, MLSTM_HEADS, HEAD_DIM)
    log_i = mi.astype(f32) + gate_bias[0].astype(f32)
    log_f = jax.nn.log_sigmoid(mf.astype(f32) + gate_bias[1].astype(f32))
    mh, c_new, n_new, m_new = _mlstm(heads(mq), heads(mk) * HEAD_DIM ** -0.5, heads(mv), log_i, log_f,
                                     c0.astype(f32), n0.astype(f32), m0.astype(f32))
    mh = _rms_norm(mh, mlstm_g) * jax.nn.sigmoid(heads(mo))
    mix = jnp.concatenate([go.reshape(b, t, GDN_WIDTH), mh.reshape(b, t, MLSTM_WIDTH)], axis=-1)
    y = mix.astype(h.dtype) @ w_out
    return y, (s_new.astype(s0.dtype), conv_new, c_new.astype(c0.dtype), n_new.astype(n0.dtype), m_new.astype(m0.dtype))


def _odd_mixer(h, past_len, sbk0, sbv0, ck0, cv0, sk0, sv0, wk0, wv0, w_in, q_g, k_g, cmp_w, cmp_b, w_out, rel_bias):
    b, t, _ = h.shape
    (sq, skn, svn, nq, ckn, cvn, selk, selv, wkn, wvn, gates) = _split(h @ w_in, ODD_SPLITS)

    def heads(a, n):
        return a.reshape(b, t, n, HEAD_DIM)

    sb_k, sb_v = heads(skn, SB_HEADS), heads(svn, SB_HEADS)
    sb_o = _stick_breaking(heads(sq, SB_HEADS), jnp.concatenate([sbk0, sb_k], axis=1),
                           jnp.concatenate([sbv0, sb_v], axis=1), past_len)
    c_k, c_v = heads(ckn, NSA_KV_HEADS), heads(cvn, NSA_KV_HEADS)
    s_k, s_v = _rms_norm(heads(selk, NSA_KV_HEADS), k_g[1]), heads(selv, NSA_KV_HEADS)
    w_k, w_v = _rms_norm(heads(wkn, NSA_KV_HEADS), k_g[2]), heads(wvn, NSA_KV_HEADS)
    win_k = jnp.concatenate([wk0, w_k], axis=1)
    win_v = jnp.concatenate([wv0, w_v], axis=1)
    nsa_o = _nsa(_rms_norm(heads(nq, NSA_HEADS), q_g),
                 jnp.concatenate([ck0, c_k], axis=1), jnp.concatenate([cv0, c_v], axis=1),
                 jnp.concatenate([sk0, s_k], axis=1), jnp.concatenate([sv0, s_v], axis=1),
                 win_k, win_v, gates, past_len, cmp_w, cmp_b, k_g[0], rel_bias)
    mix = jnp.concatenate([sb_o.reshape(b, t, SB_WIDTH), nsa_o], axis=-1)
    y = mix.astype(h.dtype) @ w_out
    keep = min(WINDOW, win_k.shape[1])
    return y, (sb_k, sb_v, c_k, c_v, s_k, s_v, win_k[:, -keep:], win_v[:, -keep:])


def _trunk(x, past_len, even_states, odd_states, w):
    new_even, new_odd = [], []
    for layer in range(DEPTH):
        x = x + 0.5 * _swiglu(_rms_norm(x, w['norm_g'][layer, 0]), w['ffn_w_gate'][layer, 0],
                              w['ffn_w_up'][layer, 0], w['ffn_w_down'][layer, 0])
        h = _rms_norm(x, w['norm_g'][layer, 1])
        j = layer // 2
        if layer % 2 == 0:
            y, st = _even_mixer(h, *even_states[j], w['even_w_in'][j], w['gdn_conv_w'][j], w['gdn_a_log'][j],
                                w['gdn_dt_bias'][j], w['gdn_norm_g'][j], w['mlstm_gate_bias'][j],
                                w['mlstm_norm_g'][j], w['even_w_out'][j])
            new_even.append(st)
        else:
            y, st = _odd_mixer(h, past_len, *odd_states[j], w['odd_w_in'][j], w['nsa_q_norm_g'][j],
                               w['nsa_k_norm_g'][j], w['nsa_cmp_w'][j], w['nsa_cmp_b'][j], w['odd_w_out'][j],
                               w['rel_bias'])
            new_odd.append(st)
        x = x + y
        x = x + 0.5 * _swiglu(_rms_norm(x, w['norm_g'][layer, 2]), w['ffn_w_gate'][layer, 1],
                              w['ffn_w_up'][layer, 1], w['ffn_w_down'][layer, 1])
    return x, new_even, new_odd


def _gather_pages(pool, page_table):
    rows = pool[page_table]
    return rows.reshape(rows.shape[0], rows.shape[1] * rows.shape[2], *rows.shape[3:])


def _stack(states, i):
    return jnp.stack([s[i] for s in states])


def setup_inputs(seed: int = 0) -> dict:
    key = jax.random.key(seed)
    keys = iter([jax.random.fold_in(key, i) for i in range(64)])

    def nrm(shape, scale=1.0):
        return jax.random.normal(next(keys), shape, jnp.float32) * scale

    def gain(shape):
        return 1.0 + nrm(shape, 0.02)

    n_pages = PAST_LEN // PAGE_SIZE
    n_used = DEC_BATCH * n_pages
    n_pool = n_used + (n_used + 3) // 4
    win_buf = min(WINDOW, PAST_LEN)
    perm = jax.random.permutation(next(keys), n_pool)
    page_table = perm[:n_used].reshape(DEC_BATCH, n_pages).astype(jnp.int32)
    dt = jnp.exp(jax.random.uniform(next(keys), (N_EVEN, GDN_HEADS), jnp.float32, math.log(1e-3), math.log(1e-1)))
    a_log = jnp.log(jax.random.uniform(next(keys), (N_EVEN, GDN_HEADS), jnp.float32, 1.0, 16.0))
    f_bias = jnp.linspace(3.0, 6.0, MLSTM_HEADS)[None, :] + nrm((N_EVEN, MLSTM_HEADS), 0.1)
    i_bias = nrm((N_EVEN, MLSTM_HEADS), 0.1)
    paged_sb = (N_ODD, n_pool, PAGE_SIZE, SB_HEADS, HEAD_DIM)
    paged_kv = (N_ODD, n_pool, PAGE_SIZE, NSA_KV_HEADS, HEAD_DIM)
    win = (N_ODD, DEC_BATCH, win_buf, NSA_KV_HEADS, HEAD_DIM)
    return {
        'x_prompt': nrm((BATCH, SEQ, D_MODEL)),
        'x_sample': nrm((DEC_BATCH, DEC_SEQ, D_MODEL)),
        'state_gdn_s': nrm((N_EVEN, DEC_BATCH, GDN_HEADS, HEAD_DIM, HEAD_DIM), 0.1),
        'state_gdn_conv': nrm((N_EVEN, DEC_BATCH, CONV_W - 1, 3 * GDN_WIDTH)),
        'state_mlstm_c': nrm((N_EVEN, DEC_BATCH, MLSTM_HEADS, HEAD_DIM, HEAD_DIM), 0.1),
        'state_mlstm_n': nrm((N_EVEN, DEC_BATCH, MLSTM_HEADS, HEAD_DIM), 0.1),
        'state_mlstm_m': nrm((N_EVEN, DEC_BATCH, MLSTM_HEADS)),
        'cache_sb_k': nrm(paged_sb),
        'cache_sb_v': nrm(paged_sb),
        'cache_nsa_cmp_k': nrm(paged_kv),
        'cache_nsa_cmp_v': nrm(paged_kv),
        'cache_nsa_sel_k': nrm(paged_kv),
        'cache_nsa_sel_v': nrm(paged_kv),
        'state_nsa_win_k': nrm(win),
        'state_nsa_win_v': nrm(win),
        'page_table': page_table,
        'norm_g': gain((DEPTH, 3, D_MODEL)),
        'ffn_w_gate': nrm((DEPTH, 2, D_MODEL, FFN_DIM), D_MODEL ** -0.5),
        'ffn_w_up': nrm((DEPTH, 2, D_MODEL, FFN_DIM), D_MODEL ** -0.5),
        'ffn_w_down': nrm((DEPTH, 2, FFN_DIM, D_MODEL), FFN_DIM ** -0.5),
        'even_w_in': nrm((N_EVEN, D_MODEL, EVEN_PROJ), D_MODEL ** -0.5),
        'gdn_conv_w': nrm((N_EVEN, CONV_W, 3 * GDN_WIDTH), CONV_W ** -0.5),
        'gdn_a_log': a_log,
        'gdn_dt_bias': dt + jnp.log(-jnp.expm1(-dt)),
        'gdn_norm_g': gain((N_EVEN, HEAD_DIM)),
        'mlstm_gate_bias': jnp.stack([i_bias, f_bias], axis=1),
        'mlstm_norm_g': gain((N_EVEN, HEAD_DIM)),
        'even_w_out': nrm((N_EVEN, MIX_WIDTH, D_MODEL), MIX_WIDTH ** -0.5),
        'odd_w_in': nrm((N_ODD, D_MODEL, ODD_PROJ), D_MODEL ** -0.5),
        'nsa_q_norm_g': gain((N_ODD, HEAD_DIM)),
        'nsa_k_norm_g': gain((N_ODD, 3, HEAD_DIM)),
        'nsa_cmp_w': nrm((N_ODD, 2, CMP_BLOCK * HEAD_DIM, HEAD_DIM), (CMP_BLOCK * HEAD_DIM) ** -0.5),
        'nsa_cmp_b': nrm((N_ODD, 2, HEAD_DIM), 0.02),
        'odd_w_out': nrm((N_ODD, MIX_WIDTH, D_MODEL), MIX_WIDTH ** -0.5),
        'rel_bias': nrm((N_BUCKETS, NSA_HEADS), 0.5),
    }


def reference(x_prompt, x_sample, state_gdn_s, state_gdn_conv, state_mlstm_c, state_mlstm_n, state_mlstm_m,
              cache_sb_k, cache_sb_v, cache_nsa_cmp_k, cache_nsa_cmp_v, cache_nsa_sel_k, cache_nsa_sel_v,
              state_nsa_win_k, state_nsa_win_v, page_table, norm_g, ffn_w_gate, ffn_w_up, ffn_w_down,
              even_w_in, gdn_conv_w, gdn_a_log, gdn_dt_bias, gdn_norm_g, mlstm_gate_bias, mlstm_norm_g, even_w_out,
              odd_w_in, nsa_q_norm_g, nsa_k_norm_g, nsa_cmp_w, nsa_cmp_b, odd_w_out, rel_bias):
    w = dict(norm_g=norm_g, ffn_w_gate=ffn_w_gate, ffn_w_up=ffn_w_up, ffn_w_down=ffn_w_down,
             even_w_in=even_w_in, gdn_conv_w=gdn_conv_w, gdn_a_log=gdn_a_log, gdn_dt_bias=gdn_dt_bias,
             gdn_norm_g=gdn_norm_g, mlstm_gate_bias=mlstm_gate_bias, mlstm_norm_g=mlstm_norm_g,
             even_w_out=even_w_out, odd_w_in=odd_w_in, nsa_q_norm_g=nsa_q_norm_g, nsa_k_norm_g=nsa_k_norm_g,
             nsa_cmp_w=nsa_cmp_w, nsa_cmp_b=nsa_cmp_b, odd_w_out=odd_w_out, rel_bias=rel_bias)
    f32 = jnp.float32
    bp = x_prompt.shape[0]
    dtp = x_prompt.dtype
    ev_zero = (jnp.zeros((bp, GDN_HEADS, HEAD_DIM, HEAD_DIM), f32), jnp.zeros((bp, CONV_W - 1, 3 * GDN_WIDTH), dtp),
               jnp.zeros((bp, MLSTM_HEADS, HEAD_DIM, HEAD_DIM), f32), jnp.zeros((bp, MLSTM_HEADS, HEAD_DIM), f32),
               jnp.zeros((bp, MLSTM_HEADS), f32))
    sb_empty = jnp.zeros((bp, 0, SB_HEADS, HEAD_DIM), dtp)
    kv_empty = jnp.zeros((bp, 0, NSA_KV_HEADS, HEAD_DIM), dtp)
    od_zero = (sb_empty, sb_empty) + (kv_empty,) * 6
    y_prompt, ev_p, od_p = _trunk(x_prompt, 0, [ev_zero] * N_EVEN, [od_zero] * N_ODD, w)
    past_len = page_table.shape[1] * cache_sb_k.shape[2]
    ev_in = [(state_gdn_s[j], state_gdn_conv[j], state_mlstm_c[j], state_mlstm_n[j], state_mlstm_m[j])
             for j in range(N_EVEN)]
    od_in = [(_gather_pages(cache_sb_k[j], page_table), _gather_pages(cache_sb_v[j], page_table),
              _gather_pages(cache_nsa_cmp_k[j], page_table), _gather_pages(cache_nsa_cmp_v[j], page_table),
              _gather_pages(cache_nsa_sel_k[j], page_table), _gather_pages(cache_nsa_sel_v[j], page_table),
              state_nsa_win_k[j], state_nsa_win_v[j]) for j in range(N_ODD)]
    y_sample, ev_s, od_s = _trunk(x_sample, past_len, ev_in, od_in, w)

    gdn_s_p, gdn_s_s = _stack(ev_p, 0), _stack(ev_s, 0)
    gdn_conv_p, gdn_conv_s = _stack(ev_p, 1), _stack(ev_s, 1)
    mlstm_c_p, mlstm_c_s = _stack(ev_p, 2), _stack(ev_s, 2)
    mlstm_n_p, mlstm_n_s = _stack(ev_p, 3), _stack(ev_s, 3)
    mlstm_m_p, mlstm_m_s = _stack(ev_p, 4), _stack(ev_s, 4)
    sb_k_p, sb_k_s = _stack(od_p, 0), _stack(od_s, 0)
    sb_v_p, sb_v_s = _stack(od_p, 1), _stack(od_s, 1)
    cmp_k_p, cmp_k_s = _stack(od_p, 2), _stack(od_s, 2)
    cmp_v_p, cmp_v_s = _stack(od_p, 3), _stack(od_s, 3)
    sel_k_p, sel_k_s = _stack(od_p, 4), _stack(od_s, 4)
    sel_v_p, sel_v_s = _stack(od_p, 5), _stack(od_s, 5)
    win_k_p, win_k_s = _stack(od_p, 6), _stack(od_s, 6)
    win_v_p, win_v_s = _stack(od_p, 7), _stack(od_s, 7)
    return (y_prompt, y_sample, gdn_s_p, gdn_s_s, gdn_conv_p, gdn_conv_s, mlstm_c_p, mlstm_c_s,
            mlstm_n_p, mlstm_n_s, mlstm_m_p, mlstm_m_s, sb_k_p, sb_k_s, sb_v_p, sb_v_s,
            cmp_k_p, cmp_k_s, cmp_v_p, cmp_v_s, sel_k_p, sel_k_s, sel_v_p, sel_v_s,
            win_k_p, win_k_s, win_v_p, win_v_s)
```

```python
import functools
import math

import jax
import jax.numpy as jnp
import numpy as np
from jax import lax
from jax.experimental import pallas as pl
from jax.experimental.pallas import tpu as pltpu

f32 = jnp.float32
bf16 = jnp.bfloat16
i32 = jnp.int32

HD = 128
NH = 8
NKV = 2
NREP = 4
RMS_EPS = 1e-6
MLSTM_EPS = 1e-6
CHUNK = 128
PAGE = 128
CMP_BLOCK = 32
SEL_BLOCK = 64
TOP_N = 8
WINDOW = 512
FORCE_SCORE = 1.0e4
N_BUCKETS = 32
MAX_DISTANCE = 128
NEG = -1e30
VMEM_LIMIT = 56 * 1024 * 1024

NN = (((1,), (0,)), ((), ()))
NT = (((1,), (1,)), ((), ()))
TN = (((0,), (0,)), ((), ()))


def _cparams(n_axes):
    return pltpu.CompilerParams(dimension_semantics=("arbitrary",) * n_axes, vmem_limit_bytes=VMEM_LIMIT)


def _mm(a, b, dims=NN):
    return lax.dot_general(a.astype(bf16), b.astype(bf16), dims, preferred_element_type=f32)


def _split3(a):
    a0 = a.astype(bf16)
    r = a - a0.astype(f32)
    a1 = r.astype(bf16)
    a2 = (r - a1.astype(f32)).astype(bf16)
    return a0, a1, a2


def _mm_exact_b(a, b01, dims=NN):
    a0, a1, a2 = _split3(a)
    b01 = b01.astype(bf16)
    d = lambda x: lax.dot_general(x, b01, dims, preferred_element_type=f32)
    return d(a0) + d(a1) + d(a2)


def _mm_exact_a(a01, b, dims=NN):
    b0, b1, b2 = _split3(b)
    a01 = a01.astype(bf16)
    d = lambda x: lax.dot_general(a01, x, dims, preferred_element_type=f32)
    return d(b0) + d(b1) + d(b2)


def _mm3(a, b, dims=NN):
    ah = a.astype(bf16)
    al = (a - ah.astype(f32)).astype(bf16)
    bh = b.astype(bf16)
    bl = (b - bh.astype(f32)).astype(bf16)
    d = lambda x, y: lax.dot_general(x, y, dims, preferred_element_type=f32)
    return d(ah, bh) + d(ah, bl) + d(al, bh)


def _softplus(x):
    return jnp.maximum(x, 0.0) + jnp.log1p(jnp.exp(-jnp.abs(x)))


def _log_sigmoid(x):
    return -_softplus(-x)


def _rms(x, g):
    return x * lax.rsqrt(jnp.mean(x * x, axis=-1, keepdims=True) + RMS_EPS) * g


def _pad_rows(x, rows, value=0.0):
    if x.shape[0] == rows:
        return x
    return jnp.concatenate([x, jnp.full((rows - x.shape[0],) + x.shape[1:], value, x.dtype)], axis=0)


def _iota2(shape, axis):
    return lax.broadcasted_iota(i32, shape, axis)


def _ffn_body(x_ref, g_ref, wg_ref, wu_ref, wd_ref, o_ref, h_ref, *, nf):
    f = pl.program_id(1)

    @pl.when(f == 0)
    def _():
        h_ref[...] = _rms(x_ref[...], g_ref[...]).astype(bf16)
        o_ref[...] = jnp.zeros_like(o_ref)

    h = h_ref[...]
    a = jnp.dot(h, wg_ref[...], preferred_element_type=f32)
    u = jnp.dot(h, wu_ref[...], preferred_element_type=f32)
    act = (a * jax.nn.sigmoid(a) * u).astype(bf16)
    o_ref[...] += jnp.dot(act, wd_ref[...], preferred_element_type=f32)

    @pl.when(f == nf - 1)
    def _():
        o_ref[...] = x_ref[...] + 0.5 * o_ref[...]


def _ffn(x, g, wg, wu, wd, *, tm=512, tf=512):
    m, d = x.shape
    fdim = wg.shape[1]
    nf = fdim // tf
    return pl.pallas_call(
        functools.partial(_ffn_body, nf=nf),
        out_shape=jax.ShapeDtypeStruct((m, d), f32),
        grid=(m // tm, nf),
        in_specs=[
            pl.BlockSpec((tm, d), lambda i, f: (i, 0)),
            pl.BlockSpec((1, d), lambda i, f: (0, 0)),
            pl.BlockSpec((d, tf), lambda i, f: (0, f)),
            pl.BlockSpec((d, tf), lambda i, f: (0, f)),
            pl.BlockSpec((tf, d), lambda i, f: (f, 0)),
        ],
        out_specs=pl.BlockSpec((tm, d), lambda i, f: (i, 0)),
        scratch_shapes=[pltpu.VMEM((tm, d), bf16)],
        compiler_params=_cparams(2),
        name="ffn",
    )(x, g.reshape(1, d), wg, wu, wd)


def _rmsmm_body(x_ref, g_ref, w_ref, o_ref, h_ref):
    @pl.when(pl.program_id(1) == 0)
    def _():
        h_ref[...] = _rms(x_ref[...], g_ref[...]).astype(bf16)

    o_ref[...] = jnp.dot(h_ref[...], w_ref[...], preferred_element_type=f32)


def _rms_matmul(x, g, w, *, tm=512, tn=640):
    m, d = x.shape
    n = w.shape[1]
    return pl.pallas_call(
        _rmsmm_body,
        out_shape=jax.ShapeDtypeStruct((m, n), f32),
        grid=(m // tm, n // tn),
        in_specs=[
            pl.BlockSpec((tm, d), lambda i, j: (i, 0)),
            pl.BlockSpec((1, d), lambda i, j: (0, 0)),
            pl.BlockSpec((d, tn), lambda i, j: (0, j)),
        ],
        out_specs=pl.BlockSpec((tm, tn), lambda i, j: (i, j)),
        scratch_shapes=[pltpu.VMEM((tm, d), bf16)],
        compiler_params=_cparams(2),
        name="rms_matmul",
    )(x, g.reshape(1, d), w)


def _outproj_body(a_ref, w_ref, r_ref, o_ref):
    o_ref[...] = r_ref[...] + jnp.dot(a_ref[...], w_ref[...], preferred_element_type=f32)


def _out_proj(a, w, res, *, tm=512, tn=512):
    m, k = a.shape
    n = w.shape[1]
    return pl.pallas_call(
        _outproj_body,
        out_shape=jax.ShapeDtypeStruct((m, n), f32),
        grid=(m // tm, n // tn),
        in_specs=[
            pl.BlockSpec((tm, k), lambda i, j: (i, 0)),
            pl.BlockSpec((k, tn), lambda i, j: (0, j)),
            pl.BlockSpec((tm, tn), lambda i, j: (i, j)),
        ],
        out_specs=pl.BlockSpec((tm, tn), lambda i, j: (i, j)),
        compiler_params=_cparams(2),
        name="out_proj",
    )(a, w, res)


def _tri_masks(c):
    row = _iota2((c, c), 0)
    col = _iota2((c, c), 1)
    return row, col


def _gdn_body(*refs, c_in, n_chunks, levels, has_state, first_chunk_zero_prev):
    if has_state:
        (u_ref, p8_ref, gate_ref, sm_ref, cw_ref, alog_ref, dtb_ref, ng_ref, s0_ref, o_ref, so_ref, s_scr) = refs
    else:
        (u_ref, p8_ref, gate_ref, sm_ref, cw_ref, alog_ref, dtb_ref, ng_ref, o_ref, so_ref, s_scr) = refs
    c = pl.program_id(1)
    C = CHUNK

    @pl.when(c == 0)
    def _():
        if has_state:
            s_scr[...] = s0_ref[...]
        else:
            s_scr[...] = jnp.zeros_like(s_scr)

    u = u_ref[...]
    p8 = p8_ref[...]
    if first_chunk_zero_prev:
        p8 = jnp.where(c == 0, 0.0, p8)
    cw = cw_ref[...]
    row8 = _iota2((8, u.shape[1]), 0)
    acc = u * cw[3:4]
    for k in range(1, 4):
        rolled = pltpu.roll(u, k, axis=0)
        first8 = jnp.where(row8 < k, pltpu.roll(p8, k, axis=0), rolled[0:8])
        sh = first8 if c_in == 8 else jnp.concatenate([first8, rolled[8:]], axis=0)
        acc = acc + sh * cw[3 - k:4 - k]
    qkv = acc * jax.nn.sigmoid(acc)

    sm = sm_ref[...]
    gfull = _pad_rows(-jnp.exp(alog_ref[...]) * _softplus(sm + dtb_ref[...]), C)
    beta = _pad_rows(jax.nn.sigmoid(sm), C)
    row, col = _tri_masks(C)
    causal = col <= row
    strict = col < row
    eye = jnp.where(row == col, 1.0, 0.0).astype(f32)
    gc = _mm_exact_a(jnp.where(causal, 1.0, 0.0), gfull)
    gate = gate_ref[...]
    ng = ng_ref[...]

    for h in range(NH):
        q = _pad_rows(qkv[:, h * HD:(h + 1) * HD], C)
        k = _pad_rows(qkv[:, (NH + h) * HD:(NH + h + 1) * HD], C)
        v = _pad_rows(qkv[:, (2 * NH + h) * HD:(2 * NH + h + 1) * HD], C)
        qn = q * lax.rsqrt(jnp.sum(q * q, axis=-1, keepdims=True) + 1e-6) * (HD ** -0.5)
        kn = k * lax.rsqrt(jnp.sum(k * k, axis=-1, keepdims=True) + 1e-6)
        gch = gc[:, h:h + 1]
        bh = beta[:, NH + h:NH + h + 1]
        gcb = jnp.broadcast_to(gch, (C, C))
        diff = gcb - jnp.transpose(gcb)
        decay = jnp.where(causal, jnp.exp(jnp.where(causal, diff, 0.0)), 0.0)
        eg = jnp.exp(gch)
        kb = kn * bh
        a_mat = jnp.where(strict, _mm(kb, kn, NT) * decay, 0.0)
        x_inv = eye - a_mat
        p = a_mat
        for _ in range(levels - 1):
            p = _mm3(p, p)
            x_inv = x_inv + _mm3(x_inv, p)
        sol = _mm3(x_inv, jnp.concatenate([v * bh, kb * eg], axis=1))
        s = s_scr[h]
        uu = sol[:, :HD] - _mm(sol[:, HD:], s)
        attn = _mm(qn, kn, NT) * decay
        o = _mm(qn * eg, s) + _mm(attn, uu)
        g_end = gch[C - 1:C, :]
        s_scr[h] = s * jnp.exp(g_end) + _mm(kn * jnp.exp(g_end - gch), uu, TN)
        gt = gate[:, h * HD:(h + 1) * HD]
        y = _rms(o[:c_in], ng) * (gt * jax.nn.sigmoid(gt))
        o_ref[:, h * HD:(h + 1) * HD] = y.astype(bf16)

    @pl.when(c == n_chunks - 1)
    def _():
        so_ref[...] = s_scr[...]


def _gdn(z, row_off, nb, t, conv_w, a_log, dt_bias, norm_g, s0=None, prev8=None):
    c_in = min(t, CHUNK)
    n_chunks = t // c_in
    has_state = s0 is not None
    levels = max(1, int(math.log2(c_in)))
    rb = row_off // c_in
    w3 = 3 * NH * HD
    if prev8 is None:
        prev_arr = z
        prev_spec = pl.BlockSpec((8, w3), lambda b, c: (jnp.maximum((row_off + b * t + c * c_in) // 8 - 1, 0), 0))
    else:
        prev_arr = prev8
        prev_spec = pl.BlockSpec((None, 8, w3), lambda b, c: (b, 0, 0))
    pad128 = lambda v: jnp.zeros((1, HD), f32).at[0, :NH].set(v.astype(f32))
    in_specs = [
        pl.BlockSpec((c_in, w3), lambda b, c: (rb + b * n_chunks + c, 0)),
        prev_spec,
        pl.BlockSpec((c_in, NH * HD), lambda b, c: (rb + b * n_chunks + c, 3)),
        pl.BlockSpec((c_in, HD), lambda b, c: (rb + b * n_chunks + c, 64)),
        pl.BlockSpec((4, w3), lambda b, c: (0, 0)),
        pl.BlockSpec((1, HD), lambda b, c: (0, 0)),
        pl.BlockSpec((1, HD), lambda b, c: (0, 0)),
        pl.BlockSpec((1, HD), lambda b, c: (0, 0)),
    ]
    args = [z, prev_arr, z, z, conv_w, pad128(a_log), pad128(dt_bias), norm_g.reshape(1, HD)]
    if has_state:
        in_specs.append(pl.BlockSpec((None, NH, HD, HD), lambda b, c: (b, 0, 0, 0)))
        args.append(s0)
    return pl.pallas_call(
        functools.partial(_gdn_body, c_in=c_in, n_chunks=n_chunks, levels=levels, has_state=has_state,
                          first_chunk_zero_prev=prev8 is None),
        out_shape=(jax.ShapeDtypeStruct((nb * t, NH * HD), bf16), jax.ShapeDtypeStruct((nb, NH, HD, HD), f32)),
        grid=(nb, n_chunks),
        in_specs=in_specs,
        out_specs=(pl.BlockSpec((c_in, NH * HD), lambda b, c: (b * n_chunks + c, 0)),
                   pl.BlockSpec((None, NH, HD, HD), lambda b, c: (b, 0, 0, 0))),
        scratch_shapes=[pltpu.VMEM((NH, HD, HD), f32)],
        compiler_params=_cparams(2),
        name="gdn",
    )(*args)


def _mlstm_body(*refs, c_in, n_chunks, has_state):
    if has_state:
        (q_ref, k_ref, v_ref, og_ref, sm_ref, ib_ref, fb_ref, ng_ref, c0_ref, n0_ref, m0_ref,
         o_ref, co_ref, no_ref, mo_ref, c_scr, n_scr, m_scr) = refs
    else:
        (q_ref, k_ref, v_ref, og_ref, sm_ref, ib_ref, fb_ref, ng_ref,
         o_ref, co_ref, no_ref, mo_ref, c_scr, n_scr, m_scr) = refs
    c = pl.program_id(1)
    C = CHUNK

    @pl.when(c == 0)
    def _():
        if has_state:
            c_scr[...] = c0_ref[...]
            n_scr[...] = n0_ref[...]
            m_scr[...] = m0_ref[...]
        else:
            c_scr[...] = jnp.zeros_like(c_scr)
            n_scr[...] = jnp.zeros_like(n_scr)
            m_scr[...] = jnp.zeros_like(m_scr)

    sm = sm_ref[...]
    log_i = _pad_rows(sm + ib_ref[...], C, NEG)
    log_f = _pad_rows(_log_sigmoid(sm + fb_ref[...]), C)
    row, col = _tri_masks(C)
    causal = col <= row
    bcum = _mm_exact_a(jnp.where(causal, 1.0, 0.0), log_f)
    ng = ng_ref[...]
    qa, ka, va, oga = q_ref[...], k_ref[...], v_ref[...], og_ref[...]

    for h in range(NH):
        sl = slice(h * HD, (h + 1) * HD)
        q = _pad_rows(qa[:, sl], C)
        k = _pad_rows(ka[:, sl], C) * (HD ** -0.5)
        v = _pad_rows(va[:, sl], C)
        b = bcum[:, 3 * NH + h:3 * NH + h + 1]
        a = log_i[:, 2 * NH + h:2 * NH + h + 1] - b
        arow = jnp.transpose(jnp.broadcast_to(a, (C, C)))
        cm = jnp.max(jnp.where(causal, arow, -jnp.inf), axis=-1, keepdims=True)
        ms = m_scr[h:h + 1, 0:1]
        m = jnp.maximum(b + ms, b + cm)
        w_inter = jnp.exp(b + ms - m)
        dmat = jnp.where(causal, jnp.exp(jnp.where(causal, b + arow - m, 0.0)), 0.0)
        sc = _mm(q, k, NT) * dmat
        cs = c_scr[h]
        ns = n_scr[h:h + 1, :]
        num = w_inter * _mm(q, cs) + _mm(sc, v)
        den = w_inter * jnp.sum(q * ns, axis=-1, keepdims=True) + jnp.sum(sc, axis=-1, keepdims=True)
        hc = num / (jnp.maximum(jnp.abs(den), jnp.exp(-m)) + MLSTM_EPS)
        m_end = m[C - 1:C, :]
        b_end = b[C - 1:C, :]
        w_end = jnp.exp(b_end + a - m_end)
        d_end = jnp.exp(b_end + ms - m_end)
        kw = k * w_end
        c_scr[h] = d_end * cs + _mm(kw, v, TN)
        n_scr[h:h + 1, :] = d_end * ns + jnp.sum(kw, axis=0, keepdims=True)
        m_scr[h:h + 1, :] = jnp.broadcast_to(m_end, (1, HD))
        og = oga[:, sl]
        o_ref[:, sl] = (_rms(hc[:c_in], ng) * jax.nn.sigmoid(og)).astype(bf16)

    @pl.when(c == n_chunks - 1)
    def _():
        co_ref[...] = c_scr[...]
        no_ref[...] = n_scr[...]
        mo_ref[...] = m_scr[...]


def _mlstm(z, row_off, nb, t, gate_bias, norm_g, c0=None, n0=None, m0=None):
    c_in = min(t, CHUNK)
    n_chunks = t // c_in
    has_state = c0 is not None
    rb = row_off // c_in
    w = NH * HD
    ib = jnp.zeros((1, HD), f32).at[0, 2 * NH:3 * NH].set(gate_bias[0].astype(f32))
    fb = jnp.zeros((1, HD), f32).at[0, 3 * NH:4 * NH].set(gate_bias[1].astype(f32))
    zspec = lambda j: pl.BlockSpec((c_in, w), lambda b, c: (rb + b * n_chunks + c, j))
    in_specs = [zspec(4), zspec(5), zspec(6), zspec(7),
                pl.BlockSpec((c_in, HD), lambda b, c: (rb + b * n_chunks + c, 64)),
                pl.BlockSpec((1, HD), lambda b, c: (0, 0)),
                pl.BlockSpec((1, HD), lambda b, c: (0, 0)),
                pl.BlockSpec((1, HD), lambda b, c: (0, 0))]
    args = [z, z, z, z, z, ib, fb, norm_g.reshape(1, HD)]
    if has_state:
        in_specs += [pl.BlockSpec((None, NH, HD, HD), lambda b, c: (b, 0, 0, 0)),
                     pl.BlockSpec((None, NH, HD), lambda b, c: (b, 0, 0)),
                     pl.BlockSpec((None, NH, HD), lambda b, c: (b, 0, 0))]
        args += [c0, n0, jnp.broadcast_to(m0[..., None], m0.shape + (HD,))]
    return pl.pallas_call(
        functools.partial(_mlstm_body, c_in=c_in, n_chunks=n_chunks, has_state=has_state),
        out_shape=(jax.ShapeDtypeStruct((nb * t, w), bf16), jax.ShapeDtypeStruct((nb, NH, HD, HD), f32),
                   jax.ShapeDtypeStruct((nb, NH, HD), f32), jax.ShapeDtypeStruct((nb, NH, HD), f32)),
        grid=(nb, n_chunks),
        in_specs=in_specs,
        out_specs=(pl.BlockSpec((c_in, w), lambda b, c: (b * n_chunks + c, 0)),
                   pl.BlockSpec((None, NH, HD, HD), lambda b, c: (b, 0, 0, 0)),
                   pl.BlockSpec((None, NH, HD), lambda b, c: (b, 0, 0)),
                   pl.BlockSpec((None, NH, HD), lambda b, c: (b, 0, 0))),
        scratch_shapes=[pltpu.VMEM((NH, HD, HD), f32), pltpu.VMEM((NH, HD), f32), pltpu.VMEM((NH, HD), f32)],
        compiler_params=_cparams(2),
        name="mlstm",
    )(*args)


def _even_perm():
    w = NH * HD
    offs = np.cumsum([0, 3 * w, NH, NH, w, w, w, w, NH, NH, w])
    seg = lambda i: np.arange(offs[i], offs[i + 1])
    return np.concatenate([seg(0), seg(3), seg(4), seg(5), seg(6), seg(9), seg(1), seg(2), seg(7), seg(8)])


ODD_W = 5760
SCALE = HD ** -0.5


def _kvnorm_body(sk_ref, wk_ref, g_ref, so_ref, wo_ref):
    g = g_ref[...]
    for src, dst, gi in ((sk_ref, so_ref, 1), (wk_ref, wo_ref, 2)):
        x = src[...]
        for j in range(NKV):
            dst[:, j * HD:(j + 1) * HD] = _rms(x[:, j * HD:(j + 1) * HD], g[gi:gi + 1])


def _kvnorm(z, k_g, *, tm=1024):
    m = z.shape[0]
    w = NKV * HD
    return pl.pallas_call(
        _kvnorm_body,
        out_shape=(jax.ShapeDtypeStruct((m, w), f32), jax.ShapeDtypeStruct((m, w), f32)),
        grid=(m // tm,),
        in_specs=[pl.BlockSpec((tm, w), lambda i: (i, 18)), pl.BlockSpec((tm, w), lambda i: (i, 20)),
                  pl.BlockSpec((3, HD), lambda i: (0, 0))],
        out_specs=(pl.BlockSpec((tm, w), lambda i: (i, 0)), pl.BlockSpec((tm, w), lambda i: (i, 0))),
        compiler_params=_cparams(1),
        name="kvnorm",
    )(z, z, k_g)


def _sb_terms(z, mask, mstrict, r):
    sp = jnp.log1p(jnp.exp(-jnp.abs(z)))
    log_beta = jnp.minimum(z, 0.0) - sp
    log_rest = -(jnp.maximum(z, 0.0) + sp)
    if mask is not None:
        log_rest = jnp.where(mask, log_rest, 0.0)
    after = _mm_exact_b(log_rest, mstrict)
    att = jnp.exp(log_beta + after + r)
    if mask is not None:
        att = jnp.where(mask, att, 0.0)
    return att, r + jnp.sum(log_rest, axis=-1, keepdims=True)


def _sbp_body(q_ref, k_ref, v_ref, o_ref):
    i = pl.program_id(2)
    q = q_ref[...]
    row = _iota2((HD, HD), 0)
    col = _iota2((HD, HD), 1)
    mstrict = jnp.where(row > col, 1.0, 0.0).astype(bf16)

    def step(s, carry):
        acc, r = carry
        j = i - s
        off = pl.multiple_of(j * HD, HD)
        z = _mm(q, k_ref[pl.ds(off, HD), :], NT) * SCALE
        mask = (col + j * HD) < (row + i * HD)
        att, r = _sb_terms(z, mask, mstrict, r)
        return acc + _mm(att, v_ref[pl.ds(off, HD), :]), r

    acc, _ = lax.fori_loop(0, i + 1, step, (jnp.zeros((HD, HD), f32), jnp.zeros((HD, 1), f32)))
    o_ref[...] = acc.astype(bf16)


def _sb_prompt(z, nb, t):
    nq = t // HD
    return pl.pallas_call(
        _sbp_body,
        out_shape=jax.ShapeDtypeStruct((nb * t, NH * HD), bf16),
        grid=(nb, NH, nq),
        in_specs=[pl.BlockSpec((HD, HD), lambda b, h, i: (b * nq + i, h)),
                  pl.BlockSpec((t, HD), lambda b, h, i: (b, NH + h)),
                  pl.BlockSpec((t, HD), lambda b, h, i: (b, 2 * NH + h))],
        out_specs=pl.BlockSpec((HD, HD), lambda b, h, i: (b * nq + i, h)),
        compiler_params=_cparams(3),
        name="sb_prompt",
    )(z, z, z)


def _sbs_body(pt_ref, q_ref, kn_ref, vn_ref, kp_ref, vp_ref, o_ref, acc_scr, r_scr, *, n_steps, t):
    del pt_ref
    s = pl.program_id(1)

    @pl.when(s == 0)
    def _():
        acc_scr[...] = jnp.zeros_like(acc_scr)
        r_scr[...] = jnp.zeros_like(r_scr)

    rows = NH * t
    row = _iota2((rows, HD), 0)
    col = _iota2((rows, HD), 1)
    mrow = _iota2((HD, HD), 0)
    mcol = _iota2((HD, HD), 1)
    mstrict = jnp.where(mrow > mcol, 1.0, 0.0).astype(bf16)

    def process(get_k, get_v, mask):
        qa = q_ref[...]
        z = jnp.concatenate([_mm(qa[:, h * HD:(h + 1) * HD], get_k(h), NT) for h in range(NH)], axis=0) * SCALE
        att, r = _sb_terms(z, mask, mstrict, r_scr[:, 0:1])
        for h in range(NH):
            acc_scr[h * t:(h + 1) * t, :] += _mm(att[h * t:(h + 1) * t], get_v(h))
        r_scr[...] = jnp.broadcast_to(r, r_scr.shape)

    @pl.when(s == 0)
    def _():
        process(lambda h: _pad_rows(kn_ref[:, h * HD:(h + 1) * HD], HD),
                lambda h: _pad_rows(vn_ref[:, h * HD:(h + 1) * HD], HD),
                col < (row % t))

    @pl.when(s > 0)
    def _():
        process(lambda h: kp_ref[:, h, :], lambda h: vp_ref[:, h, :], None)

    @pl.when(s == n_steps - 1)
    def _():
        for h in range(NH):
            o_ref[:, h * HD:(h + 1) * HD] = acc_scr[h * t:(h + 1) * t, :].astype(bf16)


def _sb_sample(z, row_off, nb, t, cache_k, cache_v, page_table):
    n_pages = page_table.shape[1]
    n_steps = n_pages + 1
    rb = row_off // t
    page = lambda b, s, pt: (pt[b, n_pages - jnp.maximum(s, 1)], 0, 0, 0)
    grid_spec = pltpu.PrefetchScalarGridSpec(
        num_scalar_prefetch=1,
        grid=(nb, n_steps),
        in_specs=[pl.BlockSpec((t, NH * HD), lambda b, s, pt: (rb + b, 0)),
                  pl.BlockSpec((t, NH * HD), lambda b, s, pt: (rb + b, 1)),
                  pl.BlockSpec((t, NH * HD), lambda b, s, pt: (rb + b, 2)),
                  pl.BlockSpec((None, PAGE, NH, HD), page),
                  pl.BlockSpec((None, PAGE, NH, HD), page)],
        out_specs=pl.BlockSpec((t, NH * HD), lambda b, s, pt: (b, 0)),
        scratch_shapes=[pltpu.VMEM((NH * t, HD), f32), pltpu.VMEM((NH * t, HD), f32)],
    )
    return pl.pallas_call(
        functools.partial(_sbs_body, n_steps=n_steps, t=t),
        out_shape=jax.ShapeDtypeStruct((nb * t, NH * HD), bf16),
        grid_spec=grid_spec,
        compiler_params=_cparams(2),
        name="sb_sample",
    )(page_table, z, z, z, cache_k, cache_v)


def _t5_bucket(dist):
    n = jnp.maximum(dist, 0)
    exact = N_BUCKETS // 2
    nf = jnp.maximum(n, 1).astype(f32)
    large = exact + (jnp.log(nf / exact) / math.log(MAX_DISTANCE / exact) * (N_BUCKETS - exact)).astype(i32)
    return jnp.where(n < exact, n, jnp.minimum(large, N_BUCKETS - 1))


def _bias_of_bucket(bucket, rel_ref, h):
    out = jnp.zeros(bucket.shape, f32)
    for k in range(N_BUCKETS):
        out = jnp.where(bucket == k, rel_ref[k, h], out)
    return out


def _softmax_step(m, l, acc, logits, mask, v):
    lg = jnp.where(mask, logits, NEG)
    m_new = jnp.maximum(m, jnp.max(lg, axis=-1, keepdims=True))
    alpha = jnp.exp(m - m_new)
    e = jnp.where(mask, jnp.exp(lg - m_new), 0.0)
    return m_new, alpha * l + jnp.sum(e, axis=-1, keepdims=True), alpha * acc + _mm(e, v)


def _softmax_full(logits, mask):
    lg = jnp.where(mask, logits, NEG)
    e = jnp.where(mask, jnp.exp(lg - jnp.max(lg, axis=-1, keepdims=True)), 0.0)
    return e / jnp.maximum(jnp.sum(e, axis=-1, keepdims=True), 1e-30)


def _select_blocks(imp_sel, cur, n_sel):
    rows = imp_sel.shape[0]
    imp_pad = jnp.concatenate([imp_sel, jnp.zeros((rows, HD - imp_sel.shape[1]), f32)], axis=1)
    blk = _iota2((rows, HD), 1)
    forced = jnp.where(blk == cur, 2.0 * FORCE_SCORE, jnp.where(blk == 0, FORCE_SCORE, -FORCE_SCORE))
    score = jnp.where((blk < cur) & (blk > 0), imp_pad, forced)
    score = jnp.where(blk < n_sel, score, -jnp.inf)
    sel = jnp.zeros((rows, HD), f32)
    for _ in range(min(TOP_N, n_sel)):
        mx = jnp.max(score, axis=-1, keepdims=True)
        idx = jnp.min(jnp.where(score == mx, blk, 1 << 30), axis=-1, keepdims=True)
        hit = blk == idx
        sel = jnp.where(hit, 1.0, sel)
        score = jnp.where(hit, -jnp.inf, score)
    return sel


def _compress(get_rows, cw_ref, cb, ckg, n_groups_rows):
    ak = jnp.zeros((n_groups_rows, HD), f32)
    av = jnp.zeros((n_groups_rows, HD), f32)
    for t in range(CMP_BLOCK):
        ak = ak + _mm(get_rows(0, t), cw_ref[0, t * HD:(t + 1) * HD, :])
        av = av + _mm(get_rows(1, t), cw_ref[1, t * HD:(t + 1) * HD, :])
    return _rms(ak + cb[0:1], ckg), av + cb[1:2]


def _cmp_order(n_cmp, shape, axis):
    c = _iota2(shape, axis)
    half = n_cmp // 2
    return 2 * (c % half) + c // half


def _nsap_body(q_ref, gt_ref, ck0_ref, ck1_ref, cv0_ref, cv1_ref, sk_ref, sv_ref, wk_ref, wv_ref, cw_ref, cb_ref, ckg_ref,
               qg_ref, rel_ref, o_ref, kc_scr, vc_scr, bias_scr, *, t):
    b = pl.program_id(0)
    i = pl.program_id(1)
    n_cmp = t // CMP_BLOCK
    half = n_cmp // 2
    n_sel = t // SEL_BLOCK
    QB = HD
    cmp_refs = ((ck0_ref, ck1_ref), (cv0_ref, cv1_ref))

    @pl.when(i == 0)
    def _():
        for g in range(NKV):
            def get_rows(kind, tt, g=g):
                ref = cmp_refs[kind][g]
                return jnp.concatenate([ref[pl.ds(tt, half, stride=2 * CMP_BLOCK), :],
                                        ref[pl.ds(CMP_BLOCK + tt, half, stride=2 * CMP_BLOCK), :]], axis=0)
            kc, vc = _compress(get_rows, cw_ref, cb_ref[...], ckg_ref[...], n_cmp)
            kc_scr[g] = kc
            vc_scr[g] = vc

    @pl.when((b == 0) & (i == 0))
    def _():
        r_ = _iota2((QB, QB), 0)
        c_ = _iota2((QB, QB), 1)
        for kk in range(3):
            bucket = _t5_bucket(r_ - c_ + QB * kk)
            for h in range(NH):
                bias_scr[kk, h] = _bias_of_bucket(bucket, rel_ref, h)

    qall = q_ref[...]
    qg = qg_ref[...]
    qs = [jnp.concatenate([_rms(qall[:, (g * NREP + r) * HD:(g * NREP + r + 1) * HD], qg) for r in range(NREP)], axis=0)
          for g in range(NKV)]
    rep = lambda x: jnp.concatenate([x] * NREP, axis=0)

    qpos_c = i * QB + _iota2((QB, n_cmp), 0)
    dist_c = qpos_c - (_cmp_order(n_cmp, (QB, n_cmp), 1) * CMP_BLOCK + CMP_BLOCK - 1)
    bucket_c = _t5_bucket(dist_c)
    mask_c = rep(dist_c >= 0)
    cur = (i * QB + _iota2((QB, 1), 0)) // SEL_BLOCK
    o_cmp, sels = [], []
    for g in range(NKV):
        bias = jnp.concatenate([_bias_of_bucket(bucket_c, rel_ref, g * NREP + r) for r in range(NREP)], axis=0)
        p = _softmax_full(_mm(qs[g], kc_scr[g], NT) * SCALE + bias, mask_c)
        o_cmp.append(_mm(p, vc_scr[g]))
        imp = p[0:QB] + p[QB:2 * QB] + p[2 * QB:3 * QB] + p[3 * QB:4 * QB]
        sels.append(_select_blocks(imp[:, :half] + imp[:, half:], cur, n_sel))

    krow = _iota2((QB, QB), 0)
    kcol = _iota2((QB, QB), 1)

    def init():
        return tuple(x for _ in range(NKV) for x in (jnp.full((NREP * QB, 1), NEG, f32), jnp.zeros((NREP * QB, 1), f32),
                                                     jnp.zeros((NREP * QB, HD), f32)))

    def attend(carry, s, k_ref, v_ref, mask_of):
        j = i - s
        off = pl.multiple_of(j * QB, QB)
        delta = jnp.minimum(s, 2)
        out = []
        for g in range(NKV):
            m, l, acc = carry[3 * g:3 * g + 3]
            bias = jnp.concatenate([bias_scr[delta, g * NREP + r] for r in range(NREP)], axis=0)
            logits = _mm(qs[g], k_ref[pl.ds(off, QB), g * HD:(g + 1) * HD], NT) * SCALE + bias
            out.extend(_softmax_step(m, l, acc, logits, rep(mask_of(g, j, s)), v_ref[pl.ds(off, QB), g * HD:(g + 1) * HD]))
        return tuple(out)

    def sel_mask(g, j, s):
        expand = jnp.where(krow == 2 * j + kcol // SEL_BLOCK, 1.0, 0.0)
        return (_mm(sels[g], expand) > 0.5) & ((kcol - krow) <= s * QB)

    def win_mask(g, j, s):
        dist = s * QB + krow - kcol
        return (dist >= 0) & (dist < WINDOW)

    c_sel = lax.fori_loop(0, i + 1, lambda s, c: attend(c, s, sk_ref, sv_ref, sel_mask), init())
    c_win = lax.fori_loop(0, jnp.minimum(i, WINDOW // QB) + 1, lambda s, c: attend(c, s, wk_ref, wv_ref, win_mask), init())

    gts = jax.nn.sigmoid(gt_ref[...])
    for h in range(NH):
        g, r = divmod(h, NREP)
        sl = slice(r * QB, (r + 1) * QB)
        o_s = c_sel[3 * g + 2][sl] / jnp.maximum(c_sel[3 * g + 1][sl], 1e-30)
        o_w = c_win[3 * g + 2][sl] / jnp.maximum(c_win[3 * g + 1][sl], 1e-30)
        o = gts[:, 3 * h:3 * h + 1] * o_cmp[g][sl] + gts[:, 3 * h + 1:3 * h + 2] * o_s + gts[:, 3 * h + 2:3 * h + 3] * o_w
        o_ref[:, h * HD:(h + 1) * HD] = o.astype(bf16)


def _nsa_prompt(z, sk, wk, nb, t, cmp_w, cmp_b, cmp_k_g, q_g, rel_bias):
    nq = t // HD
    n_cmp = t // CMP_BLOCK
    w = NKV * HD
    full = lambda j: pl.BlockSpec((t, w), lambda b, i: (b, j))
    head = lambda j: pl.BlockSpec((t, HD), lambda b, i: (b, j))
    return pl.pallas_call(
        functools.partial(_nsap_body, t=t),
        out_shape=jax.ShapeDtypeStruct((nb * t, NH * HD), bf16),
        grid=(nb, nq),
        in_specs=[pl.BlockSpec((HD, NH * HD), lambda b, i: (b * nq + i, 3)),
                  pl.BlockSpec((HD, HD), lambda b, i: (b * nq + i, 44)),
                  head(32), head(33), head(34), head(35), full(0), full(19), full(0), full(21),
                  pl.BlockSpec((2, CMP_BLOCK * HD, HD), lambda b, i: (0, 0, 0)),
                  pl.BlockSpec((2, HD), lambda b, i: (0, 0)),
                  pl.BlockSpec((1, HD), lambda b, i: (0, 0)),
                  pl.BlockSpec((1, HD), lambda b, i: (0, 0)),
                  pl.BlockSpec(memory_space=pltpu.SMEM)],
        out_specs=pl.BlockSpec((HD, NH * HD), lambda b, i: (b * nq + i, 0)),
        scratch_shapes=[pltpu.VMEM((NKV, n_cmp, HD), f32), pltpu.VMEM((NKV, n_cmp, HD), f32),
                        pltpu.VMEM((3, NH, HD, HD), f32)],
        compiler_params=_cparams(2),
        name="nsa_prompt",
    )(z, z, z, z, z, z, sk, z, wk, z, cmp_w.astype(bf16), cmp_b, cmp_k_g.reshape(1, HD), q_g.reshape(1, HD), rel_bias)


def _nsa_q_groups(q_ref, qg):
    qall = q_ref[...]
    return [jnp.concatenate([_rms(qall[:, (g * NREP + r) * HD:(g * NREP + r + 1) * HD], qg) for r in range(NREP)], axis=0)
            for g in range(NKV)]


def _nsasa_body(pt_ref, q_ref, ckp_ref, cvp_ref, cw_ref, cb_ref, ckg_ref, qg_ref, rel_ref, oc_ref, sel_ref,
                ck_scr, cv_scr, *, n_pages, t):
    del pt_ref
    p = pl.program_id(1)
    off = pl.multiple_of(p * PAGE, PAGE)
    for g in range(NKV):
        ck_scr[g, pl.ds(off, PAGE), :] = ckp_ref[:, g, :]
        cv_scr[g, pl.ds(off, PAGE), :] = cvp_ref[:, g, :]

    @pl.when(p == n_pages - 1)
    def _():
        past = n_pages * PAGE
        n_cmp = (past + t) // CMP_BLOCK
        half = n_cmp // 2
        n_sel = -(-(past + t) // SEL_BLOCK)

        def get_rows(kind, tt):
            scr = ck_scr if kind == 0 else cv_scr
            return jnp.concatenate([scr[g, pl.ds(par * CMP_BLOCK + tt, half, stride=2 * CMP_BLOCK), :]
                                    for g in range(NKV) for par in range(2)], axis=0)

        kc, vc = _compress(get_rows, cw_ref, cb_ref[...], ckg_ref[...], NKV * n_cmp)
        qs = _nsa_q_groups(q_ref, qg_ref[...])
        rows = NREP * t
        tq = _iota2((rows, n_cmp), 0) % t
        dist_c = past + tq - (_cmp_order(n_cmp, (rows, n_cmp), 1) * CMP_BLOCK + CMP_BLOCK - 1)
        bucket_c = _t5_bucket(dist_c)
        mask_c = dist_c >= 0
        cur = (past + _iota2((t, 1), 0)) // SEL_BLOCK
        for g in range(NKV):
            bias = jnp.concatenate([_bias_of_bucket(bucket_c[r * t:(r + 1) * t], rel_ref, g * NREP + r)
                                    for r in range(NREP)], axis=0)
            pr = _softmax_full(_mm(qs[g], kc[g * n_cmp:(g + 1) * n_cmp], NT) * SCALE + bias, mask_c)
            oc_ref[g * rows:(g + 1) * rows, :] = _mm(pr, vc[g * n_cmp:(g + 1) * n_cmp])
            imp = pr[0:t] + pr[t:2 * t] + pr[2 * t:3 * t] + pr[3 * t:4 * t]
            sel_ref[g * t:(g + 1) * t, :] = _select_blocks(imp[:, :half] + imp[:, half:], cur, n_sel)


def _nsa_sample_cmp(z, row_off, nb, t, cache_ck, cache_cv, page_table, cmp_w, cmp_b, cmp_k_g, q_g, rel_bias):
    n_pages = page_table.shape[1]
    rb = row_off // t
    page = lambda b, p, pt: (pt[b, p], 0, 0, 0)
    cst = lambda *shape: pl.BlockSpec(shape, lambda b, p, pt: (0,) * len(shape))
    grid_spec = pltpu.PrefetchScalarGridSpec(
        num_scalar_prefetch=1,
        grid=(nb, n_pages),
        in_specs=[pl.BlockSpec((t, NH * HD), lambda b, p, pt: (rb + b, 3)),
                  pl.BlockSpec((None, PAGE, NKV, HD), page),
                  pl.BlockSpec((None, PAGE, NKV, HD), page),
                  cst(2, CMP_BLOCK * HD, HD), cst(2, HD), cst(1, HD), cst(1, HD),
                  pl.BlockSpec(memory_space=pltpu.SMEM)],
        out_specs=(pl.BlockSpec((None, NH * t, HD), lambda b, p, pt: (b, 0, 0)),
                   pl.BlockSpec((None, NKV * t, HD), lambda b, p, pt: (b, 0, 0))),
        scratch_shapes=[pltpu.VMEM((NKV, n_pages * PAGE, HD), f32), pltpu.VMEM((NKV, n_pages * PAGE, HD), f32)],
    )
    return pl.pallas_call(
        functools.partial(_nsasa_body, n_pages=n_pages, t=t),
        out_shape=(jax.ShapeDtypeStruct((nb, NH * t, HD), f32), jax.ShapeDtypeStruct((nb, NKV * t, HD), f32)),
        grid_spec=grid_spec,
        compiler_params=_cparams(2),
        name="nsa_sample_cmp",
    )(page_table, z, cache_ck, cache_cv, cmp_w.astype(bf16), cmp_b, cmp_k_g.reshape(1, HD), q_g.reshape(1, HD), rel_bias)


def _nsasb_body(pt_ref, q_ref, gt_ref, oc_ref, sel_ref, skn_ref, svn_ref, wkn_ref, wvn_ref, wk_ref, wv_ref,
                skp_ref, svp_ref, qg_ref, rel_ref, o_ref, m_scr, l_scr, acc_scr, ow_scr, *, n_pages, t):
    del pt_ref
    s = pl.program_id(1)
    past = n_pages * PAGE
    rows = NREP * t
    qs = _nsa_q_groups(q_ref, qg_ref[...])
    tq = _iota2((t, HD), 0)
    col = _iota2((t, HD), 1)
    krow = _iota2((HD, HD), 0)
    kcol = _iota2((HD, HD), 1)
    rep = lambda x: jnp.concatenate([x] * NREP, axis=0)

    def bias_of(dist, g):
        bucket = _t5_bucket(dist)
        return jnp.concatenate([_bias_of_bucket(bucket, rel_ref, g * NREP + r) for r in range(NREP)], axis=0)

    def step(state, g, k, v, dist, mask):
        logits = _mm(qs[g], k, NT) * SCALE + bias_of(dist, g)
        return _softmax_step(*state, logits, rep(mask), v)

    def sel_update(g, k, v, dist, mask):
        sl = slice(g * rows, (g + 1) * rows)
        m, l, acc = step((m_scr[sl, 0:1], l_scr[sl, 0:1], acc_scr[sl, :]), g, k, v, dist, mask)
        m_scr[sl, :] = jnp.broadcast_to(m, (rows, HD))
        l_scr[sl, :] = jnp.broadcast_to(l, (rows, HD))
        acc_scr[sl, :] = acc

    @pl.when(s == 0)
    def _():
        m_scr[...] = jnp.full(m_scr.shape, NEG, f32)
        l_scr[...] = jnp.zeros_like(l_scr)
        acc_scr[...] = jnp.zeros_like(acc_scr)
        dist_new = tq - col
        new_ok = (dist_new >= 0) & (col < t)
        expand_new = jnp.where(krow == (past + kcol) // SEL_BLOCK, 1.0, 0.0)
        for g in range(NKV):
            gs = slice(g * HD, (g + 1) * HD)
            st = (jnp.full((rows, 1), NEG, f32), jnp.zeros((rows, 1), f32), jnp.zeros((rows, HD), f32))
            for jj in range(WINDOW // HD):
                dist = WINDOW + tq - jj * HD - col
                st = step(st, g, wk_ref[jj * HD:(jj + 1) * HD, g, :], wv_ref[jj * HD:(jj + 1) * HD, g, :], dist,
                          (dist >= 0) & (dist < WINDOW))
            st = step(st, g, _pad_rows(wkn_ref[:, gs], HD), _pad_rows(wvn_ref[:, gs], HD), dist_new, new_ok)
            ow_scr[g * rows:(g + 1) * rows, :] = st[2] / jnp.maximum(st[1], 1e-30)
            picked = _mm(sel_ref[g * t:(g + 1) * t, :], expand_new) > 0.5
            sel_update(g, _pad_rows(skn_ref[:, gs], HD), _pad_rows(svn_ref[:, gs], HD), dist_new, picked & new_ok)

    @pl.when(s > 0)
    def _():
        pg = s - 1
        dist = past + tq - (pg * PAGE + col)
        expand = jnp.where(krow == (PAGE // SEL_BLOCK) * pg + kcol // SEL_BLOCK, 1.0, 0.0)
        for g in range(NKV):
            picked = _mm(sel_ref[g * t:(g + 1) * t, :], expand) > 0.5
            sel_update(g, skp_ref[:, g, :], svp_ref[:, g, :], dist, picked & (dist >= 0))

    @pl.when(s == n_pages)
    def _():
        gts = jax.nn.sigmoid(gt_ref[...])
        for h in range(NH):
            g, r = divmod(h, NREP)
            sl = slice(g * rows + r * t, g * rows + (r + 1) * t)
            o_s = acc_scr[sl, :] / jnp.maximum(l_scr[sl, 0:1], 1e-30)
            o = (gts[:, 3 * h:3 * h + 1] * oc_ref[sl, :] + gts[:, 3 * h + 1:3 * h + 2] * o_s
                 + gts[:, 3 * h + 2:3 * h + 3] * ow_scr[sl, :])
            o_ref[:, h * HD:(h + 1) * HD] = o.astype(bf16)


def _nsa_sample_attn(z, sk, wk, row_off, nb, t, o_cmp, sel, win_k, win_v, cache_sk, cache_sv, page_table, q_g, rel_bias):
    n_pages = page_table.shape[1]
    rb = row_off // t
    w = NKV * HD
    page = lambda b, s, pt: (pt[b, jnp.maximum(s, 1) - 1], 0, 0, 0)
    rowsp = lambda width, j: pl.BlockSpec((t, width), lambda b, s, pt: (rb + b, j))
    grid_spec = pltpu.PrefetchScalarGridSpec(
        num_scalar_prefetch=1,
        grid=(nb, n_pages + 1),
        in_specs=[rowsp(NH * HD, 3), rowsp(HD, 44),
                  pl.BlockSpec((None, NH * t, HD), lambda b, s, pt: (b, 0, 0)),
                  pl.BlockSpec((None, NKV * t, HD), lambda b, s, pt: (b, 0, 0)),
                  rowsp(w, 0), rowsp(w, 19), rowsp(w, 0), rowsp(w, 21),
                  pl.BlockSpec((None, WINDOW, NKV, HD), lambda b, s, pt: (b, 0, 0, 0)),
                  pl.BlockSpec((None, WINDOW, NKV, HD), lambda b, s, pt: (b, 0, 0, 0)),
                  pl.BlockSpec((None, PAGE, NKV, HD), page),
                  pl.BlockSpec((None, PAGE, NKV, HD), page),
                  pl.BlockSpec((1, HD), lambda b, s, pt: (0, 0)),
                  pl.BlockSpec(memory_space=pltpu.SMEM)],
        out_specs=pl.BlockSpec((t, NH * HD), lambda b, s, pt: (b, 0)),
        scratch_shapes=[pltpu.VMEM((NH * t, HD), f32)] * 4,
    )
    return pl.pallas_call(
        functools.partial(_nsasb_body, n_pages=n_pages, t=t),
        out_shape=jax.ShapeDtypeStruct((nb * t, NH * HD), bf16),
        grid_spec=grid_spec,
        compiler_params=_cparams(2),
        name="nsa_sample_attn",
    )(page_table, z, z, o_cmp, sel, sk, z, wk, z, win_k, win_v, cache_sk, cache_sv, q_g.reshape(1, HD), rel_bias)


def _even_mixer(z, mp, bp, tp, bs, ts, conv_w, a_log, dt_bias, gdn_g, gate_bias, mlstm_g, s0, conv0, c0, n0, m0):
    w3 = 3 * NH * HD
    go_p, gs_p = _gdn(z, 0, bp, tp, conv_w, a_log, dt_bias, gdn_g)
    prev8 = jnp.concatenate([jnp.zeros((bs, 8 - conv0.shape[1], w3), f32), conv0], axis=1)
    go_s, gs_s = _gdn(z, mp, bs, ts, conv_w, a_log, dt_bias, gdn_g, s0=s0, prev8=prev8)
    mh_p, mc_p, mn_p, mm_p = _mlstm(z, 0, bp, tp, gate_bias, mlstm_g)
    mh_s, mc_s, mn_s, mm_s = _mlstm(z, mp, bs, ts, gate_bias, mlstm_g, c0=c0, n0=n0, m0=m0)
    mix = jnp.concatenate([jnp.concatenate([go_p, mh_p], axis=1), jnp.concatenate([go_s, mh_s], axis=1)], axis=0)
    keep = conv0.shape[1]
    conv_p = z[:mp, :w3].reshape(bp, tp, w3)[:, tp - keep:]
    conv_s = z[mp:, :w3].reshape(bs, ts, w3)[:, ts - keep:]
    states = (gs_p, gs_s, conv_p, conv_s, mc_p, mc_s, mn_p, mn_s, mm_p[..., 0], mm_s[..., 0])
    return mix, states


def _odd_mixer(z, mp, bp, tp, bs, ts, page_table, caches, win_k0, win_v0, q_g, k_g, cmp_w, cmp_b, rel_bias):
    sb_k, sb_v, cmp_k, cmp_v, sel_k, sel_v = caches
    sk, wk = _kvnorm(z, k_g)
    sb_p = _sb_prompt(z, bp, tp)
    sb_s = _sb_sample(z, mp, bs, ts, sb_k, sb_v, page_table)
    ns_p = _nsa_prompt(z, sk, wk, bp, tp, cmp_w, cmp_b, k_g[0], q_g, rel_bias)
    o_cmp, sel = _nsa_sample_cmp(z, mp, bs, ts, cmp_k, cmp_v, page_table, cmp_w, cmp_b, k_g[0], q_g, rel_bias)
    ns_s = _nsa_sample_attn(z, sk, wk, mp, bs, ts, o_cmp, sel, win_k0, win_v0, sel_k, sel_v, page_table, q_g, rel_bias)
    mix = jnp.concatenate([jnp.concatenate([sb_p, ns_p], axis=1), jnp.concatenate([sb_s, ns_s], axis=1)], axis=0)

    w = NH * HD
    kw = NKV * HD

    def rows(arr, lo, width, heads):
        sl = arr[:, lo:lo + width]
        return sl[:mp].reshape(bp, tp, heads, HD), sl[mp:].reshape(bs, ts, heads, HD)

    sbk_p, sbk_s = rows(z, w, w, NH)
    sbv_p, sbv_s = rows(z, 2 * w, w, NH)
    ck_p, ck_s = rows(z, 4 * w, kw, NKV)
    cv_p, cv_s = rows(z, 4 * w + kw, kw, NKV)
    sk_p, sk_s = rows(sk, 0, kw, NKV)
    sv_p, sv_s = rows(z, 4 * w + 3 * kw, kw, NKV)
    wk_p, wk_s = rows(wk, 0, kw, NKV)
    wv_p, wv_s = rows(z, 4 * w + 5 * kw, kw, NKV)
    keep_p = min(WINDOW, tp)
    win = lambda old, new: jnp.concatenate([old, new], axis=1)[:, -min(WINDOW, old.shape[1] + ts):]
    states = (sbk_p, sbk_s, sbv_p, sbv_s, ck_p, ck_s, cv_p, cv_s, sk_p, sk_s, sv_p, sv_s,
              wk_p[:, tp - keep_p:], win(win_k0, wk_s), wv_p[:, tp - keep_p:], win(win_v0, wv_s))
    return mix, states


def kernel(x_prompt, x_sample, state_gdn_s, state_gdn_conv, state_mlstm_c, state_mlstm_n, state_mlstm_m, cache_sb_k, cache_sb_v, cache_nsa_cmp_k, cache_nsa_cmp_v, cache_nsa_sel_k, cache_nsa_sel_v, state_nsa_win_k, state_nsa_win_v, page_table, norm_g, ffn_w_gate, ffn_w_up, ffn_w_down, even_w_in, gdn_conv_w, gdn_a_log, gdn_dt_bias, gdn_norm_g, mlstm_gate_bias, mlstm_norm_g, even_w_out, odd_w_in, nsa_q_norm_g, nsa_k_norm_g, nsa_cmp_w, nsa_cmp_b, odd_w_out, rel_bias):
    bp, tp, d = x_prompt.shape
    bs, ts, _ = x_sample.shape
    mp = bp * tp
    xs = jnp.concatenate([x_prompt.reshape(mp, d), x_sample.reshape(bs * ts, d)], axis=0)
    depth = norm_g.shape[0]
    even_states, odd_states = [], []
    for layer in range(depth):
        j = layer // 2
        ffn = lambda x, n, i: _ffn(x, norm_g[layer, n], ffn_w_gate[layer, i].astype(bf16), ffn_w_up[layer, i].astype(bf16),
                                   ffn_w_down[layer, i].astype(bf16))
        xs = ffn(xs, 0, 0)
        if layer % 2 == 0:
            w_in = even_w_in[j][:, _even_perm()]
            w_in = jnp.concatenate([w_in, jnp.zeros((d, 65 * HD - w_in.shape[1]), f32)], axis=1).astype(bf16)
            z = _rms_matmul(xs, norm_g[layer, 1], w_in)
            mix, st = _even_mixer(z, mp, bp, tp, bs, ts, gdn_conv_w[j], gdn_a_log[j], gdn_dt_bias[j], gdn_norm_g[j],
                                  mlstm_gate_bias[j], mlstm_norm_g[j], state_gdn_s[j], state_gdn_conv[j],
                                  state_mlstm_c[j], state_mlstm_n[j], state_mlstm_m[j])
            even_states.append(st)
            w_out = even_w_out[j]
        else:
            w_in = jnp.concatenate([odd_w_in[j], jnp.zeros((d, ODD_W - odd_w_in.shape[2]), f32)], axis=1).astype(bf16)
            z = _rms_matmul(xs, norm_g[layer, 1], w_in)
            caches = (cache_sb_k[j], cache_sb_v[j], cache_nsa_cmp_k[j], cache_nsa_cmp_v[j], cache_nsa_sel_k[j],
                      cache_nsa_sel_v[j])
            mix, st = _odd_mixer(z, mp, bp, tp, bs, ts, page_table, caches, state_nsa_win_k[j], state_nsa_win_v[j],
                                 nsa_q_norm_g[j], nsa_k_norm_g[j], nsa_cmp_w[j], nsa_cmp_b[j], rel_bias)
            odd_states.append(st)
            w_out = odd_w_out[j]
        xs = _out_proj(mix, w_out.astype(bf16), xs)
        xs = ffn(xs, 2, 1)
    stack = lambda sts, i: jnp.stack([s[i] for s in sts])
    outs = [xs[:mp].reshape(bp, tp, d), xs[mp:].reshape(bs, ts, d)]
    outs += [stack(even_states, i) for i in range(10)]
    outs += [stack(odd_states, i) for i in range(16)]
    return tuple(outs)
```

```python
import functools
import math

import jax
import jax.numpy as jnp
import numpy as np
from jax import lax
from jax.experimental import pallas as pl
from jax.experimental.pallas import tpu as pltpu

f32 = jnp.float32
bf16 = jnp.bfloat16
i32 = jnp.int32

HD = 128
NH = 8
NKV = 2
NREP = 4
RMS_EPS = 1e-6
MLSTM_EPS = 1e-6
CHUNK = 128
GDN_HI_LEVELS = 6
PAGE = 128
CMP_BLOCK = 32
SEL_BLOCK = 64
TOP_N = 8
WINDOW = 512
FORCE_SCORE = 1.0e4
N_BUCKETS = 32
MAX_DISTANCE = 128
NEG = -1e30
VMEM_LIMIT = 56 * 1024 * 1024

NN = (((1,), (0,)), ((), ()))
NT = (((1,), (1,)), ((), ()))
TN = (((0,), (0,)), ((), ()))


def _cparams(n_axes):
    return pltpu.CompilerParams(dimension_semantics=("arbitrary",) * n_axes, vmem_limit_bytes=VMEM_LIMIT)


def _mm(a, b, dims=NN):
    return lax.dot_general(a.astype(bf16), b.astype(bf16), dims, preferred_element_type=f32)


def _split3(a):
    a0 = a.astype(bf16)
    r = a - a0.astype(f32)
    a1 = r.astype(bf16)
    a2 = (r - a1.astype(f32)).astype(bf16)
    return a0, a1, a2


def _mm_exact_b(a, b01, dims=NN):
    a0, a1, a2 = _split3(a)
    b01 = b01.astype(bf16)
    d = lambda x: lax.dot_general(x, b01, dims, preferred_element_type=f32)
    return d(a0) + d(a1) + d(a2)


def _mm_exact_a(a01, b, dims=NN):
    b0, b1, b2 = _split3(b)
    a01 = a01.astype(bf16)
    d = lambda x: lax.dot_general(a01, x, dims, preferred_element_type=f32)
    return d(b0) + d(b1) + d(b2)


def _mm3(a, b, dims=NN):
    ah = a.astype(bf16)
    al = (a - ah.astype(f32)).astype(bf16)
    bh = b.astype(bf16)
    bl = (b - bh.astype(f32)).astype(bf16)
    d = lambda x, y: lax.dot_general(x, y, dims, preferred_element_type=f32)
    return d(ah, bh) + d(ah, bl) + d(al, bh)


def _softplus(x):
    return jnp.maximum(x, 0.0) + jnp.log1p(jnp.exp(-jnp.abs(x)))


def _log_sigmoid(x):
    return -_softplus(-x)


def _rms(x, g):
    return x * lax.rsqrt(jnp.mean(x * x, axis=-1, keepdims=True) + RMS_EPS) * g


def _pad_rows(x, rows, value=0.0):
    if x.shape[0] == rows:
        return x
    return jnp.concatenate([x, jnp.full((rows - x.shape[0],) + x.shape[1:], value, x.dtype)], axis=0)


def _iota2(shape, axis):
    return lax.broadcasted_iota(i32, shape, axis)


def _ffn_body(x_ref, g_ref, wg_ref, wu_ref, wd_ref, o_ref, h_ref, *, nf):
    f = pl.program_id(1)

    del nf

    @pl.when(f == 0)
    def _():
        x = x_ref[...]
        h_ref[...] = _rms(x, g_ref[...]).astype(bf16)
        o_ref[...] = x

    h = h_ref[...]
    a = jnp.dot(h, wg_ref[...].astype(bf16), preferred_element_type=f32)
    u = jnp.dot(h, wu_ref[...].astype(bf16), preferred_element_type=f32)
    act = (0.5 * a * jax.nn.sigmoid(a) * u).astype(bf16)
    o_ref[...] += jnp.dot(act, wd_ref[...].astype(bf16), preferred_element_type=f32)


def _ffn(x, g, wg, wu, wd, *, tm=768, tf=512):
    m, d = x.shape
    fdim = wg.shape[1]
    nf = fdim // tf
    once = pl.Buffered(1)
    return pl.pallas_call(
        functools.partial(_ffn_body, nf=nf),
        out_shape=jax.ShapeDtypeStruct((m, d), f32),
        grid=(m // tm, nf),
        in_specs=[
            pl.BlockSpec((tm, d), lambda i, f: (i, 0), pipeline_mode=once),
            pl.BlockSpec((1, d), lambda i, f: (0, 0)),
            pl.BlockSpec((d, tf), lambda i, f: (0, f)),
            pl.BlockSpec((d, tf), lambda i, f: (0, f)),
            pl.BlockSpec((tf, d), lambda i, f: (f, 0)),
        ],
        out_specs=pl.BlockSpec((tm, d), lambda i, f: (i, 0), pipeline_mode=once),
        scratch_shapes=[pltpu.VMEM((tm, d), bf16)],
        compiler_params=_cparams(2),
        name="ffn",
    )(x, g.reshape(1, d), wg, wu, wd)


def _rmsmm_body(x_ref, g_ref, w_ref, o_ref, h_ref):
    @pl.when(pl.program_id(1) == 0)
    def _():
        h_ref[...] = _rms(x_ref[...], g_ref[...]).astype(bf16)

    o_ref[...] = jnp.dot(h_ref[...], w_ref[...], preferred_element_type=f32)


def _rms_matmul(x, g, w, *, tm=512, tn=640):
    m, d = x.shape
    n = w.shape[1]
    return pl.pallas_call(
        _rmsmm_body,
        out_shape=jax.ShapeDtypeStruct((m, n), f32),
        grid=(m // tm, n // tn),
        in_specs=[
            pl.BlockSpec((tm, d), lambda i, j: (i, 0)),
            pl.BlockSpec((1, d), lambda i, j: (0, 0)),
            pl.BlockSpec((d, tn), lambda i, j: (0, j)),
        ],
        out_specs=pl.BlockSpec((tm, tn), lambda i, j: (i, j)),
        scratch_shapes=[pltpu.VMEM((tm, d), bf16)],
        compiler_params=_cparams(2),
        name="rms_matmul",
    )(x, g.reshape(1, d), w)


def _outproj_body(a_ref, w_ref, r_ref, o_ref):
    o_ref[...] = r_ref[...] + jnp.dot(a_ref[...], w_ref[...], preferred_element_type=f32)


def _out_proj(a, w, res, *, tm=512, tn=512):
    m, k = a.shape
    n = w.shape[1]
    return pl.pallas_call(
        _outproj_body,
        out_shape=jax.ShapeDtypeStruct((m, n), f32),
        grid=(m // tm, n // tn),
        in_specs=[
            pl.BlockSpec((tm, k), lambda i, j: (i, 0)),
            pl.BlockSpec((k, tn), lambda i, j: (0, j)),
            pl.BlockSpec((tm, tn), lambda i, j: (i, j)),
        ],
        out_specs=pl.BlockSpec((tm, tn), lambda i, j: (i, j)),
        compiler_params=_cparams(2),
        name="out_proj",
    )(a, w, res)


def _tri_masks(c):
    row = _iota2((c, c), 0)
    col = _iota2((c, c), 1)
    return row, col


def _gdn_body(*refs, c_in, n_chunks, levels, hi_levels, has_state, first_chunk_zero_prev):
    if has_state:
        (u_ref, p8_ref, gate_ref, sm_ref, cw_ref, alog_ref, dtb_ref, ng_ref, s0_ref, o_ref, so_ref, s_scr) = refs
    else:
        (u_ref, p8_ref, gate_ref, sm_ref, cw_ref, alog_ref, dtb_ref, ng_ref, o_ref, so_ref, s_scr) = refs
    c = pl.program_id(1)
    C = CHUNK

    @pl.when(c == 0)
    def _():
        if has_state:
            s_scr[...] = s0_ref[...]
        else:
            s_scr[...] = jnp.zeros_like(s_scr)

    u = u_ref[...]
    p8 = p8_ref[...]
    if first_chunk_zero_prev:
        p8 = jnp.where(c == 0, 0.0, p8)
    cw = cw_ref[...]
    row8 = _iota2((8, u.shape[1]), 0)
    acc = u * cw[3:4]
    for k in range(1, 4):
        rolled = pltpu.roll(u, k, axis=0)
        first8 = jnp.where(row8 < k, pltpu.roll(p8, k, axis=0), rolled[0:8])
        sh = first8 if c_in == 8 else jnp.concatenate([first8, rolled[8:]], axis=0)
        acc = acc + sh * cw[3 - k:4 - k]
    qkv = acc * jax.nn.sigmoid(acc)

    sm = sm_ref[...]
    gfull = _pad_rows(-jnp.exp(alog_ref[...]) * _softplus(sm + dtb_ref[...]), C)
    beta = _pad_rows(jax.nn.sigmoid(sm), C)
    row, col = _tri_masks(C)
    causal = col <= row
    strict = col < row
    eye = jnp.where(row == col, 1.0, 0.0).astype(f32)
    gc = _mm_exact_a(jnp.where(causal, 1.0, 0.0), gfull)
    gate = gate_ref[...]
    ng = ng_ref[...]

    for h in range(NH):
        q = _pad_rows(qkv[:, h * HD:(h + 1) * HD], C)
        k = _pad_rows(qkv[:, (NH + h) * HD:(NH + h + 1) * HD], C)
        v = _pad_rows(qkv[:, (2 * NH + h) * HD:(2 * NH + h + 1) * HD], C)
        qn = q * lax.rsqrt(jnp.sum(q * q, axis=-1, keepdims=True) + 1e-6) * (HD ** -0.5)
        kn = k * lax.rsqrt(jnp.sum(k * k, axis=-1, keepdims=True) + 1e-6)
        gch = gc[:, h:h + 1]
        bh = beta[:, NH + h:NH + h + 1]
        gcb = jnp.broadcast_to(gch, (C, C))
        diff = gcb - jnp.transpose(gcb)
        decay = jnp.where(causal, jnp.exp(jnp.where(causal, diff, 0.0)), 0.0)
        eg = jnp.exp(gch)
        kb = kn * bh
        a_mat = jnp.where(strict, _mm(kb, kn, NT) * decay, 0.0)
        x_inv = eye - a_mat
        p = a_mat
        for lvl in range(levels - 1):
            mm = _mm3 if lvl < hi_levels else _mm
            p = mm(p, p)
            x_inv = x_inv + mm(x_inv, p)
        sol = (_mm3 if hi_levels > 0 else _mm)(x_inv, jnp.concatenate([v * bh, kb * eg], axis=1))
        s = s_scr[h]
        uu = sol[:, :HD] - _mm(sol[:, HD:], s)
        attn = _mm(qn, kn, NT) * decay
        o = _mm(qn * eg, s) + _mm(attn, uu)
        g_end = gch[C - 1:C, :]
        s_scr[h] = s * jnp.exp(g_end) + _mm(kn * jnp.exp(g_end - gch), uu, TN)
        gt = gate[:, h * HD:(h + 1) * HD]
        y = _rms(o[:c_in], ng) * (gt * jax.nn.sigmoid(gt))
        o_ref[:, h * HD:(h + 1) * HD] = y.astype(bf16)

    @pl.when(c == n_chunks - 1)
    def _():
        so_ref[...] = s_scr[...]


def _gdn(z, row_off, nb, t, conv_w, a_log, dt_bias, norm_g, s0=None, prev8=None):
    c_in = min(t, CHUNK)
    n_chunks = t // c_in
    has_state = s0 is not None
    levels = max(1, int(math.log2(c_in)))
    rb = row_off // c_in
    w3 = 3 * NH * HD
    if prev8 is None:
        prev_arr = z
        prev_spec = pl.BlockSpec((8, w3), lambda b, c: (jnp.maximum((row_off + b * t + c * c_in) // 8 - 1, 0), 0))
    else:
        prev_arr = prev8
        prev_spec = pl.BlockSpec((None, 8, w3), lambda b, c: (b, 0, 0))
    pad128 = lambda v: jnp.zeros((1, HD), f32).at[0, :NH].set(v.astype(f32))
    in_specs = [
        pl.BlockSpec((c_in, w3), lambda b, c: (rb + b * n_chunks + c, 0)),
        prev_spec,
        pl.BlockSpec((c_in, NH * HD), lambda b, c: (rb + b * n_chunks + c, 3)),
        pl.BlockSpec((c_in, HD), lambda b, c: (rb + b * n_chunks + c, 64)),
        pl.BlockSpec((4, w3), lambda b, c: (0, 0)),
        pl.BlockSpec((1, HD), lambda b, c: (0, 0)),
        pl.BlockSpec((1, HD), lambda b, c: (0, 0)),
        pl.BlockSpec((1, HD), lambda b, c: (0, 0)),
    ]
    args = [z, prev_arr, z, z, conv_w, pad128(a_log), pad128(dt_bias), norm_g.reshape(1, HD)]
    if has_state:
        in_specs.append(pl.BlockSpec((None, NH, HD, HD), lambda b, c: (b, 0, 0, 0)))
        args.append(s0)
    return pl.pallas_call(
        functools.partial(_gdn_body, c_in=c_in, n_chunks=n_chunks, levels=levels,
                          hi_levels=GDN_HI_LEVELS if c_in == CHUNK else 0, has_state=has_state,
                          first_chunk_zero_prev=prev8 is None),
        out_shape=(jax.ShapeDtypeStruct((nb * t, NH * HD), bf16), jax.ShapeDtypeStruct((nb, NH, HD, HD), f32)),
        grid=(nb, n_chunks),
        in_specs=in_specs,
        out_specs=(pl.BlockSpec((c_in, NH * HD), lambda b, c: (b * n_chunks + c, 0)),
                   pl.BlockSpec((None, NH, HD, HD), lambda b, c: (b, 0, 0, 0))),
        scratch_shapes=[pltpu.VMEM((NH, HD, HD), f32)],
        compiler_params=_cparams(2),
        name="gdn",
    )(*args)


def _mlstm_body(*refs, c_in, n_chunks, has_state):
    if has_state:
        (q_ref, k_ref, v_ref, og_ref, sm_ref, ib_ref, fb_ref, ng_ref, c0_ref, n0_ref, m0_ref,
         o_ref, co_ref, no_ref, mo_ref, c_scr, n_scr, m_scr) = refs
    else:
        (q_ref, k_ref, v_ref, og_ref, sm_ref, ib_ref, fb_ref, ng_ref,
         o_ref, co_ref, no_ref, mo_ref, c_scr, n_scr, m_scr) = refs
    c = pl.program_id(1)
    C = CHUNK

    @pl.when(c == 0)
    def _():
        if has_state:
            c_scr[...] = c0_ref[...]
            n_scr[...] = n0_ref[...]
            m_scr[...] = m0_ref[...]
        else:
            c_scr[...] = jnp.zeros_like(c_scr)
            n_scr[...] = jnp.zeros_like(n_scr)
            m_scr[...] = jnp.zeros_like(m_scr)

    sm = sm_ref[...]
    log_i = _pad_rows(sm + ib_ref[...], C, NEG)
    log_f = _pad_rows(_log_sigmoid(sm + fb_ref[...]), C)
    row, col = _tri_masks(C)
    causal = col <= row
    bcum = _mm_exact_a(jnp.where(causal, 1.0, 0.0), log_f)
    ng = ng_ref[...]
    qa, ka, va, oga = q_ref[...], k_ref[...], v_ref[...], og_ref[...]

    for h in range(NH):
        sl = slice(h * HD, (h + 1) * HD)
        q = _pad_rows(qa[:, sl], C)
        k = _pad_rows(ka[:, sl], C) * (HD ** -0.5)
        v = _pad_rows(va[:, sl], C)
        b = bcum[:, 3 * NH + h:3 * NH + h + 1]
        a = log_i[:, 2 * NH + h:2 * NH + h + 1] - b
        arow = jnp.transpose(jnp.broadcast_to(a, (C, C)))
        cm = jnp.max(jnp.where(causal, arow, -jnp.inf), axis=-1, keepdims=True)
        ms = m_scr[h:h + 1, 0:1]
        m = jnp.maximum(b + ms, b + cm)
        w_inter = jnp.exp(b + ms - m)
        dmat = jnp.where(causal, jnp.exp(jnp.where(causal, b + arow - m, 0.0)), 0.0)
        sc = _mm(q, k, NT) * dmat
        cs = c_scr[h]
        ns = n_scr[h:h + 1, :]
        num = w_inter * _mm(q, cs) + _mm(sc, v)
        den = w_inter * jnp.sum(q * ns, axis=-1, keepdims=True) + jnp.sum(sc, axis=-1, keepdims=True)
        hc = num / (jnp.maximum(jnp.abs(den), jnp.exp(-m)) + MLSTM_EPS)
        m_end = m[C - 1:C, :]
        b_end = b[C - 1:C, :]
        w_end = jnp.exp(b_end + a - m_end)
        d_end = jnp.exp(b_end + ms - m_end)
        kw = k * w_end
        c_scr[h] = d_end * cs + _mm(kw, v, TN)
        n_scr[h:h + 1, :] = d_end * ns + jnp.sum(kw, axis=0, keepdims=True)
        m_scr[h:h + 1, :] = jnp.broadcast_to(m_end, (1, HD))
        og = oga[:, sl]
        o_ref[:, sl] = (_rms(hc[:c_in], ng) * jax.nn.sigmoid(og)).astype(bf16)

    @pl.when(c == n_chunks - 1)
    def _():
        co_ref[...] = c_scr[...]
        no_ref[...] = n_scr[...]
        mo_ref[...] = m_scr[...]


def _mlstm(z, row_off, nb, t, gate_bias, norm_g, c0=None, n0=None, m0=None):
    c_in = min(t, CHUNK)
    n_chunks = t // c_in
    has_state = c0 is not None
    rb = row_off // c_in
    w = NH * HD
    ib = jnp.zeros((1, HD), f32).at[0, 2 * NH:3 * NH].set(gate_bias[0].astype(f32))
    fb = jnp.zeros((1, HD), f32).at[0, 3 * NH:4 * NH].set(gate_bias[1].astype(f32))
    zspec = lambda j: pl.BlockSpec((c_in, w), lambda b, c: (rb + b * n_chunks + c, j))
    in_specs = [zspec(4), zspec(5), zspec(6), zspec(7),
                pl.BlockSpec((c_in, HD), lambda b, c: (rb + b * n_chunks + c, 64)),
                pl.BlockSpec((1, HD), lambda b, c: (0, 0)),
                pl.BlockSpec((1, HD), lambda b, c: (0, 0)),
                pl.BlockSpec((1, HD), lambda b, c: (0, 0))]
    args = [z, z, z, z, z, ib, fb, norm_g.reshape(1, HD)]
    if has_state:
        in_specs += [pl.BlockSpec((None, NH, HD, HD), lambda b, c: (b, 0, 0, 0)),
                     pl.BlockSpec((None, NH, HD), lambda b, c: (b, 0, 0)),
                     pl.BlockSpec((None, NH, HD), lambda b, c: (b, 0, 0))]
        args += [c0, n0, jnp.broadcast_to(m0[..., None], m0.shape + (HD,))]
    return pl.pallas_call(
        functools.partial(_mlstm_body, c_in=c_in, n_chunks=n_chunks, has_state=has_state),
        out_shape=(jax.ShapeDtypeStruct((nb * t, w), bf16), jax.ShapeDtypeStruct((nb, NH, HD, HD), f32),
                   jax.ShapeDtypeStruct((nb, NH, HD), f32), jax.ShapeDtypeStruct((nb, NH, HD), f32)),
        grid=(nb, n_chunks),
        in_specs=in_specs,
        out_specs=(pl.BlockSpec((c_in, w), lambda b, c: (b * n_chunks + c, 0)),
                   pl.BlockSpec((None, NH, HD, HD), lambda b, c: (b, 0, 0, 0)),
                   pl.BlockSpec((None, NH, HD), lambda b, c: (b, 0, 0)),
                   pl.BlockSpec((None, NH, HD), lambda b, c: (b, 0, 0))),
        scratch_shapes=[pltpu.VMEM((NH, HD, HD), f32), pltpu.VMEM((NH, HD), f32), pltpu.VMEM((NH, HD), f32)],
        compiler_params=_cparams(2),
        name="mlstm",
    )(*args)


def _block_masks(c, t):
    row, col = _tri_masks(c)
    same = (row // t) == (col // t)
    return row, col, same


def _gdns_body(u_ref, p_ref, gate_ref, sm_ref, cw_ref, alog_ref, dtb_ref, ng_ref, s0_ref, o_ref, so_ref, *, t, levels):
    C = CHUNK
    nseq = C // t
    u = u_ref[...]
    p = p_ref[...]
    cw = cw_ref[...]
    rowm = _iota2(u.shape, 0) % t
    acc = u * cw[3:4]
    for k in range(1, 4):
        sh = jnp.where(rowm < k, pltpu.roll(p, (k - t) % C, axis=0), pltpu.roll(u, k, axis=0))
        acc = acc + sh * cw[3 - k:4 - k]
    qkv = acc * jax.nn.sigmoid(acc)

    sm = sm_ref[...]
    gfull = -jnp.exp(alog_ref[...]) * _softplus(sm + dtb_ref[...])
    beta = jax.nn.sigmoid(sm)
    row, col, same = _block_masks(C, t)
    causal = (col <= row) & same
    strict = (col < row) & same
    eye = jnp.where(row == col, 1.0, 0.0).astype(f32)
    gc = _mm_exact_a(jnp.where(causal, 1.0, 0.0), gfull)
    gsum = _mm_exact_a(jnp.where(same, 1.0, 0.0), gfull)
    colseq = _iota2((HD, C), 1) // t
    gate = gate_ref[...]
    ng = ng_ref[...]

    for h in range(NH):
        q = qkv[:, h * HD:(h + 1) * HD]
        k = qkv[:, (NH + h) * HD:(NH + h + 1) * HD]
        v = qkv[:, (2 * NH + h) * HD:(2 * NH + h + 1) * HD]
        qn = q * lax.rsqrt(jnp.sum(q * q, axis=-1, keepdims=True) + 1e-6) * (HD ** -0.5)
        kn = k * lax.rsqrt(jnp.sum(k * k, axis=-1, keepdims=True) + 1e-6)
        gch = gc[:, h:h + 1]
        gend = gsum[:, h:h + 1]
        bh = beta[:, NH + h:NH + h + 1]
        gcb = jnp.broadcast_to(gch, (C, C))
        decay = jnp.where(causal, jnp.exp(jnp.where(causal, gcb - jnp.transpose(gcb), 0.0)), 0.0)
        eg = jnp.exp(gch)
        kb = kn * bh
        a_mat = jnp.where(strict, _mm(kb, kn, NT) * decay, 0.0)
        x_inv = eye - a_mat
        pw = a_mat
        for _ in range(levels - 1):
            pw = _mm3(pw, pw)
            x_inv = x_inv + _mm3(x_inv, pw)
        sol = _mm3(x_inv, jnp.concatenate([v * bh, kb * eg], axis=1))
        qe = qn * eg
        u_parts, o_parts = [], []
        for b in range(nseq):
            rs = slice(b * t, (b + 1) * t)
            r2 = _mm(jnp.concatenate([sol[rs, HD:], qe[rs]], axis=0), s0_ref[b, h])
            u_parts.append(sol[rs, :HD] - r2[:t])
            o_parts.append(r2[t:])
        uu = jnp.concatenate(u_parts, axis=0)
        o = jnp.concatenate(o_parts, axis=0) + _mm(_mm(qn, kn, NT) * decay, uu)
        kdec_t = jnp.transpose(kn * jnp.exp(gend - gch))
        for b in range(nseq):
            so_ref[b, h] = (s0_ref[b, h] * jnp.exp(gend[b * t:b * t + 1, :])
                            + _mm(jnp.where(colseq == b, kdec_t, 0.0), uu))
        gt = gate[:, h * HD:(h + 1) * HD]
        o_ref[:, h * HD:(h + 1) * HD] = (_rms(o, ng) * (gt * jax.nn.sigmoid(gt))).astype(bf16)


def _gdn_decode(z, row_off, nb, t, conv_w, a_log, dt_bias, norm_g, s0, prev):
    nseq = CHUNK // t
    assert t == 8 and nb % nseq == 0 and row_off % CHUNK == 0
    rb = row_off // CHUNK
    w3 = 3 * NH * HD
    pad128 = lambda v: jnp.zeros((1, HD), f32).at[0, :NH].set(v.astype(f32))
    return pl.pallas_call(
        functools.partial(_gdns_body, t=t, levels=int(math.log2(t))),
        out_shape=(jax.ShapeDtypeStruct((nb * t, NH * HD), bf16), jax.ShapeDtypeStruct((nb, NH, HD, HD), f32)),
        grid=(nb // nseq,),
        in_specs=[pl.BlockSpec((CHUNK, w3), lambda i: (rb + i, 0)),
                  pl.BlockSpec((CHUNK, w3), lambda i: (i, 0)),
                  pl.BlockSpec((CHUNK, NH * HD), lambda i: (rb + i, 3)),
                  pl.BlockSpec((CHUNK, HD), lambda i: (rb + i, 64)),
                  pl.BlockSpec((4, w3), lambda i: (0, 0)),
                  pl.BlockSpec((1, HD), lambda i: (0, 0)),
                  pl.BlockSpec((1, HD), lambda i: (0, 0)),
                  pl.BlockSpec((1, HD), lambda i: (0, 0)),
                  pl.BlockSpec((nseq, NH, HD, HD), lambda i: (i, 0, 0, 0))],
        out_specs=(pl.BlockSpec((CHUNK, NH * HD), lambda i: (i, 0)),
                   pl.BlockSpec((nseq, NH, HD, HD), lambda i: (i, 0, 0, 0))),
        compiler_params=_cparams(1),
        name="gdn_decode",
    )(z, prev.reshape(nb * t, w3), z, z, conv_w, pad128(a_log), pad128(dt_bias), norm_g.reshape(1, HD), s0)


def _mlstms_body(q_ref, k_ref, v_ref, og_ref, sm_ref, ib_ref, fb_ref, ng_ref, c0_ref, n0_ref, m0_ref,
                 o_ref, co_ref, no_ref, mo_ref, *, t):
    C = CHUNK
    nseq = C // t
    sm = sm_ref[...]
    log_i = sm + ib_ref[...]
    log_f = _log_sigmoid(sm + fb_ref[...])
    m0r = m0_ref[...]
    row, col, same = _block_masks(C, t)
    causal = (col <= row) & same
    bcum = _mm_exact_a(jnp.where(causal, 1.0, 0.0), log_f)
    last_of_row = jnp.where(col == (row // t) * t + (t - 1), 1.0, 0.0)
    last_of_seq = jnp.where(_iota2((nseq, C), 1) == _iota2((nseq, C), 0) * t + (t - 1), 1.0, 0.0)
    seq_rows = jnp.where(_iota2((nseq, C), 1) // t == _iota2((nseq, C), 0), 1.0, 0.0)
    own_seq = _iota2((C, nseq), 1) == _iota2((C, nseq), 0) // t
    colseq = _iota2((HD, C), 1) // t
    lane = _iota2((C, HD), 1)
    ng = ng_ref[...]
    qa, ka, va, oga = q_ref[...], k_ref[...], v_ref[...], og_ref[...]

    for h in range(NH):
        sl = slice(h * HD, (h + 1) * HD)
        q = qa[:, sl]
        k = ka[:, sl] * (HD ** -0.5)
        v = va[:, sl]
        b = bcum[:, 3 * NH + h:3 * NH + h + 1]
        a = log_i[:, 2 * NH + h:2 * NH + h + 1] - b
        arow = jnp.transpose(jnp.broadcast_to(a, (C, C)))
        cm = jnp.max(jnp.where(causal, arow, -jnp.inf), axis=-1, keepdims=True)
        ms = m0r[:, h:h + 1]
        m = jnp.maximum(b + ms, b + cm)
        w_inter = jnp.exp(b + ms - m)
        dmat = jnp.where(causal, jnp.exp(jnp.where(causal, b + arow - m, 0.0)), 0.0)
        sc = _mm(q, k, NT) * dmat
        ends = _mm_exact_a(last_of_row, jnp.where(lane == 0, b, jnp.where(lane == 1, m, 0.0)))
        b_end, m_end = ends[:, 0:1], ends[:, 1:2]
        w_end = jnp.exp(b_end + a - m_end)
        d_end = jnp.exp(b_end + ms - m_end)
        kw = k * w_end
        kw_t = jnp.transpose(kw)
        nh = n0_ref[h]
        qn = jnp.sum(jnp.where(own_seq, _mm(q, nh, NT), 0.0), axis=-1, keepdims=True)
        inter = jnp.concatenate([_mm(q[b_ * t:(b_ + 1) * t], c0_ref[b_, h]) for b_ in range(nseq)], axis=0)
        num = w_inter * inter + _mm(sc, v)
        den = w_inter * qn + jnp.sum(sc, axis=-1, keepdims=True)
        hc = num / (jnp.maximum(jnp.abs(den), jnp.exp(-m)) + MLSTM_EPS)
        for b_ in range(nseq):
            co_ref[b_, h] = d_end[b_ * t:b_ * t + 1, :] * c0_ref[b_, h] + _mm(jnp.where(colseq == b_, kw_t, 0.0), v)
        per_seq = _mm_exact_a(last_of_seq, jnp.where(lane == 0, d_end, jnp.where(lane == 1, m_end, 0.0)))
        no_ref[h] = per_seq[:, 0:1] * nh + _mm_exact_a(seq_rows, kw)
        mo_ref[h] = jnp.broadcast_to(per_seq[:, 1:2], (nseq, HD))
        o_ref[:, sl] = (_rms(hc, ng) * jax.nn.sigmoid(oga[:, sl])).astype(bf16)


def _mlstm_decode(z, row_off, nb, t, gate_bias, norm_g, c0, n0, m0):
    nseq = CHUNK // t
    assert nb % nseq == 0 and row_off % CHUNK == 0
    rb = row_off // CHUNK
    w = NH * HD
    ib = jnp.zeros((1, HD), f32).at[0, 2 * NH:3 * NH].set(gate_bias[0].astype(f32))
    fb = jnp.zeros((1, HD), f32).at[0, 3 * NH:4 * NH].set(gate_bias[1].astype(f32))
    m_rows = jnp.zeros((nb, t, HD), f32).at[:, :, :NH].set(jnp.broadcast_to(m0[:, None, :], (nb, t, NH))).reshape(nb * t, HD)
    zspec = lambda j: pl.BlockSpec((CHUNK, w), lambda i: (rb + i, j))
    hspec = pl.BlockSpec((NH, nseq, HD), lambda i: (0, i, 0))
    out, c_new, n_new, m_new = pl.pallas_call(
        functools.partial(_mlstms_body, t=t),
        out_shape=(jax.ShapeDtypeStruct((nb * t, w), bf16), jax.ShapeDtypeStruct((nb, NH, HD, HD), f32),
                   jax.ShapeDtypeStruct((NH, nb, HD), f32), jax.ShapeDtypeStruct((NH, nb, HD), f32)),
        grid=(nb // nseq,),
        in_specs=[zspec(4), zspec(5), zspec(6), zspec(7),
                  pl.BlockSpec((CHUNK, HD), lambda i: (rb + i, 64)),
                  pl.BlockSpec((1, HD), lambda i: (0, 0)),
                  pl.BlockSpec((1, HD), lambda i: (0, 0)),
                  pl.BlockSpec((1, HD), lambda i: (0, 0)),
                  pl.BlockSpec((nseq, NH, HD, HD), lambda i: (i, 0, 0, 0)),
                  hspec,
                  pl.BlockSpec((CHUNK, HD), lambda i: (i, 0))],
        out_specs=(pl.BlockSpec((CHUNK, w), lambda i: (i, 0)),
                   pl.BlockSpec((nseq, NH, HD, HD), lambda i: (i, 0, 0, 0)), hspec, hspec),
        compiler_params=_cparams(1),
        name="mlstm_decode",
    )(z, z, z, z, z, ib, fb, norm_g.reshape(1, HD), c0, jnp.swapaxes(n0, 0, 1), m_rows)
    return out, c_new, jnp.swapaxes(n_new, 0, 1), jnp.swapaxes(m_new, 0, 1)


def _even_perm():
    w = NH * HD
    offs = np.cumsum([0, 3 * w, NH, NH, w, w, w, w, NH, NH, w])
    seg = lambda i: np.arange(offs[i], offs[i + 1])
    return np.concatenate([seg(0), seg(3), seg(4), seg(5), seg(6), seg(9), seg(1), seg(2), seg(7), seg(8)])


ODD_W = 5760
SCALE = HD ** -0.5


def _kvnorm_body(sk_ref, wk_ref, g_ref, so_ref, wo_ref):
    g = g_ref[...]
    for src, dst, gi in ((sk_ref, so_ref, 1), (wk_ref, wo_ref, 2)):
        x = src[...]
        for j in range(NKV):
            dst[:, j * HD:(j + 1) * HD] = _rms(x[:, j * HD:(j + 1) * HD], g[gi:gi + 1])


def _kvnorm(z, k_g, *, tm=1024):
    m = z.shape[0]
    w = NKV * HD
    return pl.pallas_call(
        _kvnorm_body,
        out_shape=(jax.ShapeDtypeStruct((m, w), f32), jax.ShapeDtypeStruct((m, w), f32)),
        grid=(m // tm,),
        in_specs=[pl.BlockSpec((tm, w), lambda i: (i, 18)), pl.BlockSpec((tm, w), lambda i: (i, 20)),
                  pl.BlockSpec((3, HD), lambda i: (0, 0))],
        out_specs=(pl.BlockSpec((tm, w), lambda i: (i, 0)), pl.BlockSpec((tm, w), lambda i: (i, 0))),
        compiler_params=_cparams(1),
        name="kvnorm",
    )(z, z, k_g)


def _sb_terms(z, mask, mstrict, r):
    sp = jnp.log1p(jnp.exp(-jnp.abs(z)))
    log_beta = jnp.minimum(z, 0.0) - sp
    log_rest = -(jnp.maximum(z, 0.0) + sp)
    if mask is not None:
        log_rest = jnp.where(mask, log_rest, 0.0)
    after = _mm_exact_b(log_rest, mstrict)
    att = jnp.exp(log_beta + after + r)
    if mask is not None:
        att = jnp.where(mask, att, 0.0)
    return att, r + jnp.sum(log_rest, axis=-1, keepdims=True)


SBP_HEADS = 4


def _sbp_body(q_ref, k_ref, v_ref, o_ref):
    i = pl.program_id(2)
    row = _iota2((HD, HD), 0)
    col = _iota2((HD, HD), 1)
    mstrict = jnp.where(row > col, 1.0, 0.0).astype(bf16)
    qs = [q_ref[:, h * HD:(h + 1) * HD] for h in range(SBP_HEADS)]

    def step(s, carry):
        j = i - s
        off = pl.multiple_of(j * HD, HD)
        mask = (col + j * HD) < (row + i * HD)
        out = []
        for h in range(SBP_HEADS):
            acc, r = carry[2 * h:2 * h + 2]
            z = _mm(qs[h], k_ref[pl.ds(off, HD), h * HD:(h + 1) * HD], NT) * SCALE
            att, r = _sb_terms(z, mask, mstrict, r)
            out += [acc + _mm(att, v_ref[pl.ds(off, HD), h * HD:(h + 1) * HD]), r]
        return tuple(out)

    init = tuple(x for _ in range(SBP_HEADS) for x in (jnp.zeros((HD, HD), f32), jnp.zeros((HD, 1), f32)))
    res = lax.fori_loop(0, i + 1, step, init)
    for h in range(SBP_HEADS):
        o_ref[:, h * HD:(h + 1) * HD] = res[2 * h].astype(bf16)


def _sb_prompt(z, nb, t):
    nq = t // HD
    w = SBP_HEADS * HD
    ng = NH // SBP_HEADS
    return pl.pallas_call(
        _sbp_body,
        out_shape=jax.ShapeDtypeStruct((nb * t, NH * HD), bf16),
        grid=(nb, ng, nq),
        in_specs=[pl.BlockSpec((HD, w), lambda b, h, i: (b * nq + i, h)),
                  pl.BlockSpec((t, w), lambda b, h, i: (b, ng + h)),
                  pl.BlockSpec((t, w), lambda b, h, i: (b, 2 * ng + h))],
        out_specs=pl.BlockSpec((HD, w), lambda b, h, i: (b * nq + i, h)),
        compiler_params=_cparams(3),
        name="sb_prompt",
    )(z, z, z)


SBS_PAGES = 4


def _head_rows(page_ref, h, n_heads):
    return page_ref[pl.ds(h, PAGE, stride=n_heads), :]


def _pages2d(cache):
    return cache.reshape(cache.shape[0], cache.shape[1] * cache.shape[2], cache.shape[3])


def _sbs_body(pt_ref, q_ref, kn_ref, vn_ref, *rest, n_steps, t):
    del pt_ref
    kp_refs, vp_refs = rest[:SBS_PAGES], rest[SBS_PAGES:2 * SBS_PAGES]
    o_ref, acc_scr, r_scr = rest[2 * SBS_PAGES:]
    s = pl.program_id(1)

    @pl.when(s == 0)
    def _():
        acc_scr[...] = jnp.zeros_like(acc_scr)
        r_scr[...] = jnp.zeros_like(r_scr)

    rows = NH * t
    row = _iota2((rows, HD), 0)
    col = _iota2((rows, HD), 1)
    mrow = _iota2((HD, HD), 0)
    mcol = _iota2((HD, HD), 1)
    mstrict = jnp.where(mrow > mcol, 1.0, 0.0).astype(bf16)
    qa = q_ref[...]

    def process(state, get_k, get_v, mask):
        r, accs = state
        z = jnp.concatenate([_mm(qa[:, h * HD:(h + 1) * HD], get_k(h), NT) for h in range(NH)], axis=0) * SCALE
        att, r = _sb_terms(z, mask, mstrict, r)
        return r, [accs[h] + _mm(att[h * t:(h + 1) * t], get_v(h)) for h in range(NH)]

    def load():
        return r_scr[:, 0:1], [acc_scr[h * t:(h + 1) * t, :] for h in range(NH)]

    def store(state):
        r_scr[...] = jnp.broadcast_to(state[0], r_scr.shape)
        for h in range(NH):
            acc_scr[h * t:(h + 1) * t, :] = state[1][h]

    @pl.when(s == 0)
    def _():
        store(process(load(), lambda h: _pad_rows(kn_ref[:, h * HD:(h + 1) * HD], HD),
                      lambda h: _pad_rows(vn_ref[:, h * HD:(h + 1) * HD], HD), col < (row % t)))

    state = load()
    for jj in range(SBS_PAGES):
        state = process(state, lambda h: _head_rows(kp_refs[jj], h, NH), lambda h: _head_rows(vp_refs[jj], h, NH), None)
    store(state)

    @pl.when(s == n_steps - 1)
    def _():
        for h in range(NH):
            o_ref[:, h * HD:(h + 1) * HD] = acc_scr[h * t:(h + 1) * t, :].astype(bf16)


def _sb_sample(z, row_off, nb, t, cache_k, cache_v, page_table):
    n_pages = page_table.shape[1]
    n_steps = n_pages // SBS_PAGES
    rb = row_off // t
    page = lambda jj: pl.BlockSpec((None, PAGE * NH, HD),
                                   lambda b, s, pt: (pt[b, n_pages - 1 - (s * SBS_PAGES + jj)], 0, 0))
    pages = [page(jj) for jj in range(SBS_PAGES)]
    grid_spec = pltpu.PrefetchScalarGridSpec(
        num_scalar_prefetch=1,
        grid=(nb, n_steps),
        in_specs=[pl.BlockSpec((t, NH * HD), lambda b, s, pt: (rb + b, 0)),
                  pl.BlockSpec((t, NH * HD), lambda b, s, pt: (rb + b, 1)),
                  pl.BlockSpec((t, NH * HD), lambda b, s, pt: (rb + b, 2))] + pages + pages,
        out_specs=pl.BlockSpec((t, NH * HD), lambda b, s, pt: (b, 0)),
        scratch_shapes=[pltpu.VMEM((NH * t, HD), f32), pltpu.VMEM((NH * t, HD), f32)],
    )
    return pl.pallas_call(
        functools.partial(_sbs_body, n_steps=n_steps, t=t),
        out_shape=jax.ShapeDtypeStruct((nb * t, NH * HD), bf16),
        grid_spec=grid_spec,
        compiler_params=_cparams(2),
        name="sb_sample",
    )(page_table, z, z, z, *([_pages2d(cache_k)] * SBS_PAGES), *([_pages2d(cache_v)] * SBS_PAGES))


def _t5_bucket(dist):
    n = jnp.maximum(dist, 0)
    exact = N_BUCKETS // 2
    nf = jnp.maximum(n, 1).astype(f32)
    large = exact + (jnp.log(nf / exact) / math.log(MAX_DISTANCE / exact) * (N_BUCKETS - exact)).astype(i32)
    return jnp.where(n < exact, n, jnp.minimum(large, N_BUCKETS - 1))


def _bias_of_bucket(bucket, rel_ref, h):
    out = jnp.zeros(bucket.shape, f32)
    for k in range(N_BUCKETS):
        out = jnp.where(bucket == k, rel_ref[k, h], out)
    return out


def _softmax_step(m, l, acc, logits, mask, v):
    lg = jnp.where(mask, logits, NEG)
    m_new = jnp.maximum(m, jnp.max(lg, axis=-1, keepdims=True))
    alpha = jnp.exp(m - m_new)
    e = jnp.where(mask, jnp.exp(lg - m_new), 0.0)
    return m_new, alpha * l + jnp.sum(e, axis=-1, keepdims=True), alpha * acc + _mm(e, v)


def _softmax_full(logits, mask):
    lg = jnp.where(mask, logits, NEG)
    e = jnp.where(mask, jnp.exp(lg - jnp.max(lg, axis=-1, keepdims=True)), 0.0)
    return e / jnp.maximum(jnp.sum(e, axis=-1, keepdims=True), 1e-30)


def _select_blocks(imp_sel, cur, n_sel):
    rows = imp_sel.shape[0]
    imp_pad = jnp.concatenate([imp_sel, jnp.zeros((rows, HD - imp_sel.shape[1]), f32)], axis=1)
    blk = _iota2((rows, HD), 1)
    forced = jnp.where(blk == cur, 2.0 * FORCE_SCORE, jnp.where(blk == 0, FORCE_SCORE, -FORCE_SCORE))
    score = jnp.where((blk < cur) & (blk > 0), imp_pad, forced)
    score = jnp.where(blk < n_sel, score, -jnp.inf)
    sel = jnp.zeros((rows, HD), f32)
    for _ in range(min(TOP_N, n_sel)):
        mx = jnp.max(score, axis=-1, keepdims=True)
        idx = jnp.min(jnp.where(score == mx, blk, 1 << 30), axis=-1, keepdims=True)
        hit = blk == idx
        sel = jnp.where(hit, 1.0, sel)
        score = jnp.where(hit, -jnp.inf, score)
    return sel


def _compress(get_rows, cw_ref, cb, ckg, n_groups_rows):
    del n_groups_rows
    xk = jnp.concatenate([get_rows(0, t).astype(bf16) for t in range(CMP_BLOCK)], axis=1)
    xv = jnp.concatenate([get_rows(1, t).astype(bf16) for t in range(CMP_BLOCK)], axis=1)
    return _rms(_mm(xk, cw_ref[0]) + cb[0:1], ckg), _mm(xv, cw_ref[1]) + cb[1:2]


def _cmp_order(n_cmp, shape, axis):
    c = _iota2(shape, axis)
    half = n_cmp // 2
    return 2 * (c % half) + c // half


def _nsap_body(q_ref, gt_ref, ck0_ref, ck1_ref, cv0_ref, cv1_ref, sk_ref, sv_ref, wk_ref, wv_ref, cw_ref, cb_ref, ckg_ref,
               qg_ref, rel_ref, o_ref, kc_scr, vc_scr, bias_scr, *, t):
    b = pl.program_id(0)
    i = pl.program_id(1)
    n_cmp = t // CMP_BLOCK
    half = n_cmp // 2
    n_sel = t // SEL_BLOCK
    QB = HD
    cmp_refs = ((ck0_ref, ck1_ref), (cv0_ref, cv1_ref))

    @pl.when(i == 0)
    def _():
        for g in range(NKV):
            def get_rows(kind, tt, g=g):
                ref = cmp_refs[kind][g]
                return jnp.concatenate([ref[pl.ds(tt, half, stride=2 * CMP_BLOCK), :],
                                        ref[pl.ds(CMP_BLOCK + tt, half, stride=2 * CMP_BLOCK), :]], axis=0)
            kc, vc = _compress(get_rows, cw_ref, cb_ref[...], ckg_ref[...], n_cmp)
            kc_scr[g] = kc
            vc_scr[g] = vc

    @pl.when((b == 0) & (i == 0))
    def _():
        r_ = _iota2((QB, QB), 0)
        c_ = _iota2((QB, QB), 1)
        for kk in range(3):
            bucket = _t5_bucket(r_ - c_ + QB * kk)
            for h in range(NH):
                bias_scr[kk, h] = _bias_of_bucket(bucket, rel_ref, h)

    qall = q_ref[...]
    qg = qg_ref[...]
    qs = [jnp.concatenate([_rms(qall[:, (g * NREP + r) * HD:(g * NREP + r + 1) * HD], qg) for r in range(NREP)], axis=0)
          for g in range(NKV)]
    rep = lambda x: jnp.concatenate([x] * NREP, axis=0)

    qpos_c = i * QB + _iota2((QB, n_cmp), 0)
    dist_c = qpos_c - (_cmp_order(n_cmp, (QB, n_cmp), 1) * CMP_BLOCK + CMP_BLOCK - 1)
    bucket_c = _t5_bucket(dist_c)
    mask_c = rep(dist_c >= 0)
    cur = (i * QB + _iota2((QB, 1), 0)) // SEL_BLOCK
    o_cmp, sels = [], []
    for g in range(NKV):
        bias = jnp.concatenate([_bias_of_bucket(bucket_c, rel_ref, g * NREP + r) for r in range(NREP)], axis=0)
        p = _softmax_full(_mm(qs[g], kc_scr[g], NT) * SCALE + bias, mask_c)
        o_cmp.append(_mm(p, vc_scr[g]))
        imp = p[0:QB] + p[QB:2 * QB] + p[2 * QB:3 * QB] + p[3 * QB:4 * QB]
        sels.append(_select_blocks(imp[:, :half] + imp[:, half:], cur, n_sel))

    krow = _iota2((QB, QB), 0)
    kcol = _iota2((QB, QB), 1)

    def init():
        return tuple(x for _ in range(NKV) for x in (jnp.full((NREP * QB, 1), NEG, f32), jnp.zeros((NREP * QB, 1), f32),
                                                     jnp.zeros((NREP * QB, HD), f32)))

    def attend(carry, s, k_ref, v_ref, mask_of):
        j = i - s
        off = pl.multiple_of(j * QB, QB)
        delta = jnp.minimum(s, 2)
        out = []
        for g in range(NKV):
            m, l, acc = carry[3 * g:3 * g + 3]
            bias = jnp.concatenate([bias_scr[delta, g * NREP + r] for r in range(NREP)], axis=0)
            logits = _mm(qs[g], k_ref[pl.ds(off, QB), g * HD:(g + 1) * HD], NT) * SCALE + bias
            out.extend(_softmax_step(m, l, acc, logits, rep(mask_of(g, j, s)), v_ref[pl.ds(off, QB), g * HD:(g + 1) * HD]))
        return tuple(out)

    def sel_mask(g, j, s):
        expand = jnp.where(krow == 2 * j + kcol // SEL_BLOCK, 1.0, 0.0)
        return (_mm(sels[g], expand) > 0.5) & ((kcol - krow) <= s * QB)

    def win_mask(g, j, s):
        dist = s * QB + krow - kcol
        return (dist >= 0) & (dist < WINDOW)

    c_sel = lax.fori_loop(0, i + 1, lambda s, c: attend(c, s, sk_ref, sv_ref, sel_mask), init())
    c_win = lax.fori_loop(0, jnp.minimum(i, WINDOW // QB) + 1, lambda s, c: attend(c, s, wk_ref, wv_ref, win_mask), init())

    gts = jax.nn.sigmoid(gt_ref[...])
    for h in range(NH):
        g, r = divmod(h, NREP)
        sl = slice(r * QB, (r + 1) * QB)
        o_s = c_sel[3 * g + 2][sl] / jnp.maximum(c_sel[3 * g + 1][sl], 1e-30)
        o_w = c_win[3 * g + 2][sl] / jnp.maximum(c_win[3 * g + 1][sl], 1e-30)
        o = gts[:, 3 * h:3 * h + 1] * o_cmp[g][sl] + gts[:, 3 * h + 1:3 * h + 2] * o_s + gts[:, 3 * h + 2:3 * h + 3] * o_w
        o_ref[:, h * HD:(h + 1) * HD] = o.astype(bf16)


def _nsa_prompt(z, sk, wk, nb, t, cmp_w, cmp_b, cmp_k_g, q_g, rel_bias):
    nq = t // HD
    n_cmp = t // CMP_BLOCK
    w = NKV * HD
    full = lambda j: pl.BlockSpec((t, w), lambda b, i: (b, j))
    head = lambda j: pl.BlockSpec((t, HD), lambda b, i: (b, j))
    return pl.pallas_call(
        functools.partial(_nsap_body, t=t),
        out_shape=jax.ShapeDtypeStruct((nb * t, NH * HD), bf16),
        grid=(nb, nq),
        in_specs=[pl.BlockSpec((HD, NH * HD), lambda b, i: (b * nq + i, 3)),
                  pl.BlockSpec((HD, HD), lambda b, i: (b * nq + i, 44)),
                  head(32), head(33), head(34), head(35), full(0), full(19), full(0), full(21),
                  pl.BlockSpec((2, CMP_BLOCK * HD, HD), lambda b, i: (0, 0, 0)),
                  pl.BlockSpec((2, HD), lambda b, i: (0, 0)),
                  pl.BlockSpec((1, HD), lambda b, i: (0, 0)),
                  pl.BlockSpec((1, HD), lambda b, i: (0, 0)),
                  pl.BlockSpec(memory_space=pltpu.SMEM)],
        out_specs=pl.BlockSpec((HD, NH * HD), lambda b, i: (b * nq + i, 0)),
        scratch_shapes=[pltpu.VMEM((NKV, n_cmp, HD), f32), pltpu.VMEM((NKV, n_cmp, HD), f32),
                        pltpu.VMEM((3, NH, HD, HD), f32)],
        compiler_params=_cparams(2),
        name="nsa_prompt",
    )(z, z, z, z, z, z, sk, z, wk, z, cmp_w.astype(bf16), cmp_b, cmp_k_g.reshape(1, HD), q_g.reshape(1, HD), rel_bias)


def _nsa_q_groups(q_ref, qg):
    qall = q_ref[...]
    return [jnp.concatenate([_rms(qall[:, (g * NREP + r) * HD:(g * NREP + r + 1) * HD], qg) for r in range(NREP)], axis=0)
            for g in range(NKV)]


CMP_PAGES = 4


def _nsasa_body(pt_ref, q_ref, *rest, n_pages, t):
    del pt_ref
    ckp_refs, cvp_refs = rest[:CMP_PAGES], rest[CMP_PAGES:2 * CMP_PAGES]
    cw_ref, cb_ref, ckg_ref, qg_ref, rel_ref, oc_ref, sel_ref, ck_scr, cv_scr = rest[2 * CMP_PAGES:]
    p = pl.program_id(1)
    for jj in range(CMP_PAGES):
        off = pl.multiple_of((p * CMP_PAGES + jj) * PAGE, PAGE)
        for g in range(NKV):
            ck_scr[g, pl.ds(off, PAGE), :] = _head_rows(ckp_refs[jj], g, NKV)
            cv_scr[g, pl.ds(off, PAGE), :] = _head_rows(cvp_refs[jj], g, NKV)

    @pl.when(p == n_pages // CMP_PAGES - 1)
    def _():
        past = n_pages * PAGE
        n_cmp = (past + t) // CMP_BLOCK
        half = n_cmp // 2
        n_sel = -(-(past + t) // SEL_BLOCK)

        def get_rows(kind, tt):
            scr = ck_scr if kind == 0 else cv_scr
            return jnp.concatenate([scr[g, pl.ds(par * CMP_BLOCK + tt, half, stride=2 * CMP_BLOCK), :]
                                    for g in range(NKV) for par in range(2)], axis=0)

        kc, vc = _compress(get_rows, cw_ref, cb_ref[...], ckg_ref[...], NKV * n_cmp)
        qs = _nsa_q_groups(q_ref, qg_ref[...])
        rows = NREP * t
        tq = _iota2((rows, n_cmp), 0) % t
        dist_c = past + tq - (_cmp_order(n_cmp, (rows, n_cmp), 1) * CMP_BLOCK + CMP_BLOCK - 1)
        bucket_c = _t5_bucket(dist_c)
        mask_c = dist_c >= 0
        cur = (past + _iota2((t, 1), 0)) // SEL_BLOCK
        for g in range(NKV):
            bias = jnp.concatenate([_bias_of_bucket(bucket_c[r * t:(r + 1) * t], rel_ref, g * NREP + r)
                                    for r in range(NREP)], axis=0)
            pr = _softmax_full(_mm(qs[g], kc[g * n_cmp:(g + 1) * n_cmp], NT) * SCALE + bias, mask_c)
            oc_ref[g * rows:(g + 1) * rows, :] = _mm(pr, vc[g * n_cmp:(g + 1) * n_cmp])
            imp = pr[0:t] + pr[t:2 * t] + pr[2 * t:3 * t] + pr[3 * t:4 * t]
            sel_ref[g * t:(g + 1) * t, :] = _select_blocks(imp[:, :half] + imp[:, half:], cur, n_sel)


def _nsa_sample_cmp(z, row_off, nb, t, cache_ck, cache_cv, page_table, cmp_w, cmp_b, cmp_k_g, q_g, rel_bias):
    n_pages = page_table.shape[1]
    assert n_pages % CMP_PAGES == 0
    rb = row_off // t
    page = lambda jj: pl.BlockSpec((None, PAGE * NKV, HD), lambda b, p, pt: (pt[b, p * CMP_PAGES + jj], 0, 0))
    pages = [page(jj) for jj in range(CMP_PAGES)]
    cst = lambda *shape: pl.BlockSpec(shape, lambda b, p, pt: (0,) * len(shape))
    grid_spec = pltpu.PrefetchScalarGridSpec(
        num_scalar_prefetch=1,
        grid=(nb, n_pages // CMP_PAGES),
        in_specs=[pl.BlockSpec((t, NH * HD), lambda b, p, pt: (rb + b, 3))] + pages + pages + [
                  cst(2, CMP_BLOCK * HD, HD), cst(2, HD), cst(1, HD), cst(1, HD),
                  pl.BlockSpec(memory_space=pltpu.SMEM)],
        out_specs=(pl.BlockSpec((None, NH * t, HD), lambda b, p, pt: (b, 0, 0)),
                   pl.BlockSpec((None, NKV * t, HD), lambda b, p, pt: (b, 0, 0))),
        scratch_shapes=[pltpu.VMEM((NKV, n_pages * PAGE, HD), f32), pltpu.VMEM((NKV, n_pages * PAGE, HD), f32)],
    )
    return pl.pallas_call(
        functools.partial(_nsasa_body, n_pages=n_pages, t=t),
        out_shape=(jax.ShapeDtypeStruct((nb, NH * t, HD), f32), jax.ShapeDtypeStruct((nb, NKV * t, HD), f32)),
        grid_spec=grid_spec,
        compiler_params=_cparams(2),
        name="nsa_sample_cmp",
    )(page_table, z, *([_pages2d(cache_ck)] * CMP_PAGES), *([_pages2d(cache_cv)] * CMP_PAGES), cmp_w.astype(bf16), cmp_b,
      cmp_k_g.reshape(1, HD), q_g.reshape(1, HD), rel_bias)


NSA_PAGES = 5


def _nsasb_body(pt_ref, q_ref, gt_ref, oc_ref, sel_ref, skn_ref, svn_ref, wkn_ref, wvn_ref, wk_ref, wv_ref,
                skl_ref, svl_ref, *rest, n_pages, t):
    del pt_ref
    skp_refs, svp_refs = rest[:NSA_PAGES], rest[NSA_PAGES:2 * NSA_PAGES]
    qg_ref, rel_ref, o_ref, m_scr, l_scr, acc_scr, ow_scr = rest[2 * NSA_PAGES:]
    s = pl.program_id(1)
    n_steps = (n_pages - 1) // NSA_PAGES
    past = n_pages * PAGE
    rows = NREP * t
    qs = _nsa_q_groups(q_ref, qg_ref[...])
    tq = _iota2((t, HD), 0)
    col = _iota2((t, HD), 1)
    krow = _iota2((HD, HD), 0)
    kcol = _iota2((HD, HD), 1)
    rep = lambda x: jnp.concatenate([x] * NREP, axis=0)

    def bias_of(dist, g):
        bucket = _t5_bucket(dist)
        return jnp.concatenate([_bias_of_bucket(bucket, rel_ref, g * NREP + r) for r in range(NREP)], axis=0)

    def far_bias(g):
        return jnp.concatenate([jnp.full((t, 1), rel_ref[N_BUCKETS - 1, g * NREP + r], f32) for r in range(NREP)], axis=0)

    def attend(state, g, ks, vs, bias, mask):
        m, l, acc = state
        logits = jnp.concatenate([_mm(qs[g], k, NT) for k in ks], axis=1) * SCALE + bias
        mask = rep(mask)
        lg = jnp.where(mask, logits, NEG)
        m_new = jnp.maximum(m, jnp.max(lg, axis=-1, keepdims=True))
        alpha = jnp.exp(m - m_new)
        e = jnp.where(mask, jnp.exp(lg - m_new), 0.0)
        acc = alpha * acc
        for n, v in enumerate(vs):
            acc = acc + _mm(e[:, n * HD:(n + 1) * HD], v)
        return m_new, alpha * l + jnp.sum(e, axis=-1, keepdims=True), acc

    def sel_update(g, ks, vs, bias, mask):
        sl = slice(g * rows, (g + 1) * rows)
        m, l, acc = attend((m_scr[sl, 0:1], l_scr[sl, 0:1], acc_scr[sl, :]), g, ks, vs, bias, mask)
        m_scr[sl, :] = jnp.broadcast_to(m, (rows, HD))
        l_scr[sl, :] = jnp.broadcast_to(l, (rows, HD))
        acc_scr[sl, :] = acc

    def picked(g, first_block, n_blocks):
        er = _iota2((HD, n_blocks * HD), 0)
        ec = _iota2((HD, n_blocks * HD), 1)
        expand = jnp.where(er == first_block + ec // SEL_BLOCK, 1.0, 0.0)
        return _mm(sel_ref[g * t:(g + 1) * t, :], expand) > 0.5

    @pl.when(s == 0)
    def _():
        m_scr[...] = jnp.full(m_scr.shape, NEG, f32)
        l_scr[...] = jnp.zeros_like(l_scr)
        acc_scr[...] = jnp.zeros_like(acc_scr)
        nwin = WINDOW // HD
        tqw = _iota2((t, WINDOW + HD), 0)
        colw = _iota2((t, WINDOW + HD), 1)
        dist_w = WINDOW + tqw - colw
        ok_w = (dist_w >= 0) & (dist_w < WINDOW) & (colw < WINDOW + t)
        tqs = _iota2((t, 2 * HD), 0)
        cols = _iota2((t, 2 * HD), 1)
        dist_s = PAGE + tqs - cols
        ok_s = (dist_s >= 0) & (cols < PAGE + t)
        for g in range(NKV):
            gs = slice(g * HD, (g + 1) * HD)
            ks = [wk_ref[pl.ds(jj * HD * NKV + g, HD, stride=NKV), :] for jj in range(nwin)] + [_pad_rows(wkn_ref[:, gs], HD)]
            vs = [wv_ref[pl.ds(jj * HD * NKV + g, HD, stride=NKV), :] for jj in range(nwin)] + [_pad_rows(wvn_ref[:, gs], HD)]
            st = (jnp.full((rows, 1), NEG, f32), jnp.zeros((rows, 1), f32), jnp.zeros((rows, HD), f32))
            st = attend(st, g, ks, vs, bias_of(dist_w, g), ok_w)
            ow_scr[g * rows:(g + 1) * rows, :] = st[2] / jnp.maximum(st[1], 1e-30)
            sel_update(g, [_head_rows(skl_ref, g, NKV), _pad_rows(skn_ref[:, gs], HD)],
                       [_head_rows(svl_ref, g, NKV), _pad_rows(svn_ref[:, gs], HD)], bias_of(dist_s, g),
                       picked(g, (past - PAGE) // SEL_BLOCK, 2) & ok_s)

    for g in range(NKV):
        sel_update(g, [_head_rows(skp_refs[jj], g, NKV) for jj in range(NSA_PAGES)],
                   [_head_rows(svp_refs[jj], g, NKV) for jj in range(NSA_PAGES)], far_bias(g),
                   picked(g, (PAGE // SEL_BLOCK) * NSA_PAGES * s, NSA_PAGES))

    @pl.when(s == n_steps - 1)
    def _():
        gts = jax.nn.sigmoid(gt_ref[...])
        for h in range(NH):
            g, r = divmod(h, NREP)
            sl = slice(g * rows + r * t, g * rows + (r + 1) * t)
            o_s = acc_scr[sl, :] / jnp.maximum(l_scr[sl, 0:1], 1e-30)
            o = (gts[:, 3 * h:3 * h + 1] * oc_ref[sl, :] + gts[:, 3 * h + 1:3 * h + 2] * o_s
                 + gts[:, 3 * h + 2:3 * h + 3] * ow_scr[sl, :])
            o_ref[:, h * HD:(h + 1) * HD] = o.astype(bf16)


def _nsa_sample_attn(z, sk, wk, row_off, nb, t, o_cmp, sel, win_k, win_v, cache_sk, cache_sv, page_table, q_g, rel_bias):
    n_pages = page_table.shape[1]
    assert (n_pages - 1) % NSA_PAGES == 0 and PAGE >= MAX_DISTANCE
    rb = row_off // t
    w = NKV * HD
    page = lambda jj: pl.BlockSpec((None, PAGE * NKV, HD), lambda b, s, pt: (pt[b, s * NSA_PAGES + jj], 0, 0))
    pages = [page(jj) for jj in range(NSA_PAGES)]
    last = pl.BlockSpec((None, PAGE * NKV, HD), lambda b, s, pt: (pt[b, n_pages - 1], 0, 0))
    rowsp = lambda width, j: pl.BlockSpec((t, width), lambda b, s, pt: (rb + b, j))
    grid_spec = pltpu.PrefetchScalarGridSpec(
        num_scalar_prefetch=1,
        grid=(nb, (n_pages - 1) // NSA_PAGES),
        in_specs=[rowsp(NH * HD, 3), rowsp(HD, 44),
                  pl.BlockSpec((None, NH * t, HD), lambda b, s, pt: (b, 0, 0)),
                  pl.BlockSpec((None, NKV * t, HD), lambda b, s, pt: (b, 0, 0)),
                  rowsp(w, 0), rowsp(w, 19), rowsp(w, 0), rowsp(w, 21),
                  pl.BlockSpec((None, WINDOW * NKV, HD), lambda b, s, pt: (b, 0, 0)),
                  pl.BlockSpec((None, WINDOW * NKV, HD), lambda b, s, pt: (b, 0, 0)),
                  last, last] + pages + pages + [
                  pl.BlockSpec((1, HD), lambda b, s, pt: (0, 0)),
                  pl.BlockSpec(memory_space=pltpu.SMEM)],
        out_specs=pl.BlockSpec((t, NH * HD), lambda b, s, pt: (b, 0)),
        scratch_shapes=[pltpu.VMEM((NH * t, HD), f32)] * 4,
    )
    return pl.pallas_call(
        functools.partial(_nsasb_body, n_pages=n_pages, t=t),
        out_shape=jax.ShapeDtypeStruct((nb * t, NH * HD), bf16),
        grid_spec=grid_spec,
        compiler_params=_cparams(2),
        name="nsa_sample_attn",
    )(page_table, z, z, o_cmp, sel, sk, z, wk, z, _pages2d(win_k), _pages2d(win_v), _pages2d(cache_sk), _pages2d(cache_sv),
      *([_pages2d(cache_sk)] * NSA_PAGES), *([_pages2d(cache_sv)] * NSA_PAGES), q_g.reshape(1, HD), rel_bias)


def _even_mixer(z, mp, bp, tp, bs, ts, conv_w, a_log, dt_bias, gdn_g, gate_bias, mlstm_g, s0, conv0, c0, n0, m0):
    w3 = 3 * NH * HD
    go_p, gs_p = _gdn(z, 0, bp, tp, conv_w, a_log, dt_bias, gdn_g)
    prev8 = jnp.concatenate([jnp.zeros((bs, 8 - conv0.shape[1], w3), f32), conv0], axis=1)
    go_s, gs_s = _gdn_decode(z, mp, bs, ts, conv_w, a_log, dt_bias, gdn_g, s0, prev8)
    mh_p, mc_p, mn_p, mm_p = _mlstm(z, 0, bp, tp, gate_bias, mlstm_g)
    mh_s, mc_s, mn_s, mm_s = _mlstm_decode(z, mp, bs, ts, gate_bias, mlstm_g, c0, n0, m0)
    mix = jnp.concatenate([jnp.concatenate([go_p, mh_p], axis=1), jnp.concatenate([go_s, mh_s], axis=1)], axis=0)
    keep = conv0.shape[1]
    conv_p = z[:mp, :w3].reshape(bp, tp, w3)[:, tp - keep:]
    conv_s = z[mp:, :w3].reshape(bs, ts, w3)[:, ts - keep:]
    states = (gs_p, gs_s, conv_p, conv_s, mc_p, mc_s, mn_p, mn_s, mm_p[..., 0], mm_s[..., 0])
    return mix, states


def _odd_mixer(z, mp, bp, tp, bs, ts, page_table, caches, win_k0, win_v0, q_g, k_g, cmp_w, cmp_b, rel_bias):
    sb_k, sb_v, cmp_k, cmp_v, sel_k, sel_v = caches
    sk, wk = _kvnorm(z, k_g)
    sb_p = _sb_prompt(z, bp, tp)
    sb_s = _sb_sample(z, mp, bs, ts, sb_k, sb_v, page_table)
    ns_p = _nsa_prompt(z, sk, wk, bp, tp, cmp_w, cmp_b, k_g[0], q_g, rel_bias)
    o_cmp, sel = _nsa_sample_cmp(z, mp, bs, ts, cmp_k, cmp_v, page_table, cmp_w, cmp_b, k_g[0], q_g, rel_bias)
    ns_s = _nsa_sample_attn(z, sk, wk, mp, bs, ts, o_cmp, sel, win_k0, win_v0, sel_k, sel_v, page_table, q_g, rel_bias)
    mix = jnp.concatenate([jnp.concatenate([sb_p, ns_p], axis=1), jnp.concatenate([sb_s, ns_s], axis=1)], axis=0)

    w = NH * HD
    kw = NKV * HD

    def rows(arr, lo, width, heads):
        sl = arr[:, lo:lo + width]
        return sl[:mp].reshape(bp, tp, heads, HD), sl[mp:].reshape(bs, ts, heads, HD)

    sbk_p, sbk_s = rows(z, w, w, NH)
    sbv_p, sbv_s = rows(z, 2 * w, w, NH)
    ck_p, ck_s = rows(z, 4 * w, kw, NKV)
    cv_p, cv_s = rows(z, 4 * w + kw, kw, NKV)
    sk_p, sk_s = rows(sk, 0, kw, NKV)
    sv_p, sv_s = rows(z, 4 * w + 3 * kw, kw, NKV)
    wk_p, wk_s = rows(wk, 0, kw, NKV)
    wv_p, wv_s = rows(z, 4 * w + 5 * kw, kw, NKV)
    keep_p = min(WINDOW, tp)
    win = lambda old, new: jnp.concatenate([old, new], axis=1)[:, -min(WINDOW, old.shape[1] + ts):]
    states = (sbk_p, sbk_s, sbv_p, sbv_s, ck_p, ck_s, cv_p, cv_s, sk_p, sk_s, sv_p, sv_s,
              wk_p[:, tp - keep_p:], win(win_k0, wk_s), wv_p[:, tp - keep_p:], win(win_v0, wv_s))
    return mix, states


def kernel(x_prompt, x_sample, state_gdn_s, state_gdn_conv, state_mlstm_c, state_mlstm_n, state_mlstm_m, cache_sb_k, cache_sb_v, cache_nsa_cmp_k, cache_nsa_cmp_v, cache_nsa_sel_k, cache_nsa_sel_v, state_nsa_win_k, state_nsa_win_v, page_table, norm_g, ffn_w_gate, ffn_w_up, ffn_w_down, even_w_in, gdn_conv_w, gdn_a_log, gdn_dt_bias, gdn_norm_g, mlstm_gate_bias, mlstm_norm_g, even_w_out, odd_w_in, nsa_q_norm_g, nsa_k_norm_g, nsa_cmp_w, nsa_cmp_b, odd_w_out, rel_bias):
    bp, tp, d = x_prompt.shape
    bs, ts, _ = x_sample.shape
    mp = bp * tp
    xs = jnp.concatenate([x_prompt.reshape(mp, d), x_sample.reshape(bs * ts, d)], axis=0)
    depth = norm_g.shape[0]
    even_states, odd_states = [], []
    for layer in range(depth):
        j = layer // 2
        ffn = lambda x, n, i: _ffn(x, norm_g[layer, n], ffn_w_gate[layer, i], ffn_w_up[layer, i], ffn_w_down[layer, i])
        xs = ffn(xs, 0, 0)
        if layer % 2 == 0:
            w_in = even_w_in[j][:, _even_perm()]
            w_in = jnp.concatenate([w_in, jnp.zeros((d, 65 * HD - w_in.shape[1]), f32)], axis=1).astype(bf16)
            z = _rms_matmul(xs, norm_g[layer, 1], w_in)
            mix, st = _even_mixer(z, mp, bp, tp, bs, ts, gdn_conv_w[j], gdn_a_log[j], gdn_dt_bias[j], gdn_norm_g[j],
                                  mlstm_gate_bias[j], mlstm_norm_g[j], state_gdn_s[j], state_gdn_conv[j],
                                  state_mlstm_c[j], state_mlstm_n[j], state_mlstm_m[j])
            even_states.append(st)
            w_out = even_w_out[j]
        else:
            w_in = jnp.concatenate([odd_w_in[j], jnp.zeros((d, ODD_W - odd_w_in.shape[2]), f32)], axis=1).astype(bf16)
            z = _rms_matmul(xs, norm_g[layer, 1], w_in)
            caches = (cache_sb_k[j], cache_sb_v[j], cache_nsa_cmp_k[j], cache_nsa_cmp_v[j], cache_nsa_sel_k[j],
                      cache_nsa_sel_v[j])
            mix, st = _odd_mixer(z, mp, bp, tp, bs, ts, page_table, caches, state_nsa_win_k[j], state_nsa_win_v[j],
                                 nsa_q_norm_g[j], nsa_k_norm_g[j], nsa_cmp_w[j], nsa_cmp_b[j], rel_bias)
            odd_states.append(st)
            w_out = odd_w_out[j]
        xs = _out_proj(mix, w_out.astype(bf16), xs)
        xs = ffn(xs, 2, 1)
    stack = lambda sts, i: jnp.stack([s[i] for s in sts])
    outs = [xs[:mp].reshape(bp, tp, d), xs[mp:].reshape(bs, ts, d)]
    outs += [stack(even_states, i) for i in range(10)]
    outs += [stack(odd_states, i) for i in range(16)]
    return tuple(outs)
```

```python
import functools
import math

import jax
import jax.numpy as jnp
import numpy as np
from jax import lax
from jax.experimental import pallas as pl
from jax.experimental.pallas import tpu as pltpu

f32 = jnp.float32
bf16 = jnp.bfloat16
i32 = jnp.int32

HD = 128
NH = 8
NKV = 2
NREP = 4
RMS_EPS = 1e-6
MLSTM_EPS = 1e-6
CHUNK = 128
PAGE = 128
CMP_BLOCK = 32
SEL_BLOCK = 64
TOP_N = 8
WINDOW = 512
FORCE_SCORE = 1.0e4
N_BUCKETS = 32
MAX_DISTANCE = 128
NEG = -1e30
VMEM_LIMIT = 56 * 1024 * 1024

NN = (((1,), (0,)), ((), ()))
NT = (((1,), (1,)), ((), ()))
TN = (((0,), (0,)), ((), ()))


def _cparams(n_axes):
    return pltpu.CompilerParams(dimension_semantics=("arbitrary",) * n_axes, vmem_limit_bytes=VMEM_LIMIT)


def _mm(a, b, dims=NN):
    return lax.dot_general(a.astype(bf16), b.astype(bf16), dims, preferred_element_type=f32)


def _split3(a):
    a0 = a.astype(bf16)
    r = a - a0.astype(f32)
    a1 = r.astype(bf16)
    a2 = (r - a1.astype(f32)).astype(bf16)
    return a0, a1, a2


def _mm_exact_b(a, b01, dims=NN):
    a0, a1, a2 = _split3(a)
    b01 = b01.astype(bf16)
    d = lambda x: lax.dot_general(x, b01, dims, preferred_element_type=f32)
    return d(a0) + d(a1) + d(a2)


def _mm_exact_a(a01, b, dims=NN):
    b0, b1, b2 = _split3(b)
    a01 = a01.astype(bf16)
    d = lambda x: lax.dot_general(a01, x, dims, preferred_element_type=f32)
    return d(b0) + d(b1) + d(b2)


def _split2(a):
    ah = a.astype(bf16)
    return ah, (a - ah.astype(f32)).astype(bf16)


def _mm_split(a, bs):
    n = a.shape[0]
    rhs = jnp.concatenate([x for b in bs for x in _split2(b)], axis=1)
    r = jnp.dot(jnp.concatenate(_split2(a), axis=0), rhs, preferred_element_type=f32)
    outs, off = [], 0
    for b in bs:
        w = b.shape[1]
        outs.append((r[:n, off:off + w] + r[:n, off + w:off + 2 * w]) + (r[n:, off:off + w] + r[n:, off + w:off + 2 * w]))
        off += 2 * w
    return outs


def _tri_inverse(a_mats, eye, levels):
    ys = [eye - a for a in a_mats]
    if levels < 2:
        return ys
    qs = [_mm_split(a, [a])[0] for a in a_mats]
    for lvl in range(levels - 1):
        last = lvl == levels - 2
        res = [_mm_split(q, [y] if last else [y, q]) for q, y in zip(qs, ys)]
        ys = [y + r[0] for y, r in zip(ys, res)]
        if not last:
            qs = [r[1] for r in res]
    return ys


def _softplus(x):
    return jnp.maximum(x, 0.0) + jnp.log1p(jnp.exp(-jnp.abs(x)))


def _log_sigmoid(x):
    return -_softplus(-x)


def _rms(x, g):
    return x * lax.rsqrt(jnp.mean(x * x, axis=-1, keepdims=True) + RMS_EPS) * g


def _pad_rows(x, rows, value=0.0):
    if x.shape[0] == rows:
        return x
    return jnp.concatenate([x, jnp.full((rows - x.shape[0],) + x.shape[1:], value, x.dtype)], axis=0)


def _iota2(shape, axis):
    return lax.broadcasted_iota(i32, shape, axis)


def _ffn_body(x_ref, g_ref, wg_ref, wu_ref, wd_ref, o_ref, h_ref, *, nf):
    f = pl.program_id(1)

    del nf

    @pl.when(f == 0)
    def _():
        x = x_ref[...]
        h_ref[...] = _rms(x, g_ref[...]).astype(bf16)
        o_ref[...] = x

    h = h_ref[...]
    a = jnp.dot(h, wg_ref[...].astype(bf16), preferred_element_type=f32)
    u = jnp.dot(h, wu_ref[...].astype(bf16), preferred_element_type=f32)
    act = (0.5 * a * jax.nn.sigmoid(a) * u).astype(bf16)
    o_ref[...] += jnp.dot(act, wd_ref[...].astype(bf16), preferred_element_type=f32)


def _ffn(x, g, wg, wu, wd, *, tm=768, tf=512):
    m, d = x.shape
    fdim = wg.shape[1]
    nf = fdim // tf
    once = pl.Buffered(1)
    return pl.pallas_call(
        functools.partial(_ffn_body, nf=nf),
        out_shape=jax.ShapeDtypeStruct((m, d), f32),
        grid=(m // tm, nf),
        in_specs=[
            pl.BlockSpec((tm, d), lambda i, f: (i, 0), pipeline_mode=once),
            pl.BlockSpec((1, d), lambda i, f: (0, 0)),
            pl.BlockSpec((d, tf), lambda i, f: (0, f)),
            pl.BlockSpec((d, tf), lambda i, f: (0, f)),
            pl.BlockSpec((tf, d), lambda i, f: (f, 0)),
        ],
        out_specs=pl.BlockSpec((tm, d), lambda i, f: (i, 0), pipeline_mode=once),
        scratch_shapes=[pltpu.VMEM((tm, d), bf16)],
        compiler_params=_cparams(2),
        name="ffn",
    )(x, g.reshape(1, d), wg, wu, wd)


def _rmsmm_body(x_ref, g_ref, w_ref, o_ref, h_ref):
    @pl.when(pl.program_id(1) == 0)
    def _():
        h_ref[...] = _rms(x_ref[...], g_ref[...]).astype(bf16)

    o_ref[...] = jnp.dot(h_ref[...], w_ref[...], preferred_element_type=f32)


PROJ_TN = 768


def _pad_cols(w, mult):
    pad = -w.shape[1] % mult
    return w if pad == 0 else jnp.concatenate([w, jnp.zeros((w.shape[0], pad), w.dtype)], axis=1)


def _rms_matmul(x, g, w, *, tm=1024, tn=PROJ_TN):
    m, d = x.shape
    n = w.shape[1]
    return pl.pallas_call(
        _rmsmm_body,
        out_shape=jax.ShapeDtypeStruct((m, n), f32),
        grid=(m // tm, n // tn),
        in_specs=[
            pl.BlockSpec((tm, d), lambda i, j: (i, 0), pipeline_mode=pl.Buffered(1)),
            pl.BlockSpec((1, d), lambda i, j: (0, 0)),
            pl.BlockSpec((d, tn), lambda i, j: (0, j)),
        ],
        out_specs=pl.BlockSpec((tm, tn), lambda i, j: (i, j)),
        scratch_shapes=[pltpu.VMEM((tm, d), bf16)],
        compiler_params=_cparams(2),
        name="rms_matmul",
    )(x, g.reshape(1, d), w)


def _outproj_body(a_ref, w_ref, r_ref, o_ref):
    o_ref[...] = r_ref[...] + jnp.dot(a_ref[...], w_ref[...], preferred_element_type=f32)


def _out_proj(a, w, res, *, tm=512, tn=512):
    m, k = a.shape
    n = w.shape[1]
    return pl.pallas_call(
        _outproj_body,
        out_shape=jax.ShapeDtypeStruct((m, n), f32),
        grid=(m // tm, n // tn),
        in_specs=[
            pl.BlockSpec((tm, k), lambda i, j: (i, 0)),
            pl.BlockSpec((k, tn), lambda i, j: (0, j)),
            pl.BlockSpec((tm, tn), lambda i, j: (i, j)),
        ],
        out_specs=pl.BlockSpec((tm, tn), lambda i, j: (i, j)),
        compiler_params=_cparams(2),
        name="out_proj",
    )(a, w, res)


def _tri_masks(c):
    row = _iota2((c, c), 0)
    col = _iota2((c, c), 1)
    return row, col


def _gdn_body(*refs, c_in, n_chunks, levels, has_state, first_chunk_zero_prev):
    if has_state:
        (u_ref, p8_ref, gate_ref, sm_ref, cw_ref, alog_ref, dtb_ref, ng_ref, s0_ref, o_ref, so_ref, s_scr) = refs
    else:
        (u_ref, p8_ref, gate_ref, sm_ref, cw_ref, alog_ref, dtb_ref, ng_ref, o_ref, so_ref, s_scr) = refs
    c = pl.program_id(1)
    C = CHUNK

    @pl.when(c == 0)
    def _():
        if has_state:
            s_scr[...] = s0_ref[...]
        else:
            s_scr[...] = jnp.zeros_like(s_scr)

    u = u_ref[...]
    p8 = p8_ref[...]
    if first_chunk_zero_prev:
        p8 = jnp.where(c == 0, 0.0, p8)
    cw = cw_ref[...]
    row8 = _iota2((8, u.shape[1]), 0)
    acc = u * cw[3:4]
    for k in range(1, 4):
        rolled = pltpu.roll(u, k, axis=0)
        first8 = jnp.where(row8 < k, pltpu.roll(p8, k, axis=0), rolled[0:8])
        sh = first8 if c_in == 8 else jnp.concatenate([first8, rolled[8:]], axis=0)
        acc = acc + sh * cw[3 - k:4 - k]
    qkv = acc * jax.nn.sigmoid(acc)

    sm = sm_ref[...]
    gfull = _pad_rows(-jnp.exp(alog_ref[...]) * _softplus(sm + dtb_ref[...]), C)
    beta = _pad_rows(jax.nn.sigmoid(sm), C)
    row, col = _tri_masks(C)
    causal = col <= row
    strict = col < row
    eye = jnp.where(row == col, 1.0, 0.0).astype(f32)
    gc = _mm_exact_a(jnp.where(causal, 1.0, 0.0), gfull)
    gate = gate_ref[...]
    ng = ng_ref[...]

    heads = range(NH)
    qn, kn, vb, kbe, decay, gch, a_mats, qk = [], [], [], [], [], [], [], []
    for h in heads:
        q = _pad_rows(qkv[:, h * HD:(h + 1) * HD], C)
        k = _pad_rows(qkv[:, (NH + h) * HD:(NH + h + 1) * HD], C)
        v = _pad_rows(qkv[:, (2 * NH + h) * HD:(2 * NH + h + 1) * HD], C)
        qn.append(q * lax.rsqrt(jnp.sum(q * q, axis=-1, keepdims=True) + 1e-6) * (HD ** -0.5))
        kn.append(k * lax.rsqrt(jnp.sum(k * k, axis=-1, keepdims=True) + 1e-6))
        gch.append(gc[:, h:h + 1])
        bh = beta[:, NH + h:NH + h + 1]
        gcb = jnp.broadcast_to(gch[h], (C, C))
        decay.append(jnp.where(causal, jnp.exp(jnp.where(causal, gcb - jnp.transpose(gcb), 0.0)), 0.0))
        kb = kn[h] * bh
        vb.append(v * bh)
        kbe.append(kb * jnp.exp(gch[h]))
        kq = _mm(jnp.concatenate([kb, qn[h]], axis=0), kn[h], NT)
        a_mats.append(jnp.where(strict, kq[:C] * decay[h], 0.0))
        qk.append(kq[C:] * decay[h])
    x_inv = _tri_inverse(a_mats, eye, levels)
    sol = [_mm_split(x_inv[h], [jnp.concatenate([vb[h], kbe[h]], axis=1)])[0] for h in heads]
    from_s = [_mm(jnp.concatenate([sol[h][:, HD:], qn[h] * jnp.exp(gch[h])], axis=0), s_scr[h]) for h in heads]
    uu = [sol[h][:, :HD] - from_s[h][:C] for h in heads]
    g_end = [gch[h][C - 1:C, :] for h in heads]
    from_u = [_mm(jnp.concatenate([qk[h], jnp.transpose(kn[h] * jnp.exp(g_end[h] - gch[h]))], axis=0), uu[h])
              for h in heads]
    for h in heads:
        s_scr[h] = s_scr[h] * jnp.exp(g_end[h]) + from_u[h][C:]
        gt = gate[:, h * HD:(h + 1) * HD]
        y = _rms((from_s[h][C:] + from_u[h][:C])[:c_in], ng) * (gt * jax.nn.sigmoid(gt))
        o_ref[:, h * HD:(h + 1) * HD] = y.astype(bf16)

    @pl.when(c == n_chunks - 1)
    def _():
        so_ref[...] = s_scr[...]


def _gdn(z, row_off, nb, t, conv_w, a_log, dt_bias, norm_g, s0=None, prev8=None):
    c_in = min(t, CHUNK)
    n_chunks = t // c_in
    has_state = s0 is not None
    levels = max(1, int(math.log2(c_in)))
    rb = row_off // c_in
    w3 = 3 * NH * HD
    if prev8 is None:
        prev_arr = z
        prev_spec = pl.BlockSpec((8, w3), lambda b, c: (jnp.maximum((row_off + b * t + c * c_in) // 8 - 1, 0), 0))
    else:
        prev_arr = prev8
        prev_spec = pl.BlockSpec((None, 8, w3), lambda b, c: (b, 0, 0))
    pad128 = lambda v: jnp.zeros((1, HD), f32).at[0, :NH].set(v.astype(f32))
    in_specs = [
        pl.BlockSpec((c_in, w3), lambda b, c: (rb + b * n_chunks + c, 0)),
        prev_spec,
        pl.BlockSpec((c_in, NH * HD), lambda b, c: (rb + b * n_chunks + c, 3)),
        pl.BlockSpec((c_in, HD), lambda b, c: (rb + b * n_chunks + c, 64)),
        pl.BlockSpec((4, w3), lambda b, c: (0, 0)),
        pl.BlockSpec((1, HD), lambda b, c: (0, 0)),
        pl.BlockSpec((1, HD), lambda b, c: (0, 0)),
        pl.BlockSpec((1, HD), lambda b, c: (0, 0)),
    ]
    args = [z, prev_arr, z, z, conv_w, pad128(a_log), pad128(dt_bias), norm_g.reshape(1, HD)]
    if has_state:
        in_specs.append(pl.BlockSpec((None, NH, HD, HD), lambda b, c: (b, 0, 0, 0)))
        args.append(s0)
    return pl.pallas_call(
        functools.partial(_gdn_body, c_in=c_in, n_chunks=n_chunks, levels=levels, has_state=has_state,
                          first_chunk_zero_prev=prev8 is None),
        out_shape=(jax.ShapeDtypeStruct((nb * t, NH * HD), bf16), jax.ShapeDtypeStruct((nb, NH, HD, HD), f32)),
        grid=(nb, n_chunks),
        in_specs=in_specs,
        out_specs=(pl.BlockSpec((c_in, NH * HD), lambda b, c: (b * n_chunks + c, 0)),
                   pl.BlockSpec((None, NH, HD, HD), lambda b, c: (b, 0, 0, 0))),
        scratch_shapes=[pltpu.VMEM((NH, HD, HD), f32)],
        compiler_params=_cparams(2),
        name="gdn",
    )(*args)


def _mlstm_body(*refs, c_in, n_chunks, has_state):
    if has_state:
        (q_ref, k_ref, v_ref, og_ref, sm_ref, ib_ref, fb_ref, ng_ref, c0_ref, n0_ref, m0_ref,
         o_ref, co_ref, no_ref, mo_ref, c_scr, n_scr, m_scr) = refs
    else:
        (q_ref, k_ref, v_ref, og_ref, sm_ref, ib_ref, fb_ref, ng_ref,
         o_ref, co_ref, no_ref, mo_ref, c_scr, n_scr, m_scr) = refs
    c = pl.program_id(1)
    C = CHUNK

    @pl.when(c == 0)
    def _():
        if has_state:
            c_scr[...] = c0_ref[...]
            n_scr[...] = n0_ref[...]
            m_scr[...] = m0_ref[...]
        else:
            c_scr[...] = jnp.zeros_like(c_scr)
            n_scr[...] = jnp.zeros_like(n_scr)
            m_scr[...] = jnp.zeros_like(m_scr)

    sm = sm_ref[...]
    log_i = _pad_rows(sm + ib_ref[...], C, NEG)
    log_f = _pad_rows(_log_sigmoid(sm + fb_ref[...]), C)
    row, col = _tri_masks(C)
    causal = col <= row
    bcum = _mm_exact_a(jnp.where(causal, 1.0, 0.0), log_f)
    ng = ng_ref[...]
    qa, ka, va, oga = q_ref[...], k_ref[...], v_ref[...], og_ref[...]

    heads = range(NH)
    hsl = [slice(h * HD, (h + 1) * HD) for h in heads]
    q = [_pad_rows(qa[:, hsl[h]], C) for h in heads]
    k = [_pad_rows(ka[:, hsl[h]], C) * (HD ** -0.5) for h in heads]
    v = [_pad_rows(va[:, hsl[h]], C) for h in heads]
    qk = [_mm(q[h], k[h], NT) for h in heads]
    inter = [_mm(q[h], c_scr[h]) for h in heads]
    m, w_inter, sc, a, b, ms = [], [], [], [], [], []
    for h in heads:
        b.append(bcum[:, 3 * NH + h:3 * NH + h + 1])
        a.append(log_i[:, 2 * NH + h:2 * NH + h + 1] - b[h])
        arow = jnp.transpose(jnp.broadcast_to(a[h], (C, C)))
        cm = jnp.max(jnp.where(causal, arow, -jnp.inf), axis=-1, keepdims=True)
        ms.append(m_scr[h:h + 1, 0:1])
        m.append(jnp.maximum(b[h] + ms[h], b[h] + cm))
        w_inter.append(jnp.exp(b[h] + ms[h] - m[h]))
        sc.append(qk[h] * jnp.where(causal, jnp.exp(jnp.where(causal, b[h] + arow - m[h], 0.0)), 0.0))
    kw, d_end = [], []
    for h in heads:
        m_end = m[h][C - 1:C, :]
        b_end = b[h][C - 1:C, :]
        kw.append(k[h] * jnp.exp(b_end + a[h] - m_end))
        d_end.append(jnp.exp(b_end + ms[h] - m_end))
    from_v = [_mm(jnp.concatenate([sc[h], jnp.transpose(kw[h])], axis=0), v[h]) for h in heads]
    for h in heads:
        ns = n_scr[h:h + 1, :]
        num = w_inter[h] * inter[h] + from_v[h][:C]
        den = w_inter[h] * jnp.sum(q[h] * ns, axis=-1, keepdims=True) + jnp.sum(sc[h], axis=-1, keepdims=True)
        hc = num / (jnp.maximum(jnp.abs(den), jnp.exp(-m[h])) + MLSTM_EPS)
        c_scr[h] = d_end[h] * c_scr[h] + from_v[h][C:]
        n_scr[h:h + 1, :] = d_end[h] * ns + jnp.sum(kw[h], axis=0, keepdims=True)
        m_scr[h:h + 1, :] = jnp.broadcast_to(m[h][C - 1:C, :], (1, HD))
        o_ref[:, hsl[h]] = (_rms(hc[:c_in], ng) * jax.nn.sigmoid(oga[:, hsl[h]])).astype(bf16)

    @pl.when(c == n_chunks - 1)
    def _():
        co_ref[...] = c_scr[...]
        no_ref[...] = n_scr[...]
        mo_ref[...] = m_scr[...]


def _mlstm(z, row_off, nb, t, gate_bias, norm_g, c0=None, n0=None, m0=None):
    c_in = min(t, CHUNK)
    n_chunks = t // c_in
    has_state = c0 is not None
    rb = row_off // c_in
    w = NH * HD
    ib = jnp.zeros((1, HD), f32).at[0, 2 * NH:3 * NH].set(gate_bias[0].astype(f32))
    fb = jnp.zeros((1, HD), f32).at[0, 3 * NH:4 * NH].set(gate_bias[1].astype(f32))
    zspec = lambda j: pl.BlockSpec((c_in, w), lambda b, c: (rb + b * n_chunks + c, j))
    in_specs = [zspec(4), zspec(5), zspec(6), zspec(7),
                pl.BlockSpec((c_in, HD), lambda b, c: (rb + b * n_chunks + c, 64)),
                pl.BlockSpec((1, HD), lambda b, c: (0, 0)),
                pl.BlockSpec((1, HD), lambda b, c: (0, 0)),
                pl.BlockSpec((1, HD), lambda b, c: (0, 0))]
    args = [z, z, z, z, z, ib, fb, norm_g.reshape(1, HD)]
    if has_state:
        in_specs += [pl.BlockSpec((None, NH, HD, HD), lambda b, c: (b, 0, 0, 0)),
                     pl.BlockSpec((None, NH, HD), lambda b, c: (b, 0, 0)),
                     pl.BlockSpec((None, NH, HD), lambda b, c: (b, 0, 0))]
        args += [c0, n0, jnp.broadcast_to(m0[..., None], m0.shape + (HD,))]
    return pl.pallas_call(
        functools.partial(_mlstm_body, c_in=c_in, n_chunks=n_chunks, has_state=has_state),
        out_shape=(jax.ShapeDtypeStruct((nb * t, w), bf16), jax.ShapeDtypeStruct((nb, NH, HD, HD), f32),
                   jax.ShapeDtypeStruct((nb, NH, HD), f32), jax.ShapeDtypeStruct((nb, NH, HD), f32)),
        grid=(nb, n_chunks),
        in_specs=in_specs,
        out_specs=(pl.BlockSpec((c_in, w), lambda b, c: (b * n_chunks + c, 0)),
                   pl.BlockSpec((None, NH, HD, HD), lambda b, c: (b, 0, 0, 0)),
                   pl.BlockSpec((None, NH, HD), lambda b, c: (b, 0, 0)),
                   pl.BlockSpec((None, NH, HD), lambda b, c: (b, 0, 0))),
        scratch_shapes=[pltpu.VMEM((NH, HD, HD), f32), pltpu.VMEM((NH, HD), f32), pltpu.VMEM((NH, HD), f32)],
        compiler_params=_cparams(2),
        name="mlstm",
    )(*args)


def _block_masks(c, t):
    row, col = _tri_masks(c)
    same = (row // t) == (col // t)
    return row, col, same


def _gdns_body(u_ref, p_ref, gate_ref, sm_ref, cw_ref, alog_ref, dtb_ref, ng_ref, s0_ref, o_ref, so_ref, *, t, levels):
    C = CHUNK
    nseq = C // t
    u = u_ref[...]
    p = p_ref[...]
    cw = cw_ref[...]
    rowm = _iota2(u.shape, 0) % t
    acc = u * cw[3:4]
    for k in range(1, 4):
        sh = jnp.where(rowm < k, pltpu.roll(p, (k - t) % C, axis=0), pltpu.roll(u, k, axis=0))
        acc = acc + sh * cw[3 - k:4 - k]
    qkv = acc * jax.nn.sigmoid(acc)

    sm = sm_ref[...]
    gfull = -jnp.exp(alog_ref[...]) * _softplus(sm + dtb_ref[...])
    beta = jax.nn.sigmoid(sm)
    row, col, same = _block_masks(C, t)
    causal = (col <= row) & same
    strict = (col < row) & same
    eye = jnp.where(row == col, 1.0, 0.0).astype(f32)
    gc = _mm_exact_a(jnp.where(causal, 1.0, 0.0), gfull)
    gsum = _mm_exact_a(jnp.where(same, 1.0, 0.0), gfull)
    colseq = _iota2((HD, C), 1) // t
    gate = gate_ref[...]
    ng = ng_ref[...]

    for h in range(NH):
        q = qkv[:, h * HD:(h + 1) * HD]
        k = qkv[:, (NH + h) * HD:(NH + h + 1) * HD]
        v = qkv[:, (2 * NH + h) * HD:(2 * NH + h + 1) * HD]
        qn = q * lax.rsqrt(jnp.sum(q * q, axis=-1, keepdims=True) + 1e-6) * (HD ** -0.5)
        kn = k * lax.rsqrt(jnp.sum(k * k, axis=-1, keepdims=True) + 1e-6)
        gch = gc[:, h:h + 1]
        gend = gsum[:, h:h + 1]
        bh = beta[:, NH + h:NH + h + 1]
        gcb = jnp.broadcast_to(gch, (C, C))
        decay = jnp.where(causal, jnp.exp(jnp.where(causal, gcb - jnp.transpose(gcb), 0.0)), 0.0)
        eg = jnp.exp(gch)
        kb = kn * bh
        kq = _mm(jnp.concatenate([kb, qn], axis=0), kn, NT)
        a_mat = jnp.where(strict, kq[:C] * decay, 0.0)
        (x_inv,) = _tri_inverse([a_mat], eye, levels)
        (sol,) = _mm_split(x_inv, [jnp.concatenate([v * bh, kb * eg], axis=1)])
        qe = qn * eg
        u_parts, o_parts = [], []
        for b in range(nseq):
            rs = slice(b * t, (b + 1) * t)
            r2 = _mm(jnp.concatenate([sol[rs, HD:], qe[rs]], axis=0), s0_ref[b, h])
            u_parts.append(sol[rs, :HD] - r2[:t])
            o_parts.append(r2[t:])
        uu = jnp.concatenate(u_parts, axis=0)
        o = jnp.concatenate(o_parts, axis=0) + _mm(kq[C:] * decay, uu)
        kdec_t = jnp.transpose(kn * jnp.exp(gend - gch))
        for b in range(nseq):
            so_ref[b, h] = (s0_ref[b, h] * jnp.exp(gend[b * t:b * t + 1, :])
                            + _mm(jnp.where(colseq == b, kdec_t, 0.0), uu))
        gt = gate[:, h * HD:(h + 1) * HD]
        o_ref[:, h * HD:(h + 1) * HD] = (_rms(o, ng) * (gt * jax.nn.sigmoid(gt))).astype(bf16)


def _gdn_decode(z, row_off, nb, t, conv_w, a_log, dt_bias, norm_g, s0, prev):
    nseq = CHUNK // t
    assert t == 8 and nb % nseq == 0 and row_off % CHUNK == 0
    rb = row_off // CHUNK
    w3 = 3 * NH * HD
    pad128 = lambda v: jnp.zeros((1, HD), f32).at[0, :NH].set(v.astype(f32))
    return pl.pallas_call(
        functools.partial(_gdns_body, t=t, levels=int(math.log2(t))),
        out_shape=(jax.ShapeDtypeStruct((nb * t, NH * HD), bf16), jax.ShapeDtypeStruct((nb, NH, HD, HD), f32)),
        grid=(nb // nseq,),
        in_specs=[pl.BlockSpec((CHUNK, w3), lambda i: (rb + i, 0)),
                  pl.BlockSpec((CHUNK, w3), lambda i: (i, 0)),
                  pl.BlockSpec((CHUNK, NH * HD), lambda i: (rb + i, 3)),
                  pl.BlockSpec((CHUNK, HD), lambda i: (rb + i, 64)),
                  pl.BlockSpec((4, w3), lambda i: (0, 0)),
                  pl.BlockSpec((1, HD), lambda i: (0, 0)),
                  pl.BlockSpec((1, HD), lambda i: (0, 0)),
                  pl.BlockSpec((1, HD), lambda i: (0, 0)),
                  pl.BlockSpec((nseq, NH, HD, HD), lambda i: (i, 0, 0, 0))],
        out_specs=(pl.BlockSpec((CHUNK, NH * HD), lambda i: (i, 0)),
                   pl.BlockSpec((nseq, NH, HD, HD), lambda i: (i, 0, 0, 0))),
        compiler_params=_cparams(1),
        name="gdn_decode",
    )(z, prev.reshape(nb * t, w3), z, z, conv_w, pad128(a_log), pad128(dt_bias), norm_g.reshape(1, HD), s0)


def _mlstms_body(q_ref, k_ref, v_ref, og_ref, sm_ref, ib_ref, fb_ref, ng_ref, c0_ref, n0_ref, m0_ref,
                 o_ref, co_ref, no_ref, mo_ref, *, t):
    C = CHUNK
    nseq = C // t
    sm = sm_ref[...]
    log_i = sm + ib_ref[...]
    log_f = _log_sigmoid(sm + fb_ref[...])
    m0r = m0_ref[...]
    row, col, same = _block_masks(C, t)
    causal = (col <= row) & same
    bcum = _mm_exact_a(jnp.where(causal, 1.0, 0.0), log_f)
    last_of_row = jnp.where(col == (row // t) * t + (t - 1), 1.0, 0.0)
    last_of_seq = jnp.where(_iota2((nseq, C), 1) == _iota2((nseq, C), 0) * t + (t - 1), 1.0, 0.0)
    seq_rows = jnp.where(_iota2((nseq, C), 1) // t == _iota2((nseq, C), 0), 1.0, 0.0)
    own_seq = _iota2((C, nseq), 1) == _iota2((C, nseq), 0) // t
    colseq = _iota2((HD, C), 1) // t
    lane = _iota2((C, HD), 1)
    ng = ng_ref[...]
    qa, ka, va, oga = q_ref[...], k_ref[...], v_ref[...], og_ref[...]

    for h in range(NH):
        sl = slice(h * HD, (h + 1) * HD)
        q = qa[:, sl]
        k = ka[:, sl] * (HD ** -0.5)
        v = va[:, sl]
        b = bcum[:, 3 * NH + h:3 * NH + h + 1]
        a = log_i[:, 2 * NH + h:2 * NH + h + 1] - b
        arow = jnp.transpose(jnp.broadcast_to(a, (C, C)))
        cm = jnp.max(jnp.where(causal, arow, -jnp.inf), axis=-1, keepdims=True)
        ms = m0r[:, h:h + 1]
        m = jnp.maximum(b + ms, b + cm)
        w_inter = jnp.exp(b + ms - m)
        dmat = jnp.where(causal, jnp.exp(jnp.where(causal, b + arow - m, 0.0)), 0.0)
        sc = _mm(q, k, NT) * dmat
        ends = _mm_exact_a(last_of_row, jnp.where(lane == 0, b, jnp.where(lane == 1, m, 0.0)))
        b_end, m_end = ends[:, 0:1], ends[:, 1:2]
        w_end = jnp.exp(b_end + a - m_end)
        d_end = jnp.exp(b_end + ms - m_end)
        kw = k * w_end
        kw_t = jnp.transpose(kw)
        nh = n0_ref[h]
        qn = jnp.sum(jnp.where(own_seq, _mm(q, nh, NT), 0.0), axis=-1, keepdims=True)
        inter = jnp.concatenate([_mm(q[b_ * t:(b_ + 1) * t], c0_ref[b_, h]) for b_ in range(nseq)], axis=0)
        num = w_inter * inter + _mm(sc, v)
        den = w_inter * qn + jnp.sum(sc, axis=-1, keepdims=True)
        hc = num / (jnp.maximum(jnp.abs(den), jnp.exp(-m)) + MLSTM_EPS)
        for b_ in range(nseq):
            co_ref[b_, h] = d_end[b_ * t:b_ * t + 1, :] * c0_ref[b_, h] + _mm(jnp.where(colseq == b_, kw_t, 0.0), v)
        per_seq = _mm_exact_a(last_of_seq, jnp.where(lane == 0, d_end, jnp.where(lane == 1, m_end, 0.0)))
        no_ref[h] = per_seq[:, 0:1] * nh + _mm_exact_a(seq_rows, kw)
        mo_ref[h] = jnp.broadcast_to(per_seq[:, 1:2], (nseq, HD))
        o_ref[:, sl] = (_rms(hc, ng) * jax.nn.sigmoid(oga[:, sl])).astype(bf16)


def _mlstm_decode(z, row_off, nb, t, gate_bias, norm_g, c0, n0, m0):
    nseq = CHUNK // t
    assert nb % nseq == 0 and row_off % CHUNK == 0
    rb = row_off // CHUNK
    w = NH * HD
    ib = jnp.zeros((1, HD), f32).at[0, 2 * NH:3 * NH].set(gate_bias[0].astype(f32))
    fb = jnp.zeros((1, HD), f32).at[0, 3 * NH:4 * NH].set(gate_bias[1].astype(f32))
    m_rows = jnp.zeros((nb, t, HD), f32).at[:, :, :NH].set(jnp.broadcast_to(m0[:, None, :], (nb, t, NH))).reshape(nb * t, HD)
    zspec = lambda j: pl.BlockSpec((CHUNK, w), lambda i: (rb + i, j))
    hspec = pl.BlockSpec((NH, nseq, HD), lambda i: (0, i, 0))
    out, c_new, n_new, m_new = pl.pallas_call(
        functools.partial(_mlstms_body, t=t),
        out_shape=(jax.ShapeDtypeStruct((nb * t, w), bf16), jax.ShapeDtypeStruct((nb, NH, HD, HD), f32),
                   jax.ShapeDtypeStruct((NH, nb, HD), f32), jax.ShapeDtypeStruct((NH, nb, HD), f32)),
        grid=(nb // nseq,),
        in_specs=[zspec(4), zspec(5), zspec(6), zspec(7),
                  pl.BlockSpec((CHUNK, HD), lambda i: (rb + i, 64)),
                  pl.BlockSpec((1, HD), lambda i: (0, 0)),
                  pl.BlockSpec((1, HD), lambda i: (0, 0)),
                  pl.BlockSpec((1, HD), lambda i: (0, 0)),
                  pl.BlockSpec((nseq, NH, HD, HD), lambda i: (i, 0, 0, 0)),
                  hspec,
                  pl.BlockSpec((CHUNK, HD), lambda i: (i, 0))],
        out_specs=(pl.BlockSpec((CHUNK, w), lambda i: (i, 0)),
                   pl.BlockSpec((nseq, NH, HD, HD), lambda i: (i, 0, 0, 0)), hspec, hspec),
        compiler_params=_cparams(1),
        name="mlstm_decode",
    )(z, z, z, z, z, ib, fb, norm_g.reshape(1, HD), c0, jnp.swapaxes(n0, 0, 1), m_rows)
    return out, c_new, jnp.swapaxes(n_new, 0, 1), jnp.swapaxes(m_new, 0, 1)


def _even_perm():
    w = NH * HD
    offs = np.cumsum([0, 3 * w, NH, NH, w, w, w, w, NH, NH, w])
    seg = lambda i: np.arange(offs[i], offs[i + 1])
    return np.concatenate([seg(0), seg(3), seg(4), seg(5), seg(6), seg(9), seg(1), seg(2), seg(7), seg(8)])


SCALE = HD ** -0.5


def _kvnorm_body(sk_ref, wk_ref, g_ref, so_ref, wo_ref):
    g = g_ref[...]
    for src, dst, gi in ((sk_ref, so_ref, 1), (wk_ref, wo_ref, 2)):
        x = src[...]
        for j in range(NKV):
            dst[:, j * HD:(j + 1) * HD] = _rms(x[:, j * HD:(j + 1) * HD], g[gi:gi + 1])


def _kvnorm(z, k_g, *, tm=1024):
    m = z.shape[0]
    w = NKV * HD
    return pl.pallas_call(
        _kvnorm_body,
        out_shape=(jax.ShapeDtypeStruct((m, w), f32), jax.ShapeDtypeStruct((m, w), f32)),
        grid=(m // tm,),
        in_specs=[pl.BlockSpec((tm, w), lambda i: (i, 18)), pl.BlockSpec((tm, w), lambda i: (i, 20)),
                  pl.BlockSpec((3, HD), lambda i: (0, 0))],
        out_specs=(pl.BlockSpec((tm, w), lambda i: (i, 0)), pl.BlockSpec((tm, w), lambda i: (i, 0))),
        compiler_params=_cparams(1),
        name="kvnorm",
    )(z, z, k_g)


SBP_HEADS = 4
SBP_QB = 256


def _sb_block_multi(zs, mask, mstrict, rs):
    n = zs[0].shape[0]
    log_betas, log_rests, parts = [], [], []
    for z in zs:
        sp = jnp.log1p(jnp.exp(-jnp.abs(z)))
        log_betas.append(jnp.minimum(z, 0.0) - sp)
        lr = -(jnp.maximum(z, 0.0) + sp)
        if mask is not None:
            lr = jnp.where(mask, lr, 0.0)
        log_rests.append(lr)
        parts.extend(_split2(lr))
    after = jnp.dot(jnp.concatenate(parts, axis=0), mstrict, preferred_element_type=f32)
    atts, new_rs = [], []
    for h, (lb, lr, r) in enumerate(zip(log_betas, log_rests, rs)):
        att = jnp.exp(lb + after[2 * h * n:(2 * h + 1) * n] + after[(2 * h + 1) * n:(2 * h + 2) * n] + r)
        atts.append(att if mask is None else jnp.where(mask, att, 0.0))
        new_rs.append(r + jnp.sum(lr, axis=-1, keepdims=True))
    return atts, new_rs


def _sbp_body(q_ref, k_ref, v_ref, o_ref):
    i = pl.program_id(2)
    row = _iota2((SBP_QB, HD), 0)
    col = _iota2((SBP_QB, HD), 1)
    mstrict = jnp.where(_iota2((HD, HD), 0) > _iota2((HD, HD), 1), 1.0, 0.0).astype(bf16)
    qs = [q_ref[:, h * HD:(h + 1) * HD].astype(bf16) for h in range(SBP_HEADS)]
    n_kb = (i + 1) * (SBP_QB // HD)

    def step(s, carry):
        j = n_kb - 1 - s
        off = pl.multiple_of(j * HD, HD)
        mask = (col + j * HD) < (row + i * SBP_QB)
        zs = [_mm(qs[h], k_ref[pl.ds(off, HD), h * HD:(h + 1) * HD], NT) * SCALE for h in range(SBP_HEADS)]
        atts, rs = _sb_block_multi(zs, mask, mstrict, carry[SBP_HEADS:])
        accs = [carry[h] + _mm(atts[h], v_ref[pl.ds(off, HD), h * HD:(h + 1) * HD]) for h in range(SBP_HEADS)]
        return tuple(accs) + tuple(rs)

    init = tuple([jnp.zeros((SBP_QB, HD), f32)] * SBP_HEADS + [jnp.zeros((SBP_QB, 1), f32)] * SBP_HEADS)
    res = lax.fori_loop(0, n_kb, step, init)
    for h in range(SBP_HEADS):
        o_ref[:, h * HD:(h + 1) * HD] = res[h].astype(bf16)


def _sb_prompt(z, nb, t):
    nq = t // SBP_QB
    w = SBP_HEADS * HD
    ng = NH // SBP_HEADS
    return pl.pallas_call(
        _sbp_body,
        out_shape=jax.ShapeDtypeStruct((nb * t, NH * HD), bf16),
        grid=(nb, ng, nq),
        in_specs=[pl.BlockSpec((SBP_QB, w), lambda b, h, i: (b * nq + i, h)),
                  pl.BlockSpec((t, w), lambda b, h, i: (b, ng + h)),
                  pl.BlockSpec((t, w), lambda b, h, i: (b, 2 * ng + h))],
        out_specs=pl.BlockSpec((SBP_QB, w), lambda b, h, i: (b * nq + i, h)),
        compiler_params=_cparams(3),
        name="sb_prompt",
    )(z, z, z)


SBS_PAGES = 4


def _head_rows(page_ref, h, n_heads):
    return page_ref[pl.ds(h, PAGE, stride=n_heads), :]


def _pages2d(cache):
    return cache.reshape(cache.shape[0], cache.shape[1] * cache.shape[2], cache.shape[3])


def _sbs_body(pt_ref, q_ref, kn_ref, vn_ref, *rest, n_steps, t):
    del pt_ref
    kp_refs, vp_refs = rest[:SBS_PAGES], rest[SBS_PAGES:2 * SBS_PAGES]
    o_ref, acc_scr, r_scr = rest[2 * SBS_PAGES:]
    s = pl.program_id(1)

    @pl.when(s == 0)
    def _():
        acc_scr[...] = jnp.zeros_like(acc_scr)
        r_scr[...] = jnp.zeros_like(r_scr)

    rows = NH * t
    row = _iota2((rows, HD), 0)
    col = _iota2((rows, HD), 1)
    mrow = _iota2((HD, HD), 0)
    mcol = _iota2((HD, HD), 1)
    mstrict = jnp.where(mrow > mcol, 1.0, 0.0).astype(bf16)
    qa = q_ref[...]

    def load():
        return r_scr[:, 0:1], [acc_scr[h * t:(h + 1) * t, :] for h in range(NH)]

    def store(r, accs):
        r_scr[...] = jnp.broadcast_to(r, r_scr.shape)
        for h in range(NH):
            acc_scr[h * t:(h + 1) * t, :] = accs[h]

    @pl.when(s == 0)
    def _():
        r, accs = load()
        z = jnp.concatenate([_mm(qa[:, h * HD:(h + 1) * HD], _pad_rows(kn_ref[:, h * HD:(h + 1) * HD], HD), NT)
                             for h in range(NH)], axis=0) * SCALE
        (att,), (r,) = _sb_block_multi([z], col < (row % t), mstrict, [r])
        store(r, [accs[h] + _mm(att[h * t:(h + 1) * t], _pad_rows(vn_ref[:, h * HD:(h + 1) * HD], HD))
                  for h in range(NH)])

    slots = range(SBS_PAGES - 1, -1, -1)
    r, accs = load()
    z = jnp.concatenate(
        [_mm(qa[:, h * HD:(h + 1) * HD],
             jnp.concatenate([_head_rows(kp_refs[jj], h, NH).astype(bf16) for jj in slots], axis=0), NT)
         for h in range(NH)], axis=0) * SCALE
    sp = jnp.log1p(jnp.exp(-jnp.abs(z)))
    log_beta = jnp.minimum(z, 0.0) - sp
    log_rest = -(jnp.maximum(z, 0.0) + sp)
    blocks = [log_rest[:, c * HD:(c + 1) * HD] for c in range(SBS_PAGES)]
    after = jnp.dot(jnp.concatenate([x for blk in blocks for x in _split2(blk)], axis=0), mstrict,
                    preferred_element_type=f32)
    atts = [None] * SBS_PAGES
    for c in range(SBS_PAGES - 1, -1, -1):
        aft = after[2 * c * rows:(2 * c + 1) * rows] + after[(2 * c + 1) * rows:(2 * c + 2) * rows]
        atts[c] = jnp.exp(log_beta[:, c * HD:(c + 1) * HD] + aft + r)
        r = r + jnp.sum(blocks[c], axis=-1, keepdims=True)
    att = jnp.concatenate(atts, axis=1)
    store(r, [accs[h] + _mm(att[h * t:(h + 1) * t],
                            jnp.concatenate([_head_rows(vp_refs[jj], h, NH).astype(bf16) for jj in slots], axis=0))
              for h in range(NH)])

    @pl.when(s == n_steps - 1)
    def _():
        for h in range(NH):
            o_ref[:, h * HD:(h + 1) * HD] = acc_scr[h * t:(h + 1) * t, :].astype(bf16)


def _sb_sample(z, row_off, nb, t, cache_k, cache_v, page_table):
    n_pages = page_table.shape[1]
    n_steps = n_pages // SBS_PAGES
    rb = row_off // t
    page = lambda jj: pl.BlockSpec((None, PAGE * NH, HD),
                                   lambda b, s, pt: (pt[b, n_pages - 1 - (s * SBS_PAGES + jj)], 0, 0))
    pages = [page(jj) for jj in range(SBS_PAGES)]
    grid_spec = pltpu.PrefetchScalarGridSpec(
        num_scalar_prefetch=1,
        grid=(nb, n_steps),
        in_specs=[pl.BlockSpec((t, NH * HD), lambda b, s, pt: (rb + b, 0)),
                  pl.BlockSpec((t, NH * HD), lambda b, s, pt: (rb + b, 1)),
                  pl.BlockSpec((t, NH * HD), lambda b, s, pt: (rb + b, 2))] + pages + pages,
        out_specs=pl.BlockSpec((t, NH * HD), lambda b, s, pt: (b, 0)),
        scratch_shapes=[pltpu.VMEM((NH * t, HD), f32), pltpu.VMEM((NH * t, HD), f32)],
    )
    return pl.pallas_call(
        functools.partial(_sbs_body, n_steps=n_steps, t=t),
        out_shape=jax.ShapeDtypeStruct((nb * t, NH * HD), bf16),
        grid_spec=grid_spec,
        compiler_params=_cparams(2),
        name="sb_sample",
    )(page_table, z, z, z, *([_pages2d(cache_k)] * SBS_PAGES), *([_pages2d(cache_v)] * SBS_PAGES))


def _t5_bucket(dist):
    n = jnp.maximum(dist, 0)
    exact = N_BUCKETS // 2
    nf = jnp.maximum(n, 1).astype(f32)
    large = exact + (jnp.log(nf / exact) / math.log(MAX_DISTANCE / exact) * (N_BUCKETS - exact)).astype(i32)
    return jnp.where(n < exact, n, jnp.minimum(large, N_BUCKETS - 1))


def _bias_of_bucket(bucket, rel_ref, h):
    out = jnp.zeros(bucket.shape, f32)
    for k in range(N_BUCKETS):
        out = jnp.where(bucket == k, rel_ref[k, h], out)
    return out


def _softmax_step(m, l, acc, logits, mask, v):
    lg = jnp.where(mask, logits, NEG)
    m_new = jnp.maximum(m, jnp.max(lg, axis=-1, keepdims=True))
    alpha = jnp.exp(m - m_new)
    e = jnp.where(mask, jnp.exp(lg - m_new), 0.0)
    return m_new, alpha * l + jnp.sum(e, axis=-1, keepdims=True), alpha * acc + _mm(e, v)


def _softmax_full(logits, mask):
    lg = jnp.where(mask, logits, NEG)
    e = jnp.where(mask, jnp.exp(lg - jnp.max(lg, axis=-1, keepdims=True)), 0.0)
    return e / jnp.maximum(jnp.sum(e, axis=-1, keepdims=True), 1e-30)


def _select_blocks(imp_sel, cur, n_sel):
    rows = imp_sel.shape[0]
    imp_pad = jnp.concatenate([imp_sel, jnp.zeros((rows, HD - imp_sel.shape[1]), f32)], axis=1)
    blk = _iota2((rows, HD), 1)
    forced = jnp.where(blk == cur, 2.0 * FORCE_SCORE, jnp.where(blk == 0, FORCE_SCORE, -FORCE_SCORE))
    score = jnp.where((blk < cur) & (blk > 0), imp_pad, forced)
    score = jnp.where(blk < n_sel, score, -jnp.inf)
    sel = jnp.zeros((rows, HD), f32)
    for _ in range(min(TOP_N, n_sel)):
        mx = jnp.max(score, axis=-1, keepdims=True)
        idx = jnp.min(jnp.where(score == mx, blk, 1 << 30), axis=-1, keepdims=True)
        hit = blk == idx
        sel = jnp.where(hit, 1.0, sel)
        score = jnp.where(hit, -jnp.inf, score)
    return sel


def _compress(get_rows, cw_ref, cb, ckg, n_groups_rows):
    del n_groups_rows
    xk = jnp.concatenate([get_rows(0, t).astype(bf16) for t in range(CMP_BLOCK)], axis=1)
    xv = jnp.concatenate([get_rows(1, t).astype(bf16) for t in range(CMP_BLOCK)], axis=1)
    return _rms(_mm(xk, cw_ref[0]) + cb[0:1], ckg), _mm(xv, cw_ref[1]) + cb[1:2]


def _cmp_order(n_cmp, shape, axis):
    c = _iota2(shape, axis)
    half = n_cmp // 2
    return 2 * (c % half) + c // half


def _nsap_body(q_ref, gt_ref, ck0_ref, ck1_ref, cv0_ref, cv1_ref, sk_ref, sv_ref, wk_ref, wv_ref, cw_ref, cb_ref, ckg_ref,
               qg_ref, rel_ref, o_ref, kc_scr, vc_scr, bias_scr, *, t):
    b = pl.program_id(0)
    i = pl.program_id(1)
    n_cmp = t // CMP_BLOCK
    half = n_cmp // 2
    n_sel = t // SEL_BLOCK
    QB = HD
    cmp_refs = ((ck0_ref, ck1_ref), (cv0_ref, cv1_ref))

    @pl.when(i == 0)
    def _():
        for g in range(NKV):
            def get_rows(kind, tt, g=g):
                ref = cmp_refs[kind][g]
                return jnp.concatenate([ref[pl.ds(tt, half, stride=2 * CMP_BLOCK), :],
                                        ref[pl.ds(CMP_BLOCK + tt, half, stride=2 * CMP_BLOCK), :]], axis=0)
            kc, vc = _compress(get_rows, cw_ref, cb_ref[...], ckg_ref[...], n_cmp)
            kc_scr[g] = kc
            vc_scr[g] = vc

    @pl.when((b == 0) & (i == 0))
    def _():
        r_ = _iota2((QB, QB), 0)
        c_ = _iota2((QB, QB), 1)
        for kk in range(3):
            bucket = _t5_bucket(r_ - c_ + QB * kk)
            for h in range(NH):
                bias_scr[kk, h] = _bias_of_bucket(bucket, rel_ref, h)

    qall = q_ref[...]
    qg = qg_ref[...]
    qs = [jnp.concatenate([_rms(qall[:, (g * NREP + r) * HD:(g * NREP + r + 1) * HD], qg) for r in range(NREP)], axis=0)
          for g in range(NKV)]
    rep = lambda x: jnp.concatenate([x] * NREP, axis=0)

    qpos_c = i * QB + _iota2((QB, n_cmp), 0)
    dist_c = qpos_c - (_cmp_order(n_cmp, (QB, n_cmp), 1) * CMP_BLOCK + CMP_BLOCK - 1)
    bucket_c = _t5_bucket(dist_c)
    mask_c = rep(dist_c >= 0)
    cur = (i * QB + _iota2((QB, 1), 0)) // SEL_BLOCK
    o_cmp, sels = [], []
    for g in range(NKV):
        bias = jnp.concatenate([_bias_of_bucket(bucket_c, rel_ref, g * NREP + r) for r in range(NREP)], axis=0)
        p = _softmax_full(_mm(qs[g], kc_scr[g], NT) * SCALE + bias, mask_c)
        o_cmp.append(_mm(p, vc_scr[g]))
        imp = p[0:QB] + p[QB:2 * QB] + p[2 * QB:3 * QB] + p[3 * QB:4 * QB]
        sels.append(_select_blocks(imp[:, :half] + imp[:, half:], cur, n_sel))

    KP = 2 * QB
    krow = _iota2((QB, KP), 0)
    kcol = _iota2((QB, KP), 1)

    def init():
        return tuple(x for _ in range(NKV) for x in (jnp.full((NREP * QB, 1), NEG, f32), jnp.zeros((NREP * QB, 1), f32),
                                                     jnp.zeros((NREP * QB, HD), f32)))

    def attend(carry, p, k_ref, v_ref, mask_of):
        off = pl.multiple_of(p * KP, KP)
        dist = i * QB + krow - (p * KP + kcol)
        d0 = jnp.clip(i - 2 * p, 0, 2)
        d1 = jnp.clip(i - 2 * p - 1, 0, 2)
        out = []
        for g in range(NKV):
            m, l, acc = carry[3 * g:3 * g + 3]
            bias = jnp.concatenate([jnp.concatenate([bias_scr[d0, g * NREP + r], bias_scr[d1, g * NREP + r]], axis=1)
                                    for r in range(NREP)], axis=0)
            logits = _mm(qs[g], k_ref[pl.ds(off, KP), g * HD:(g + 1) * HD], NT) * SCALE + bias
            out.extend(_softmax_step(m, l, acc, logits, rep(mask_of(g, p, dist)), v_ref[pl.ds(off, KP), g * HD:(g + 1) * HD]))
        return tuple(out)

    def sel_mask(g, p, dist):
        er = _iota2((HD, KP), 0)
        ec = _iota2((HD, KP), 1)
        expand = jnp.where(er == (KP // SEL_BLOCK) * p + ec // SEL_BLOCK, 1.0, 0.0)
        return (_mm(sels[g], expand) > 0.5) & (dist >= 0)

    def win_mask(g, p, dist):
        return (dist >= 0) & (dist < WINDOW)

    last_pair = i // 2
    first_win = jnp.maximum(i - WINDOW // QB, 0) // 2
    c_sel = lax.fori_loop(0, last_pair + 1, lambda p, c: attend(c, p, sk_ref, sv_ref, sel_mask), init())
    c_win = lax.fori_loop(first_win, last_pair + 1, lambda p, c: attend(c, p, wk_ref, wv_ref, win_mask), init())

    gts = jax.nn.sigmoid(gt_ref[...])
    for h in range(NH):
        g, r = divmod(h, NREP)
        sl = slice(r * QB, (r + 1) * QB)
        o_s = c_sel[3 * g + 2][sl] / jnp.maximum(c_sel[3 * g + 1][sl], 1e-30)
        o_w = c_win[3 * g + 2][sl] / jnp.maximum(c_win[3 * g + 1][sl], 1e-30)
        o = gts[:, 3 * h:3 * h + 1] * o_cmp[g][sl] + gts[:, 3 * h + 1:3 * h + 2] * o_s + gts[:, 3 * h + 2:3 * h + 3] * o_w
        o_ref[:, h * HD:(h + 1) * HD] = o.astype(bf16)


def _nsa_prompt(z, sk, wk, nb, t, cmp_w, cmp_b, cmp_k_g, q_g, rel_bias):
    nq = t // HD
    n_cmp = t // CMP_BLOCK
    w = NKV * HD
    full = lambda j: pl.BlockSpec((t, w), lambda b, i: (b, j))
    head = lambda j: pl.BlockSpec((t, HD), lambda b, i: (b, j))
    return pl.pallas_call(
        functools.partial(_nsap_body, t=t),
        out_shape=jax.ShapeDtypeStruct((nb * t, NH * HD), bf16),
        grid=(nb, nq),
        in_specs=[pl.BlockSpec((HD, NH * HD), lambda b, i: (b * nq + i, 3)),
                  pl.BlockSpec((HD, HD), lambda b, i: (b * nq + i, 44)),
                  head(32), head(33), head(34), head(35), full(0), full(19), full(0), full(21),
                  pl.BlockSpec((2, CMP_BLOCK * HD, HD), lambda b, i: (0, 0, 0)),
                  pl.BlockSpec((2, HD), lambda b, i: (0, 0)),
                  pl.BlockSpec((1, HD), lambda b, i: (0, 0)),
                  pl.BlockSpec((1, HD), lambda b, i: (0, 0)),
                  pl.BlockSpec(memory_space=pltpu.SMEM)],
        out_specs=pl.BlockSpec((HD, NH * HD), lambda b, i: (b * nq + i, 0)),
        scratch_shapes=[pltpu.VMEM((NKV, n_cmp, HD), f32), pltpu.VMEM((NKV, n_cmp, HD), f32),
                        pltpu.VMEM((3, NH, HD, HD), f32)],
        compiler_params=_cparams(2),
        name="nsa_prompt",
    )(z, z, z, z, z, z, sk, z, wk, z, cmp_w.astype(bf16), cmp_b, cmp_k_g.reshape(1, HD), q_g.reshape(1, HD), rel_bias)


def _nsa_q_groups(q_ref, qg):
    qall = q_ref[...]
    return [jnp.concatenate([_rms(qall[:, (g * NREP + r) * HD:(g * NREP + r + 1) * HD], qg) for r in range(NREP)], axis=0)
            for g in range(NKV)]


CMP_PAGES = 4


def _nsasa_body(pt_ref, q_ref, *rest, n_pages, t):
    del pt_ref
    ckp_refs, cvp_refs = rest[:CMP_PAGES], rest[CMP_PAGES:2 * CMP_PAGES]
    cw_ref, cb_ref, ckg_ref, qg_ref, rel_ref, oc_ref, sel_ref, ck_scr, cv_scr = rest[2 * CMP_PAGES:]
    p = pl.program_id(1)
    for jj in range(CMP_PAGES):
        off = pl.multiple_of((p * CMP_PAGES + jj) * PAGE, PAGE)
        for g in range(NKV):
            ck_scr[g, pl.ds(off, PAGE), :] = _head_rows(ckp_refs[jj], g, NKV)
            cv_scr[g, pl.ds(off, PAGE), :] = _head_rows(cvp_refs[jj], g, NKV)

    @pl.when(p == n_pages // CMP_PAGES - 1)
    def _():
        past = n_pages * PAGE
        n_cmp = (past + t) // CMP_BLOCK
        half = n_cmp // 2
        n_sel = -(-(past + t) // SEL_BLOCK)

        def get_rows(kind, tt):
            scr = ck_scr if kind == 0 else cv_scr
            return jnp.concatenate([scr[g, pl.ds(par * CMP_BLOCK + tt, half, stride=2 * CMP_BLOCK), :]
                                    for g in range(NKV) for par in range(2)], axis=0)

        kc, vc = _compress(get_rows, cw_ref, cb_ref[...], ckg_ref[...], NKV * n_cmp)
        qs = _nsa_q_groups(q_ref, qg_ref[...])
        rows = NREP * t
        tq = _iota2((rows, n_cmp), 0) % t
        dist_c = past + tq - (_cmp_order(n_cmp, (rows, n_cmp), 1) * CMP_BLOCK + CMP_BLOCK - 1)
        bucket_c = _t5_bucket(dist_c)
        mask_c = dist_c >= 0
        cur = (past + _iota2((t, 1), 0)) // SEL_BLOCK
        for g in range(NKV):
            bias = jnp.concatenate([_bias_of_bucket(bucket_c[r * t:(r + 1) * t], rel_ref, g * NREP + r)
                                    for r in range(NREP)], axis=0)
            pr = _softmax_full(_mm(qs[g], kc[g * n_cmp:(g + 1) * n_cmp], NT) * SCALE + bias, mask_c)
            oc_ref[g * rows:(g + 1) * rows, :] = _mm(pr, vc[g * n_cmp:(g + 1) * n_cmp])
            imp = pr[0:t] + pr[t:2 * t] + pr[2 * t:3 * t] + pr[3 * t:4 * t]
            sel_ref[g * t:(g + 1) * t, :] = _select_blocks(imp[:, :half] + imp[:, half:], cur, n_sel)


def _nsa_sample_cmp(z, row_off, nb, t, cache_ck, cache_cv, page_table, cmp_w, cmp_b, cmp_k_g, q_g, rel_bias):
    n_pages = page_table.shape[1]
    assert n_pages % CMP_PAGES == 0
    rb = row_off // t
    page = lambda jj: pl.BlockSpec((None, PAGE * NKV, HD), lambda b, p, pt: (pt[b, p * CMP_PAGES + jj], 0, 0))
    pages = [page(jj) for jj in range(CMP_PAGES)]
    cst = lambda *shape: pl.BlockSpec(shape, lambda b, p, pt: (0,) * len(shape))
    grid_spec = pltpu.PrefetchScalarGridSpec(
        num_scalar_prefetch=1,
        grid=(nb, n_pages // CMP_PAGES),
        in_specs=[pl.BlockSpec((t, NH * HD), lambda b, p, pt: (rb + b, 3))] + pages + pages + [
                  cst(2, CMP_BLOCK * HD, HD), cst(2, HD), cst(1, HD), cst(1, HD),
                  pl.BlockSpec(memory_space=pltpu.SMEM)],
        out_specs=(pl.BlockSpec((None, NH * t, HD), lambda b, p, pt: (b, 0, 0)),
                   pl.BlockSpec((None, NKV * t, HD), lambda b, p, pt: (b, 0, 0))),
        scratch_shapes=[pltpu.VMEM((NKV, n_pages * PAGE, HD), f32), pltpu.VMEM((NKV, n_pages * PAGE, HD), f32)],
    )
    return pl.pallas_call(
        functools.partial(_nsasa_body, n_pages=n_pages, t=t),
        out_shape=(jax.ShapeDtypeStruct((nb, NH * t, HD), f32), jax.ShapeDtypeStruct((nb, NKV * t, HD), f32)),
        grid_spec=grid_spec,
        compiler_params=_cparams(2),
        name="nsa_sample_cmp",
    )(page_table, z, *([_pages2d(cache_ck)] * CMP_PAGES), *([_pages2d(cache_cv)] * CMP_PAGES), cmp_w.astype(bf16), cmp_b,
      cmp_k_g.reshape(1, HD), q_g.reshape(1, HD), rel_bias)


NSA_PAGES = 5


def _nsasb_body(pt_ref, q_ref, gt_ref, oc_ref, sel_ref, skn_ref, svn_ref, wkn_ref, wvn_ref, wk_ref, wv_ref,
                skl_ref, svl_ref, *rest, n_pages, t):
    del pt_ref
    skp_refs, svp_refs = rest[:NSA_PAGES], rest[NSA_PAGES:2 * NSA_PAGES]
    qg_ref, rel_ref, o_ref, m_scr, l_scr, acc_scr, ow_scr = rest[2 * NSA_PAGES:]
    s = pl.program_id(1)
    n_steps = (n_pages - 1) // NSA_PAGES
    past = n_pages * PAGE
    rows = NREP * t
    qs = _nsa_q_groups(q_ref, qg_ref[...])
    tq = _iota2((t, HD), 0)
    col = _iota2((t, HD), 1)
    krow = _iota2((HD, HD), 0)
    kcol = _iota2((HD, HD), 1)
    rep = lambda x: jnp.concatenate([x] * NREP, axis=0)

    def bias_of(dist, g):
        bucket = _t5_bucket(dist)
        return jnp.concatenate([_bias_of_bucket(bucket, rel_ref, g * NREP + r) for r in range(NREP)], axis=0)

    def far_bias(g):
        return jnp.concatenate([jnp.full((t, 1), rel_ref[N_BUCKETS - 1, g * NREP + r], f32) for r in range(NREP)], axis=0)

    def attend(state, g, ks, vs, bias, mask):
        m, l, acc = state
        logits = jnp.concatenate([_mm(qs[g], k, NT) for k in ks], axis=1) * SCALE + bias
        mask = rep(mask)
        lg = jnp.where(mask, logits, NEG)
        m_new = jnp.maximum(m, jnp.max(lg, axis=-1, keepdims=True))
        alpha = jnp.exp(m - m_new)
        e = jnp.where(mask, jnp.exp(lg - m_new), 0.0)
        acc = alpha * acc
        for n, v in enumerate(vs):
            acc = acc + _mm(e[:, n * HD:(n + 1) * HD], v)
        return m_new, alpha * l + jnp.sum(e, axis=-1, keepdims=True), acc

    def sel_update(g, ks, vs, bias, mask):
        sl = slice(g * rows, (g + 1) * rows)
        m, l, acc = attend((m_scr[sl, 0:1], l_scr[sl, 0:1], acc_scr[sl, :]), g, ks, vs, bias, mask)
        m_scr[sl, :] = jnp.broadcast_to(m, (rows, HD))
        l_scr[sl, :] = jnp.broadcast_to(l, (rows, HD))
        acc_scr[sl, :] = acc

    def picked(g, first_block, n_blocks):
        er = _iota2((HD, n_blocks * HD), 0)
        ec = _iota2((HD, n_blocks * HD), 1)
        expand = jnp.where(er == first_block + ec // SEL_BLOCK, 1.0, 0.0)
        return _mm(sel_ref[g * t:(g + 1) * t, :], expand) > 0.5

    @pl.when(s == 0)
    def _():
        m_scr[...] = jnp.full(m_scr.shape, NEG, f32)
        l_scr[...] = jnp.zeros_like(l_scr)
        acc_scr[...] = jnp.zeros_like(acc_scr)
        nwin = WINDOW // HD
        tqw = _iota2((t, WINDOW + HD), 0)
        colw = _iota2((t, WINDOW + HD), 1)
        dist_w = WINDOW + tqw - colw
        ok_w = (dist_w >= 0) & (dist_w < WINDOW) & (colw < WINDOW + t)
        tqs = _iota2((t, 2 * HD), 0)
        cols = _iota2((t, 2 * HD), 1)
        dist_s = PAGE + tqs - cols
        ok_s = (dist_s >= 0) & (cols < PAGE + t)
        for g in range(NKV):
            gs = slice(g * HD, (g + 1) * HD)
            ks = [wk_ref[pl.ds(jj * HD * NKV + g, HD, stride=NKV), :] for jj in range(nwin)] + [_pad_rows(wkn_ref[:, gs], HD)]
            vs = [wv_ref[pl.ds(jj * HD * NKV + g, HD, stride=NKV), :] for jj in range(nwin)] + [_pad_rows(wvn_ref[:, gs], HD)]
            st = (jnp.full((rows, 1), NEG, f32), jnp.zeros((rows, 1), f32), jnp.zeros((rows, HD), f32))
            st = attend(st, g, ks, vs, bias_of(dist_w, g), ok_w)
            ow_scr[g * rows:(g + 1) * rows, :] = st[2] / jnp.maximum(st[1], 1e-30)
            sel_update(g, [_head_rows(skl_ref, g, NKV), _pad_rows(skn_ref[:, gs], HD)],
                       [_head_rows(svl_ref, g, NKV), _pad_rows(svn_ref[:, gs], HD)], bias_of(dist_s, g),
                       picked(g, (past - PAGE) // SEL_BLOCK, 2) & ok_s)

    for g in range(NKV):
        sel_update(g, [_head_rows(skp_refs[jj], g, NKV) for jj in range(NSA_PAGES)],
                   [_head_rows(svp_refs[jj], g, NKV) for jj in range(NSA_PAGES)], far_bias(g),
                   picked(g, (PAGE // SEL_BLOCK) * NSA_PAGES * s, NSA_PAGES))

    @pl.when(s == n_steps - 1)
    def _():
        gts = jax.nn.sigmoid(gt_ref[...])
        for h in range(NH):
            g, r = divmod(h, NREP)
            sl = slice(g * rows + r * t, g * rows + (r + 1) * t)
            o_s = acc_scr[sl, :] / jnp.maximum(l_scr[sl, 0:1], 1e-30)
            o = (gts[:, 3 * h:3 * h + 1] * oc_ref[sl, :] + gts[:, 3 * h + 1:3 * h + 2] * o_s
                 + gts[:, 3 * h + 2:3 * h + 3] * ow_scr[sl, :])
            o_ref[:, h * HD:(h + 1) * HD] = o.astype(bf16)


def _nsa_sample_attn(z, sk, wk, row_off, nb, t, o_cmp, sel, win_k, win_v, cache_sk, cache_sv, page_table, q_g, rel_bias):
    n_pages = page_table.shape[1]
    assert (n_pages - 1) % NSA_PAGES == 0 and PAGE >= MAX_DISTANCE
    rb = row_off // t
    w = NKV * HD
    page = lambda jj: pl.BlockSpec((None, PAGE * NKV, HD), lambda b, s, pt: (pt[b, s * NSA_PAGES + jj], 0, 0))
    pages = [page(jj) for jj in range(NSA_PAGES)]
    last = pl.BlockSpec((None, PAGE * NKV, HD), lambda b, s, pt: (pt[b, n_pages - 1], 0, 0))
    rowsp = lambda width, j: pl.BlockSpec((t, width), lambda b, s, pt: (rb + b, j))
    grid_spec = pltpu.PrefetchScalarGridSpec(
        num_scalar_prefetch=1,
        grid=(nb, (n_pages - 1) // NSA_PAGES),
        in_specs=[rowsp(NH * HD, 3), rowsp(HD, 44),
                  pl.BlockSpec((None, NH * t, HD), lambda b, s, pt: (b, 0, 0)),
                  pl.BlockSpec((None, NKV * t, HD), lambda b, s, pt: (b, 0, 0)),
                  rowsp(w, 0), rowsp(w, 19), rowsp(w, 0), rowsp(w, 21),
                  pl.BlockSpec((None, WINDOW * NKV, HD), lambda b, s, pt: (b, 0, 0)),
                  pl.BlockSpec((None, WINDOW * NKV, HD), lambda b, s, pt: (b, 0, 0)),
                  last, last] + pages + pages + [
                  pl.BlockSpec((1, HD), lambda b, s, pt: (0, 0)),
                  pl.BlockSpec(memory_space=pltpu.SMEM)],
        out_specs=pl.BlockSpec((t, NH * HD), lambda b, s, pt: (b, 0)),
        scratch_shapes=[pltpu.VMEM((NH * t, HD), f32)] * 4,
    )
    return pl.pallas_call(
        functools.partial(_nsasb_body, n_pages=n_pages, t=t),
        out_shape=jax.ShapeDtypeStruct((nb * t, NH * HD), bf16),
        grid_spec=grid_spec,
        compiler_params=_cparams(2),
        name="nsa_sample_attn",
    )(page_table, z, z, o_cmp, sel, sk, z, wk, z, _pages2d(win_k), _pages2d(win_v), _pages2d(cache_sk), _pages2d(cache_sv),
      *([_pages2d(cache_sk)] * NSA_PAGES), *([_pages2d(cache_sv)] * NSA_PAGES), q_g.reshape(1, HD), rel_bias)


def _even_mixer(z, mp, bp, tp, bs, ts, conv_w, a_log, dt_bias, gdn_g, gate_bias, mlstm_g, s0, conv0, c0, n0, m0):
    w3 = 3 * NH * HD
    go_p, gs_p = _gdn(z, 0, bp, tp, conv_w, a_log, dt_bias, gdn_g)
    prev8 = jnp.concatenate([jnp.zeros((bs, 8 - conv0.shape[1], w3), f32), conv0], axis=1)
    go_s, gs_s = _gdn_decode(z, mp, bs, ts, conv_w, a_log, dt_bias, gdn_g, s0, prev8)
    mh_p, mc_p, mn_p, mm_p = _mlstm(z, 0, bp, tp, gate_bias, mlstm_g)
    mh_s, mc_s, mn_s, mm_s = _mlstm_decode(z, mp, bs, ts, gate_bias, mlstm_g, c0, n0, m0)
    mix = jnp.concatenate([jnp.concatenate([go_p, mh_p], axis=1), jnp.concatenate([go_s, mh_s], axis=1)], axis=0)
    keep = conv0.shape[1]
    conv_p = z[:mp, :w3].reshape(bp, tp, w3)[:, tp - keep:]
    conv_s = z[mp:, :w3].reshape(bs, ts, w3)[:, ts - keep:]
    states = (gs_p, gs_s, conv_p, conv_s, mc_p, mc_s, mn_p, mn_s, mm_p[..., 0], mm_s[..., 0])
    return mix, states


def _odd_mixer(z, mp, bp, tp, bs, ts, page_table, caches, win_k0, win_v0, q_g, k_g, cmp_w, cmp_b, rel_bias):
    sb_k, sb_v, cmp_k, cmp_v, sel_k, sel_v = caches
    sk, wk = _kvnorm(z, k_g)
    sb_p = _sb_prompt(z, bp, tp)
    sb_s = _sb_sample(z, mp, bs, ts, sb_k, sb_v, page_table)
    ns_p = _nsa_prompt(z, sk, wk, bp, tp, cmp_w, cmp_b, k_g[0], q_g, rel_bias)
    o_cmp, sel = _nsa_sample_cmp(z, mp, bs, ts, cmp_k, cmp_v, page_table, cmp_w, cmp_b, k_g[0], q_g, rel_bias)
    ns_s = _nsa_sample_attn(z, sk, wk, mp, bs, ts, o_cmp, sel, win_k0, win_v0, sel_k, sel_v, page_table, q_g, rel_bias)
    mix = jnp.concatenate([jnp.concatenate([sb_p, ns_p], axis=1), jnp.concatenate([sb_s, ns_s], axis=1)], axis=0)

    w = NH * HD
    kw = NKV * HD

    def rows(arr, lo, width, heads):
        sl = arr[:, lo:lo + width]
        return sl[:mp].reshape(bp, tp, heads, HD), sl[mp:].reshape(bs, ts, heads, HD)

    sbk_p, sbk_s = rows(z, w, w, NH)
    sbv_p, sbv_s = rows(z, 2 * w, w, NH)
    ck_p, ck_s = rows(z, 4 * w, kw, NKV)
    cv_p, cv_s = rows(z, 4 * w + kw, kw, NKV)
    sk_p, sk_s = rows(sk, 0, kw, NKV)
    sv_p, sv_s = rows(z, 4 * w + 3 * kw, kw, NKV)
    wk_p, wk_s = rows(wk, 0, kw, NKV)
    wv_p, wv_s = rows(z, 4 * w + 5 * kw, kw, NKV)
    keep_p = min(WINDOW, tp)
    win = lambda old, new: jnp.concatenate([old, new], axis=1)[:, -min(WINDOW, old.shape[1] + ts):]
    states = (sbk_p, sbk_s, sbv_p, sbv_s, ck_p, ck_s, cv_p, cv_s, sk_p, sk_s, sv_p, sv_s,
              wk_p[:, tp - keep_p:], win(win_k0, wk_s), wv_p[:, tp - keep_p:], win(win_v0, wv_s))
    return mix, states


def kernel(x_prompt, x_sample, state_gdn_s, state_gdn_conv, state_mlstm_c, state_mlstm_n, state_mlstm_m, cache_sb_k, cache_sb_v, cache_nsa_cmp_k, cache_nsa_cmp_v, cache_nsa_sel_k, cache_nsa_sel_v, state_nsa_win_k, state_nsa_win_v, page_table, norm_g, ffn_w_gate, ffn_w_up, ffn_w_down, even_w_in, gdn_conv_w, gdn_a_log, gdn_dt_bias, gdn_norm_g, mlstm_gate_bias, mlstm_norm_g, even_w_out, odd_w_in, nsa_q_norm_g, nsa_k_norm_g, nsa_cmp_w, nsa_cmp_b, odd_w_out, rel_bias):
    bp, tp, d = x_prompt.shape
    bs, ts, _ = x_sample.shape
    mp = bp * tp
    xs = jnp.concatenate([x_prompt.reshape(mp, d), x_sample.reshape(bs * ts, d)], axis=0)
    depth = norm_g.shape[0]
    even_states, odd_states = [], []
    for layer in range(depth):
        j = layer // 2
        ffn = lambda x, n, i: _ffn(x, norm_g[layer, n], ffn_w_gate[layer, i], ffn_w_up[layer, i], ffn_w_down[layer, i])
        xs = ffn(xs, 0, 0)
        if layer % 2 == 0:
            w_in = even_w_in[j][:, _even_perm()]
            w_in = _pad_cols(w_in.astype(bf16), PROJ_TN)
            z = _rms_matmul(xs, norm_g[layer, 1], w_in)
            mix, st = _even_mixer(z, mp, bp, tp, bs, ts, gdn_conv_w[j], gdn_a_log[j], gdn_dt_bias[j], gdn_norm_g[j],
                                  mlstm_gate_bias[j], mlstm_norm_g[j], state_gdn_s[j], state_gdn_conv[j],
                                  state_mlstm_c[j], state_mlstm_n[j], state_mlstm_m[j])
            even_states.append(st)
            w_out = even_w_out[j]
        else:
            w_in = _pad_cols(odd_w_in[j].astype(bf16), PROJ_TN)
            z = _rms_matmul(xs, norm_g[layer, 1], w_in)
            caches = (cache_sb_k[j], cache_sb_v[j], cache_nsa_cmp_k[j], cache_nsa_cmp_v[j], cache_nsa_sel_k[j],
                      cache_nsa_sel_v[j])
            mix, st = _odd_mixer(z, mp, bp, tp, bs, ts, page_table, caches, state_nsa_win_k[j], state_nsa_win_v[j],
                                 nsa_q_norm_g[j], nsa_k_norm_g[j], nsa_cmp_w[j], nsa_cmp_b[j], rel_bias)
            odd_states.append(st)
            w_out = odd_w_out[j]
        xs = _out_proj(mix, w_out.astype(bf16), xs)
        xs = ffn(xs, 2, 1)
    stack = lambda sts, i: jnp.stack([s[i] for s in sts])
    outs = [xs[:mp].reshape(bp, tp, d), xs[mp:].reshape(bs, ts, d)]
    outs += [stack(even_states, i) for i in range(10)]
    outs += [stack(odd_states, i) for i in range(16)]
    return tuple(outs)
```

```python
import functools
import math

import jax
import jax.numpy as jnp
import numpy as np
from jax import lax
from jax.experimental import pallas as pl
from jax.experimental.pallas import tpu as pltpu

f32 = jnp.float32
bf16 = jnp.bfloat16
i32 = jnp.int32

HD = 128
NH = 8
NKV = 2
NREP = 4
RMS_EPS = 1e-6
MLSTM_EPS = 1e-6
CHUNK = 128
PAGE = 128
CMP_BLOCK = 32
SEL_BLOCK = 64
TOP_N = 8
WINDOW = 512
FORCE_SCORE = 1.0e4
N_BUCKETS = 32
MAX_DISTANCE = 128
NEG = -1e30
VMEM_LIMIT = 56 * 1024 * 1024

NN = (((1,), (0,)), ((), ()))
NT = (((1,), (1,)), ((), ()))
TN = (((0,), (0,)), ((), ()))


FFN_VMEM_LIMIT = 60 * 1024 * 1024


def _cparams(n_axes, vmem_limit=None):
    return pltpu.CompilerParams(dimension_semantics=("arbitrary",) * n_axes,
                                vmem_limit_bytes=VMEM_LIMIT if vmem_limit is None else vmem_limit)


def _mm(a, b, dims=NN):
    return lax.dot_general(a.astype(bf16), b.astype(bf16), dims, preferred_element_type=f32)


def _split3(a):
    a0 = a.astype(bf16)
    r = a - a0.astype(f32)
    a1 = r.astype(bf16)
    a2 = (r - a1.astype(f32)).astype(bf16)
    return a0, a1, a2


def _mm_exact_b(a, b01, dims=NN):
    a0, a1, a2 = _split3(a)
    b01 = b01.astype(bf16)
    d = lambda x: lax.dot_general(x, b01, dims, preferred_element_type=f32)
    return d(a0) + d(a1) + d(a2)


def _mm_exact_a(a01, b, dims=NN):
    b0, b1, b2 = _split3(b)
    a01 = a01.astype(bf16)
    d = lambda x: lax.dot_general(a01, x, dims, preferred_element_type=f32)
    return d(b0) + d(b1) + d(b2)


def _split2(a):
    ah = a.astype(bf16)
    return ah, (a - ah.astype(f32)).astype(bf16)


def _mm_split(a, bs):
    n = a.shape[0]
    rhs = jnp.concatenate([x for b in bs for x in _split2(b)], axis=1)
    r = jnp.dot(jnp.concatenate(_split2(a), axis=0), rhs, preferred_element_type=f32)
    outs, off = [], 0
    for b in bs:
        w = b.shape[1]
        outs.append((r[:n, off:off + w] + r[:n, off + w:off + 2 * w]) + (r[n:, off:off + w] + r[n:, off + w:off + 2 * w]))
        off += 2 * w
    return outs


def _tri_inverse(a_mats, eye, levels):
    ys = [eye - a for a in a_mats]
    if levels < 2:
        return ys
    qs = [_mm_split(a, [a])[0] for a in a_mats]
    for lvl in range(levels - 1):
        last = lvl == levels - 2
        res = [_mm_split(q, [y] if last else [y, q]) for q, y in zip(qs, ys)]
        ys = [y + r[0] for y, r in zip(ys, res)]
        if not last:
            qs = [r[1] for r in res]
    return ys


def _softplus(x):
    return jnp.maximum(x, 0.0) + jnp.log1p(jnp.exp(-jnp.abs(x)))


def _log_sigmoid(x):
    return -_softplus(-x)


def _rms(x, g):
    return x * lax.rsqrt(jnp.mean(x * x, axis=-1, keepdims=True) + RMS_EPS) * g


def _pad_rows(x, rows, value=0.0):
    if x.shape[0] == rows:
        return x
    return jnp.concatenate([x, jnp.full((rows - x.shape[0],) + x.shape[1:], value, x.dtype)], axis=0)


def _iota2(shape, axis):
    return lax.broadcasted_iota(i32, shape, axis)


def _ffn_body(x_ref, g_ref, wg_ref, wu_ref, wd_ref, o_ref, h_ref, *, nf):
    f = pl.program_id(1)

    del nf

    @pl.when(f == 0)
    def _():
        x = x_ref[...]
        h_ref[...] = _rms(x, g_ref[...]).astype(bf16)
        o_ref[...] = x

    h = h_ref[...]
    a = jnp.dot(h, wg_ref[...].astype(bf16), preferred_element_type=f32)
    u = jnp.dot(h, wu_ref[...].astype(bf16), preferred_element_type=f32)
    act = (0.5 * a * jax.nn.sigmoid(a) * u).astype(bf16)
    o_ref[...] += jnp.dot(act, wd_ref[...].astype(bf16), preferred_element_type=f32)


def _ffn(x, g, wg, wu, wd, layer, idx, *, tm=1024, tf=512):
    m, d = x.shape
    fdim = wg.shape[-1]
    nf = fdim // tf
    once = pl.Buffered(1)
    return pl.pallas_call(
        functools.partial(_ffn_body, nf=nf),
        out_shape=jax.ShapeDtypeStruct((m, d), f32),
        grid=(m // tm, nf),
        in_specs=[
            pl.BlockSpec((tm, d), lambda i, f: (i, 0), pipeline_mode=once),
            pl.BlockSpec((1, d), lambda i, f: (0, 0)),
            pl.BlockSpec((None, None, d, tf), lambda i, f: (layer, idx, 0, f)),
            pl.BlockSpec((None, None, d, tf), lambda i, f: (layer, idx, 0, f)),
            pl.BlockSpec((None, None, tf, d), lambda i, f: (layer, idx, f, 0)),
        ],
        out_specs=pl.BlockSpec((tm, d), lambda i, f: (i, 0), pipeline_mode=once),
        scratch_shapes=[pltpu.VMEM((tm, d), bf16)],
        compiler_params=_cparams(2, FFN_VMEM_LIMIT),
        name="ffn",
    )(x, g.reshape(1, d), wg, wu, wd)


def _rmsmm_body(x_ref, g_ref, w_ref, o_ref, h_ref):
    @pl.when(pl.program_id(1) == 0)
    def _():
        h_ref[...] = _rms(x_ref[...], g_ref[...]).astype(bf16)

    o_ref[...] = jnp.dot(h_ref[...], w_ref[...], preferred_element_type=f32)


PROJ_TN = 768


def _pad_cols(w, mult):
    pad = -w.shape[1] % mult
    return w if pad == 0 else jnp.concatenate([w, jnp.zeros((w.shape[0], pad), w.dtype)], axis=1)


def _rms_matmul(x, g, w, *, tm=1024, tn=PROJ_TN):
    m, d = x.shape
    n = w.shape[1]
    return pl.pallas_call(
        _rmsmm_body,
        out_shape=jax.ShapeDtypeStruct((m, n), f32),
        grid=(m // tm, n // tn),
        in_specs=[
            pl.BlockSpec((tm, d), lambda i, j: (i, 0), pipeline_mode=pl.Buffered(1)),
            pl.BlockSpec((1, d), lambda i, j: (0, 0)),
            pl.BlockSpec((d, tn), lambda i, j: (0, j)),
        ],
        out_specs=pl.BlockSpec((tm, tn), lambda i, j: (i, j)),
        scratch_shapes=[pltpu.VMEM((tm, d), bf16)],
        compiler_params=_cparams(2),
        name="rms_matmul",
    )(x, g.reshape(1, d), w)


def _outproj_body(ap_ref, bp_ref, as_ref, bs_ref, w_ref, r_ref, o_ref, *, n_prompt_tiles):
    i = pl.program_id(0)
    half = ap_ref.shape[1]

    def run(a_ref, b_ref):
        o_ref[...] = (r_ref[...] + jnp.dot(a_ref[...], w_ref[:half, :], preferred_element_type=f32)
                      + jnp.dot(b_ref[...], w_ref[half:, :], preferred_element_type=f32))

    @pl.when(i < n_prompt_tiles)
    def _():
        run(ap_ref, bp_ref)

    @pl.when(i >= n_prompt_tiles)
    def _():
        run(as_ref, bs_ref)


def _out_proj(prompt_parts, sample_parts, w, res, *, tm=512, tn=512):
    (a_p, b_p), (a_s, b_s) = prompt_parts, sample_parts
    m, n = res.shape
    half = a_p.shape[1]
    npt = a_p.shape[0] // tm
    assert a_p.shape[0] % tm == 0 and a_s.shape[0] % tm == 0 and m == a_p.shape[0] + a_s.shape[0]
    pspec = pl.BlockSpec((tm, half), lambda i, j: (jnp.minimum(i, npt - 1), 0))
    sspec = pl.BlockSpec((tm, half), lambda i, j: (jnp.maximum(i - npt, 0), 0))
    return pl.pallas_call(
        functools.partial(_outproj_body, n_prompt_tiles=npt),
        out_shape=jax.ShapeDtypeStruct((m, n), f32),
        grid=(m // tm, n // tn),
        in_specs=[pspec, pspec, sspec, sspec,
                  pl.BlockSpec((2 * half, tn), lambda i, j: (0, j)),
                  pl.BlockSpec((tm, tn), lambda i, j: (i, j))],
        out_specs=pl.BlockSpec((tm, tn), lambda i, j: (i, j)),
        compiler_params=_cparams(2),
        name="out_proj",
    )(a_p, b_p, a_s, b_s, w, res)


def _tri_masks(c):
    row = _iota2((c, c), 0)
    col = _iota2((c, c), 1)
    return row, col


def _gdn_body(*refs, c_in, n_chunks, levels, has_state, first_chunk_zero_prev):
    if has_state:
        (u_ref, p8_ref, gate_ref, sm_ref, cw_ref, alog_ref, dtb_ref, ng_ref, s0_ref, o_ref, so_ref, s_scr) = refs
    else:
        (u_ref, p8_ref, gate_ref, sm_ref, cw_ref, alog_ref, dtb_ref, ng_ref, o_ref, so_ref, s_scr) = refs
    c = pl.program_id(1)
    C = CHUNK

    @pl.when(c == 0)
    def _():
        if has_state:
            s_scr[...] = s0_ref[...]
        else:
            s_scr[...] = jnp.zeros_like(s_scr)

    u = u_ref[...]
    p8 = p8_ref[...]
    if first_chunk_zero_prev:
        p8 = jnp.where(c == 0, 0.0, p8)
    cw = cw_ref[...]
    row8 = _iota2((8, u.shape[1]), 0)
    acc = u * cw[3:4]
    for k in range(1, 4):
        rolled = pltpu.roll(u, k, axis=0)
        first8 = jnp.where(row8 < k, pltpu.roll(p8, k, axis=0), rolled[0:8])
        sh = first8 if c_in == 8 else jnp.concatenate([first8, rolled[8:]], axis=0)
        acc = acc + sh * cw[3 - k:4 - k]
    qkv = acc * jax.nn.sigmoid(acc)

    sm = sm_ref[...]
    gfull = _pad_rows(-jnp.exp(alog_ref[...]) * _softplus(sm + dtb_ref[...]), C)
    beta = _pad_rows(jax.nn.sigmoid(sm), C)
    row, col = _tri_masks(C)
    causal = col <= row
    strict = col < row
    eye = jnp.where(row == col, 1.0, 0.0).astype(f32)
    gc = _mm_exact_a(jnp.where(causal, 1.0, 0.0), gfull)
    gate = gate_ref[...]
    ng = ng_ref[...]

    heads = range(NH)
    qn, kn, vb, kbe, decay, gch, a_mats, qk = [], [], [], [], [], [], [], []
    for h in heads:
        q = _pad_rows(qkv[:, h * HD:(h + 1) * HD], C)
        k = _pad_rows(qkv[:, (NH + h) * HD:(NH + h + 1) * HD], C)
        v = _pad_rows(qkv[:, (2 * NH + h) * HD:(2 * NH + h + 1) * HD], C)
        qn.append(q * lax.rsqrt(jnp.sum(q * q, axis=-1, keepdims=True) + 1e-6) * (HD ** -0.5))
        kn.append(k * lax.rsqrt(jnp.sum(k * k, axis=-1, keepdims=True) + 1e-6))
        gch.append(gc[:, h:h + 1])
        bh = beta[:, NH + h:NH + h + 1]
        gcb = jnp.broadcast_to(gch[h], (C, C))
        decay.append(jnp.where(causal, jnp.exp(jnp.where(causal, gcb - jnp.transpose(gcb), 0.0)), 0.0))
        kb = kn[h] * bh
        vb.append(v * bh)
        kbe.append(kb * jnp.exp(gch[h]))
        kq = _mm(jnp.concatenate([kb, qn[h]], axis=0), kn[h], NT)
        a_mats.append(jnp.where(strict, kq[:C] * decay[h], 0.0))
        qk.append(kq[C:] * decay[h])
    x_inv = _tri_inverse(a_mats, eye, levels)
    sol = [_mm_split(x_inv[h], [jnp.concatenate([vb[h], kbe[h]], axis=1)])[0] for h in heads]
    from_s = [_mm(jnp.concatenate([sol[h][:, HD:], qn[h] * jnp.exp(gch[h])], axis=0), s_scr[h]) for h in heads]
    uu = [sol[h][:, :HD] - from_s[h][:C] for h in heads]
    g_end = [gch[h][C - 1:C, :] for h in heads]
    from_u = [_mm(jnp.concatenate([qk[h], jnp.transpose(kn[h] * jnp.exp(g_end[h] - gch[h]))], axis=0), uu[h])
              for h in heads]
    for h in heads:
        s_scr[h] = s_scr[h] * jnp.exp(g_end[h]) + from_u[h][C:]
        gt = gate[:, h * HD:(h + 1) * HD]
        y = _rms((from_s[h][C:] + from_u[h][:C])[:c_in], ng) * (gt * jax.nn.sigmoid(gt))
        o_ref[:, h * HD:(h + 1) * HD] = y.astype(bf16)

    @pl.when(c == n_chunks - 1)
    def _():
        so_ref[...] = s_scr[...]


def _gdn(z, row_off, nb, t, conv_w, a_log, dt_bias, norm_g, s0=None, prev8=None):
    c_in = min(t, CHUNK)
    n_chunks = t // c_in
    has_state = s0 is not None
    levels = max(1, int(math.log2(c_in)))
    rb = row_off // c_in
    w3 = 3 * NH * HD
    if prev8 is None:
        prev_arr = z
        prev_spec = pl.BlockSpec((8, w3), lambda b, c: (jnp.maximum((row_off + b * t + c * c_in) // 8 - 1, 0), 0))
    else:
        prev_arr = prev8
        prev_spec = pl.BlockSpec((None, 8, w3), lambda b, c: (b, 0, 0))
    pad128 = lambda v: jnp.zeros((1, HD), f32).at[0, :NH].set(v.astype(f32))
    in_specs = [
        pl.BlockSpec((c_in, w3), lambda b, c: (rb + b * n_chunks + c, 0)),
        prev_spec,
        pl.BlockSpec((c_in, NH * HD), lambda b, c: (rb + b * n_chunks + c, 3)),
        pl.BlockSpec((c_in, HD), lambda b, c: (rb + b * n_chunks + c, 64)),
        pl.BlockSpec((4, w3), lambda b, c: (0, 0)),
        pl.BlockSpec((1, HD), lambda b, c: (0, 0)),
        pl.BlockSpec((1, HD), lambda b, c: (0, 0)),
        pl.BlockSpec((1, HD), lambda b, c: (0, 0)),
    ]
    args = [z, prev_arr, z, z, conv_w, pad128(a_log), pad128(dt_bias), norm_g.reshape(1, HD)]
    if has_state:
        in_specs.append(pl.BlockSpec((None, NH, HD, HD), lambda b, c: (b, 0, 0, 0)))
        args.append(s0)
    return pl.pallas_call(
        functools.partial(_gdn_body, c_in=c_in, n_chunks=n_chunks, levels=levels, has_state=has_state,
                          first_chunk_zero_prev=prev8 is None),
        out_shape=(jax.ShapeDtypeStruct((nb * t, NH * HD), bf16), jax.ShapeDtypeStruct((nb, NH, HD, HD), f32)),
        grid=(nb, n_chunks),
        in_specs=in_specs,
        out_specs=(pl.BlockSpec((c_in, NH * HD), lambda b, c: (b * n_chunks + c, 0)),
                   pl.BlockSpec((None, NH, HD, HD), lambda b, c: (b, 0, 0, 0))),
        scratch_shapes=[pltpu.VMEM((NH, HD, HD), f32)],
        compiler_params=_cparams(2),
        name="gdn",
    )(*args)


def _mlstm_body(*refs, c_in, n_chunks, has_state):
    if has_state:
        (q_ref, k_ref, v_ref, og_ref, sm_ref, ib_ref, fb_ref, ng_ref, c0_ref, n0_ref, m0_ref,
         o_ref, co_ref, no_ref, mo_ref, c_scr, n_scr, m_scr) = refs
    else:
        (q_ref, k_ref, v_ref, og_ref, sm_ref, ib_ref, fb_ref, ng_ref,
         o_ref, co_ref, no_ref, mo_ref, c_scr, n_scr, m_scr) = refs
    c = pl.program_id(1)
    C = CHUNK

    @pl.when(c == 0)
    def _():
        if has_state:
            c_scr[...] = c0_ref[...]
            n_scr[...] = n0_ref[...]
            m_scr[...] = m0_ref[...]
        else:
            c_scr[...] = jnp.zeros_like(c_scr)
            n_scr[...] = jnp.zeros_like(n_scr)
            m_scr[...] = jnp.zeros_like(m_scr)

    sm = sm_ref[...]
    log_i = _pad_rows(sm + ib_ref[...], C, NEG)
    log_f = _pad_rows(_log_sigmoid(sm + fb_ref[...]), C)
    row, col = _tri_masks(C)
    causal = col <= row
    bcum = _mm_exact_a(jnp.where(causal, 1.0, 0.0), log_f)
    ng = ng_ref[...]
    qa, ka, va, oga = q_ref[...], k_ref[...], v_ref[...], og_ref[...]

    heads = range(NH)
    hsl = [slice(h * HD, (h + 1) * HD) for h in heads]
    q = [_pad_rows(qa[:, hsl[h]], C) for h in heads]
    k = [_pad_rows(ka[:, hsl[h]], C) * (HD ** -0.5) for h in heads]
    v = [_pad_rows(va[:, hsl[h]], C) for h in heads]
    qk = [_mm(q[h], k[h], NT) for h in heads]
    inter = [_mm(q[h], c_scr[h]) for h in heads]
    m, w_inter, sc, a, b, ms = [], [], [], [], [], []
    for h in heads:
        b.append(bcum[:, 3 * NH + h:3 * NH + h + 1])
        a.append(log_i[:, 2 * NH + h:2 * NH + h + 1] - b[h])
        arow = jnp.transpose(jnp.broadcast_to(a[h], (C, C)))
        cm = jnp.max(jnp.where(causal, arow, -jnp.inf), axis=-1, keepdims=True)
        ms.append(m_scr[h:h + 1, 0:1])
        m.append(jnp.maximum(b[h] + ms[h], b[h] + cm))
        w_inter.append(jnp.exp(b[h] + ms[h] - m[h]))
        sc.append(qk[h] * jnp.where(causal, jnp.exp(jnp.where(causal, b[h] + arow - m[h], 0.0)), 0.0))
    kw, d_end = [], []
    for h in heads:
        m_end = m[h][C - 1:C, :]
        b_end = b[h][C - 1:C, :]
        kw.append(k[h] * jnp.exp(b_end + a[h] - m_end))
        d_end.append(jnp.exp(b_end + ms[h] - m_end))
    from_v = [_mm(jnp.concatenate([sc[h], jnp.transpose(kw[h])], axis=0), v[h]) for h in heads]
    for h in heads:
        ns = n_scr[h:h + 1, :]
        num = w_inter[h] * inter[h] + from_v[h][:C]
        den = w_inter[h] * jnp.sum(q[h] * ns, axis=-1, keepdims=True) + jnp.sum(sc[h], axis=-1, keepdims=True)
        hc = num / (jnp.maximum(jnp.abs(den), jnp.exp(-m[h])) + MLSTM_EPS)
        c_scr[h] = d_end[h] * c_scr[h] + from_v[h][C:]
        n_scr[h:h + 1, :] = d_end[h] * ns + jnp.sum(kw[h], axis=0, keepdims=True)
        m_scr[h:h + 1, :] = jnp.broadcast_to(m[h][C - 1:C, :], (1, HD))
        o_ref[:, hsl[h]] = (_rms(hc[:c_in], ng) * jax.nn.sigmoid(oga[:, hsl[h]])).astype(bf16)

    @pl.when(c == n_chunks - 1)
    def _():
        co_ref[...] = c_scr[...]
        no_ref[...] = n_scr[...]
        mo_ref[...] = m_scr[...]


def _mlstm(z, row_off, nb, t, gate_bias, norm_g, c0=None, n0=None, m0=None):
    c_in = min(t, CHUNK)
    n_chunks = t // c_in
    has_state = c0 is not None
    rb = row_off // c_in
    w = NH * HD
    ib = jnp.zeros((1, HD), f32).at[0, 2 * NH:3 * NH].set(gate_bias[0].astype(f32))
    fb = jnp.zeros((1, HD), f32).at[0, 3 * NH:4 * NH].set(gate_bias[1].astype(f32))
    zspec = lambda j: pl.BlockSpec((c_in, w), lambda b, c: (rb + b * n_chunks + c, j))
    in_specs = [zspec(4), zspec(5), zspec(6), zspec(7),
                pl.BlockSpec((c_in, HD), lambda b, c: (rb + b * n_chunks + c, 64)),
                pl.BlockSpec((1, HD), lambda b, c: (0, 0)),
                pl.BlockSpec((1, HD), lambda b, c: (0, 0)),
                pl.BlockSpec((1, HD), lambda b, c: (0, 0))]
    args = [z, z, z, z, z, ib, fb, norm_g.reshape(1, HD)]
    if has_state:
        in_specs += [pl.BlockSpec((None, NH, HD, HD), lambda b, c: (b, 0, 0, 0)),
                     pl.BlockSpec((None, NH, HD), lambda b, c: (b, 0, 0)),
                     pl.BlockSpec((None, NH, HD), lambda b, c: (b, 0, 0))]
        args += [c0, n0, jnp.broadcast_to(m0[..., None], m0.shape + (HD,))]
    return pl.pallas_call(
        functools.partial(_mlstm_body, c_in=c_in, n_chunks=n_chunks, has_state=has_state),
        out_shape=(jax.ShapeDtypeStruct((nb * t, w), bf16), jax.ShapeDtypeStruct((nb, NH, HD, HD), f32),
                   jax.ShapeDtypeStruct((nb, NH, HD), f32), jax.ShapeDtypeStruct((nb, NH, HD), f32)),
        grid=(nb, n_chunks),
        in_specs=in_specs,
        out_specs=(pl.BlockSpec((c_in, w), lambda b, c: (b * n_chunks + c, 0)),
                   pl.BlockSpec((None, NH, HD, HD), lambda b, c: (b, 0, 0, 0)),
                   pl.BlockSpec((None, NH, HD), lambda b, c: (b, 0, 0)),
                   pl.BlockSpec((None, NH, HD), lambda b, c: (b, 0, 0))),
        scratch_shapes=[pltpu.VMEM((NH, HD, HD), f32), pltpu.VMEM((NH, HD), f32), pltpu.VMEM((NH, HD), f32)],
        compiler_params=_cparams(2),
        name="mlstm",
    )(*args)


def _block_masks(c, t):
    row, col = _tri_masks(c)
    same = (row // t) == (col // t)
    return row, col, same


def _gdns_body(u_ref, p_ref, gate_ref, sm_ref, cw_ref, alog_ref, dtb_ref, ng_ref, s0_ref, o_ref, so_ref, *, t, levels):
    C = CHUNK
    nseq = C // t
    u = u_ref[...]
    p = p_ref[...]
    cw = cw_ref[...]
    rowm = _iota2(u.shape, 0) % t
    acc = u * cw[3:4]
    for k in range(1, 4):
        sh = jnp.where(rowm < k, pltpu.roll(p, (k - t) % C, axis=0), pltpu.roll(u, k, axis=0))
        acc = acc + sh * cw[3 - k:4 - k]
    qkv = acc * jax.nn.sigmoid(acc)

    sm = sm_ref[...]
    gfull = -jnp.exp(alog_ref[...]) * _softplus(sm + dtb_ref[...])
    beta = jax.nn.sigmoid(sm)
    row, col, same = _block_masks(C, t)
    causal = (col <= row) & same
    strict = (col < row) & same
    eye = jnp.where(row == col, 1.0, 0.0).astype(f32)
    gc = _mm_exact_a(jnp.where(causal, 1.0, 0.0), gfull)
    gsum = _mm_exact_a(jnp.where(same, 1.0, 0.0), gfull)
    colseq = _iota2((HD, C), 1) // t
    gate = gate_ref[...]
    ng = ng_ref[...]

    for h in range(NH):
        q = qkv[:, h * HD:(h + 1) * HD]
        k = qkv[:, (NH + h) * HD:(NH + h + 1) * HD]
        v = qkv[:, (2 * NH + h) * HD:(2 * NH + h + 1) * HD]
        qn = q * lax.rsqrt(jnp.sum(q * q, axis=-1, keepdims=True) + 1e-6) * (HD ** -0.5)
        kn = k * lax.rsqrt(jnp.sum(k * k, axis=-1, keepdims=True) + 1e-6)
        gch = gc[:, h:h + 1]
        gend = gsum[:, h:h + 1]
        bh = beta[:, NH + h:NH + h + 1]
        gcb = jnp.broadcast_to(gch, (C, C))
        decay = jnp.where(causal, jnp.exp(jnp.where(causal, gcb - jnp.transpose(gcb), 0.0)), 0.0)
        eg = jnp.exp(gch)
        kb = kn * bh
        kq = _mm(jnp.concatenate([kb, qn], axis=0), kn, NT)
        a_mat = jnp.where(strict, kq[:C] * decay, 0.0)
        (x_inv,) = _tri_inverse([a_mat], eye, levels)
        (sol,) = _mm_split(x_inv, [jnp.concatenate([v * bh, kb * eg], axis=1)])
        qe = qn * eg
        u_parts, o_parts = [], []
        for b in range(nseq):
            rs = slice(b * t, (b + 1) * t)
            r2 = _mm(jnp.concatenate([sol[rs, HD:], qe[rs]], axis=0), s0_ref[b, h])
            u_parts.append(sol[rs, :HD] - r2[:t])
            o_parts.append(r2[t:])
        uu = jnp.concatenate(u_parts, axis=0)
        o = jnp.concatenate(o_parts, axis=0) + _mm(kq[C:] * decay, uu)
        kdec_t = jnp.transpose(kn * jnp.exp(gend - gch))
        for b in range(nseq):
            so_ref[b, h] = (s0_ref[b, h] * jnp.exp(gend[b * t:b * t + 1, :])
                            + _mm(jnp.where(colseq == b, kdec_t, 0.0), uu))
        gt = gate[:, h * HD:(h + 1) * HD]
        o_ref[:, h * HD:(h + 1) * HD] = (_rms(o, ng) * (gt * jax.nn.sigmoid(gt))).astype(bf16)


def _gdn_decode(z, row_off, nb, t, conv_w, a_log, dt_bias, norm_g, s0, prev):
    nseq = CHUNK // t
    assert t == 8 and nb % nseq == 0 and row_off % CHUNK == 0
    rb = row_off // CHUNK
    w3 = 3 * NH * HD
    pad128 = lambda v: jnp.zeros((1, HD), f32).at[0, :NH].set(v.astype(f32))
    return pl.pallas_call(
        functools.partial(_gdns_body, t=t, levels=int(math.log2(t))),
        out_shape=(jax.ShapeDtypeStruct((nb * t, NH * HD), bf16), jax.ShapeDtypeStruct((nb, NH, HD, HD), f32)),
        grid=(nb // nseq,),
        in_specs=[pl.BlockSpec((CHUNK, w3), lambda i: (rb + i, 0)),
                  pl.BlockSpec((CHUNK, w3), lambda i: (i, 0)),
                  pl.BlockSpec((CHUNK, NH * HD), lambda i: (rb + i, 3)),
                  pl.BlockSpec((CHUNK, HD), lambda i: (rb + i, 64)),
                  pl.BlockSpec((4, w3), lambda i: (0, 0)),
                  pl.BlockSpec((1, HD), lambda i: (0, 0)),
                  pl.BlockSpec((1, HD), lambda i: (0, 0)),
                  pl.BlockSpec((1, HD), lambda i: (0, 0)),
                  pl.BlockSpec((nseq, NH, HD, HD), lambda i: (i, 0, 0, 0))],
        out_specs=(pl.BlockSpec((CHUNK, NH * HD), lambda i: (i, 0)),
                   pl.BlockSpec((nseq, NH, HD, HD), lambda i: (i, 0, 0, 0))),
        compiler_params=_cparams(1),
        name="gdn_decode",
    )(z, prev.reshape(nb * t, w3), z, z, conv_w, pad128(a_log), pad128(dt_bias), norm_g.reshape(1, HD), s0)


def _mlstms_body(q_ref, k_ref, v_ref, og_ref, sm_ref, ib_ref, fb_ref, ng_ref, c0_ref, n0_ref, m0_ref,
                 o_ref, co_ref, no_ref, mo_ref, *, t):
    C = CHUNK
    nseq = C // t
    sm = sm_ref[...]
    log_i = sm + ib_ref[...]
    log_f = _log_sigmoid(sm + fb_ref[...])
    m0r = m0_ref[...]
    row, col, same = _block_masks(C, t)
    causal = (col <= row) & same
    bcum = _mm_exact_a(jnp.where(causal, 1.0, 0.0), log_f)
    last_of_row = jnp.where(col == (row // t) * t + (t - 1), 1.0, 0.0)
    last_of_seq = jnp.where(_iota2((nseq, C), 1) == _iota2((nseq, C), 0) * t + (t - 1), 1.0, 0.0)
    seq_rows = jnp.where(_iota2((nseq, C), 1) // t == _iota2((nseq, C), 0), 1.0, 0.0)
    own_seq = _iota2((C, nseq), 1) == _iota2((C, nseq), 0) // t
    colseq = _iota2((HD, C), 1) // t
    lane = _iota2((C, HD), 1)
    ng = ng_ref[...]
    qa, ka, va, oga = q_ref[...], k_ref[...], v_ref[...], og_ref[...]

    for h in range(NH):
        sl = slice(h * HD, (h + 1) * HD)
        q = qa[:, sl]
        k = ka[:, sl] * (HD ** -0.5)
        v = va[:, sl]
        b = bcum[:, 3 * NH + h:3 * NH + h + 1]
        a = log_i[:, 2 * NH + h:2 * NH + h + 1] - b
        arow = jnp.transpose(jnp.broadcast_to(a, (C, C)))
        cm = jnp.max(jnp.where(causal, arow, -jnp.inf), axis=-1, keepdims=True)
        ms = m0r[:, h:h + 1]
        m = jnp.maximum(b + ms, b + cm)
        w_inter = jnp.exp(b + ms - m)
        dmat = jnp.where(causal, jnp.exp(jnp.where(causal, b + arow - m, 0.0)), 0.0)
        sc = _mm(q, k, NT) * dmat
        ends = _mm_exact_a(last_of_row, jnp.where(lane == 0, b, jnp.where(lane == 1, m, 0.0)))
        b_end, m_end = ends[:, 0:1], ends[:, 1:2]
        w_end = jnp.exp(b_end + a - m_end)
        d_end = jnp.exp(b_end + ms - m_end)
        kw = k * w_end
        kw_t = jnp.transpose(kw)
        nh = n0_ref[h]
        qn = jnp.sum(jnp.where(own_seq, _mm(q, nh, NT), 0.0), axis=-1, keepdims=True)
        inter = jnp.concatenate([_mm(q[b_ * t:(b_ + 1) * t], c0_ref[b_, h]) for b_ in range(nseq)], axis=0)
        num = w_inter * inter + _mm(sc, v)
        den = w_inter * qn + jnp.sum(sc, axis=-1, keepdims=True)
        hc = num / (jnp.maximum(jnp.abs(den), jnp.exp(-m)) + MLSTM_EPS)
        for b_ in range(nseq):
            co_ref[b_, h] = d_end[b_ * t:b_ * t + 1, :] * c0_ref[b_, h] + _mm(jnp.where(colseq == b_, kw_t, 0.0), v)
        per_seq = _mm_exact_a(last_of_seq, jnp.where(lane == 0, d_end, jnp.where(lane == 1, m_end, 0.0)))
        no_ref[h] = per_seq[:, 0:1] * nh + _mm_exact_a(seq_rows, kw)
        mo_ref[h] = jnp.broadcast_to(per_seq[:, 1:2], (nseq, HD))
        o_ref[:, sl] = (_rms(hc, ng) * jax.nn.sigmoid(oga[:, sl])).astype(bf16)


def _mlstm_decode(z, row_off, nb, t, gate_bias, norm_g, c0, n0, m0):
    nseq = CHUNK // t
    assert nb % nseq == 0 and row_off % CHUNK == 0
    rb = row_off // CHUNK
    w = NH * HD
    ib = jnp.zeros((1, HD), f32).at[0, 2 * NH:3 * NH].set(gate_bias[0].astype(f32))
    fb = jnp.zeros((1, HD), f32).at[0, 3 * NH:4 * NH].set(gate_bias[1].astype(f32))
    m_rows = jnp.zeros((nb, t, HD), f32).at[:, :, :NH].set(jnp.broadcast_to(m0[:, None, :], (nb, t, NH))).reshape(nb * t, HD)
    zspec = lambda j: pl.BlockSpec((CHUNK, w), lambda i: (rb + i, j))
    hspec = pl.BlockSpec((NH, nseq, HD), lambda i: (0, i, 0))
    out, c_new, n_new, m_new = pl.pallas_call(
        functools.partial(_mlstms_body, t=t),
        out_shape=(jax.ShapeDtypeStruct((nb * t, w), bf16), jax.ShapeDtypeStruct((nb, NH, HD, HD), f32),
                   jax.ShapeDtypeStruct((NH, nb, HD), f32), jax.ShapeDtypeStruct((NH, nb, HD), f32)),
        grid=(nb // nseq,),
        in_specs=[zspec(4), zspec(5), zspec(6), zspec(7),
                  pl.BlockSpec((CHUNK, HD), lambda i: (rb + i, 64)),
                  pl.BlockSpec((1, HD), lambda i: (0, 0)),
                  pl.BlockSpec((1, HD), lambda i: (0, 0)),
                  pl.BlockSpec((1, HD), lambda i: (0, 0)),
                  pl.BlockSpec((nseq, NH, HD, HD), lambda i: (i, 0, 0, 0)),
                  hspec,
                  pl.BlockSpec((CHUNK, HD), lambda i: (i, 0))],
        out_specs=(pl.BlockSpec((CHUNK, w), lambda i: (i, 0)),
                   pl.BlockSpec((nseq, NH, HD, HD), lambda i: (i, 0, 0, 0)), hspec, hspec),
        compiler_params=_cparams(1),
        name="mlstm_decode",
    )(z, z, z, z, z, ib, fb, norm_g.reshape(1, HD), c0, jnp.swapaxes(n0, 0, 1), m_rows)
    return out, c_new, jnp.swapaxes(n_new, 0, 1), jnp.swapaxes(m_new, 0, 1)


def _even_perm():
    w = NH * HD
    offs = np.cumsum([0, 3 * w, NH, NH, w, w, w, w, NH, NH, w])
    seg = lambda i: np.arange(offs[i], offs[i + 1])
    return np.concatenate([seg(0), seg(3), seg(4), seg(5), seg(6), seg(9), seg(1), seg(2), seg(7), seg(8)])


SCALE = HD ** -0.5


def _kvnorm_body(sk_ref, wk_ref, g_ref, so_ref, wo_ref):
    g = g_ref[...]
    for src, dst, gi in ((sk_ref, so_ref, 1), (wk_ref, wo_ref, 2)):
        x = src[...]
        for j in range(NKV):
            dst[:, j * HD:(j + 1) * HD] = _rms(x[:, j * HD:(j + 1) * HD], g[gi:gi + 1])


def _kvnorm(z, k_g, *, tm=1024):
    m = z.shape[0]
    w = NKV * HD
    return pl.pallas_call(
        _kvnorm_body,
        out_shape=(jax.ShapeDtypeStruct((m, w), f32), jax.ShapeDtypeStruct((m, w), f32)),
        grid=(m // tm,),
        in_specs=[pl.BlockSpec((tm, w), lambda i: (i, 18)), pl.BlockSpec((tm, w), lambda i: (i, 20)),
                  pl.BlockSpec((3, HD), lambda i: (0, 0))],
        out_specs=(pl.BlockSpec((tm, w), lambda i: (i, 0)), pl.BlockSpec((tm, w), lambda i: (i, 0))),
        compiler_params=_cparams(1),
        name="kvnorm",
    )(z, z, k_g)


SBP_HEADS = 4
SBP_QB = 256


def _sb_block_multi(zs, mask, mstrict, rs):
    n = zs[0].shape[0]
    log_betas, log_rests, parts = [], [], []
    for z in zs:
        sp = jnp.log1p(jnp.exp(-jnp.abs(z)))
        log_betas.append(jnp.minimum(z, 0.0) - sp)
        lr = -(jnp.maximum(z, 0.0) + sp)
        if mask is not None:
            lr = jnp.where(mask, lr, 0.0)
        log_rests.append(lr)
        parts.extend(_split2(lr))
    after = jnp.dot(jnp.concatenate(parts, axis=0), mstrict, preferred_element_type=f32)
    atts, new_rs = [], []
    for h, (lb, lr, r) in enumerate(zip(log_betas, log_rests, rs)):
        att = jnp.exp(lb + after[2 * h * n:(2 * h + 1) * n] + after[(2 * h + 1) * n:(2 * h + 2) * n] + r)
        atts.append(att if mask is None else jnp.where(mask, att, 0.0))
        new_rs.append(r + jnp.sum(lr, axis=-1, keepdims=True))
    return atts, new_rs


def _sbp_body(q_ref, k_ref, v_ref, o_ref):
    i = pl.program_id(2)
    row = _iota2((SBP_QB, HD), 0)
    col = _iota2((SBP_QB, HD), 1)
    mstrict = jnp.where(_iota2((HD, HD), 0) > _iota2((HD, HD), 1), 1.0, 0.0).astype(bf16)
    qs = [q_ref[:, h * HD:(h + 1) * HD].astype(bf16) for h in range(SBP_HEADS)]
    n_kb = (i + 1) * (SBP_QB // HD)

    def step(s, carry):
        j = n_kb - 1 - s
        off = pl.multiple_of(j * HD, HD)
        mask = (col + j * HD) < (row + i * SBP_QB)
        zs = [_mm(qs[h], k_ref[pl.ds(off, HD), h * HD:(h + 1) * HD], NT) * SCALE for h in range(SBP_HEADS)]
        atts, rs = _sb_block_multi(zs, mask, mstrict, carry[SBP_HEADS:])
        accs = [carry[h] + _mm(atts[h], v_ref[pl.ds(off, HD), h * HD:(h + 1) * HD]) for h in range(SBP_HEADS)]
        return tuple(accs) + tuple(rs)

    init = tuple([jnp.zeros((SBP_QB, HD), f32)] * SBP_HEADS + [jnp.zeros((SBP_QB, 1), f32)] * SBP_HEADS)
    res = lax.fori_loop(0, n_kb, step, init)
    for h in range(SBP_HEADS):
        o_ref[:, h * HD:(h + 1) * HD] = res[h].astype(bf16)


def _sb_prompt(z, nb, t):
    nq = t // SBP_QB
    w = SBP_HEADS * HD
    ng = NH // SBP_HEADS
    return pl.pallas_call(
        _sbp_body,
        out_shape=jax.ShapeDtypeStruct((nb * t, NH * HD), bf16),
        grid=(nb, ng, nq),
        in_specs=[pl.BlockSpec((SBP_QB, w), lambda b, h, i: (b * nq + i, h)),
                  pl.BlockSpec((t, w), lambda b, h, i: (b, ng + h)),
                  pl.BlockSpec((t, w), lambda b, h, i: (b, 2 * ng + h))],
        out_specs=pl.BlockSpec((SBP_QB, w), lambda b, h, i: (b * nq + i, h)),
        compiler_params=_cparams(3),
        name="sb_prompt",
    )(z, z, z)


SBS_PAGES = 8


def _head_rows(page_ref, h, n_heads):
    return page_ref[pl.ds(h, PAGE, stride=n_heads), :]


def _pages2d(cache):
    return cache.reshape(cache.shape[0], cache.shape[1] * cache.shape[2], cache.shape[3])


def _sbs_body(pt_ref, q_ref, kn_ref, vn_ref, *rest, n_steps, t):
    del pt_ref
    kp_refs, vp_refs = rest[:SBS_PAGES], rest[SBS_PAGES:2 * SBS_PAGES]
    o_ref, acc_scr, r_scr = rest[2 * SBS_PAGES:]
    s = pl.program_id(1)

    @pl.when(s == 0)
    def _():
        acc_scr[...] = jnp.zeros_like(acc_scr)
        r_scr[...] = jnp.zeros_like(r_scr)

    rows = NH * t
    row = _iota2((rows, HD), 0)
    col = _iota2((rows, HD), 1)
    mrow = _iota2((HD, HD), 0)
    mcol = _iota2((HD, HD), 1)
    mstrict = jnp.where(mrow > mcol, 1.0, 0.0).astype(bf16)
    qa = q_ref[...]

    def load():
        return r_scr[:, 0:1], [acc_scr[h * t:(h + 1) * t, :] for h in range(NH)]

    def store(r, accs):
        r_scr[...] = jnp.broadcast_to(r, r_scr.shape)
        for h in range(NH):
            acc_scr[h * t:(h + 1) * t, :] = accs[h]

    @pl.when(s == 0)
    def _():
        r, accs = load()
        z = jnp.concatenate([_mm(qa[:, h * HD:(h + 1) * HD], _pad_rows(kn_ref[:, h * HD:(h + 1) * HD], HD), NT)
                             for h in range(NH)], axis=0) * SCALE
        (att,), (r,) = _sb_block_multi([z], col < (row % t), mstrict, [r])
        store(r, [accs[h] + _mm(att[h * t:(h + 1) * t], _pad_rows(vn_ref[:, h * HD:(h + 1) * HD], HD))
                  for h in range(NH)])

    slots = range(SBS_PAGES - 1, -1, -1)
    r, accs = load()
    z = jnp.concatenate(
        [_mm(qa[:, h * HD:(h + 1) * HD],
             jnp.concatenate([_head_rows(kp_refs[jj], h, NH).astype(bf16) for jj in slots], axis=0), NT)
         for h in range(NH)], axis=0) * SCALE
    sp = jnp.log1p(jnp.exp(-jnp.abs(z)))
    log_beta = jnp.minimum(z, 0.0) - sp
    log_rest = -(jnp.maximum(z, 0.0) + sp)
    blocks = [log_rest[:, c * HD:(c + 1) * HD] for c in range(SBS_PAGES)]
    after = jnp.dot(jnp.concatenate([x for blk in blocks for x in _split2(blk)], axis=0), mstrict,
                    preferred_element_type=f32)
    atts = [None] * SBS_PAGES
    for c in range(SBS_PAGES - 1, -1, -1):
        aft = after[2 * c * rows:(2 * c + 1) * rows] + after[(2 * c + 1) * rows:(2 * c + 2) * rows]
        atts[c] = jnp.exp(log_beta[:, c * HD:(c + 1) * HD] + aft + r)
        r = r + jnp.sum(blocks[c], axis=-1, keepdims=True)
    att = jnp.concatenate(atts, axis=1)
    store(r, [accs[h] + _mm(att[h * t:(h + 1) * t],
                            jnp.concatenate([_head_rows(vp_refs[jj], h, NH).astype(bf16) for jj in slots], axis=0))
              for h in range(NH)])

    @pl.when(s == n_steps - 1)
    def _():
        for h in range(NH):
            o_ref[:, h * HD:(h + 1) * HD] = acc_scr[h * t:(h + 1) * t, :].astype(bf16)


def _sb_sample(z, row_off, nb, t, cache_k, cache_v, page_table):
    n_pages = page_table.shape[1]
    n_steps = n_pages // SBS_PAGES
    rb = row_off // t
    page = lambda jj: pl.BlockSpec((None, PAGE * NH, HD),
                                   lambda b, s, pt: (pt[b, n_pages - 1 - (s * SBS_PAGES + jj)], 0, 0))
    pages = [page(jj) for jj in range(SBS_PAGES)]
    grid_spec = pltpu.PrefetchScalarGridSpec(
        num_scalar_prefetch=1,
        grid=(nb, n_steps),
        in_specs=[pl.BlockSpec((t, NH * HD), lambda b, s, pt: (rb + b, 0)),
                  pl.BlockSpec((t, NH * HD), lambda b, s, pt: (rb + b, 1)),
                  pl.BlockSpec((t, NH * HD), lambda b, s, pt: (rb + b, 2))] + pages + pages,
        out_specs=pl.BlockSpec((t, NH * HD), lambda b, s, pt: (b, 0)),
        scratch_shapes=[pltpu.VMEM((NH * t, HD), f32), pltpu.VMEM((NH * t, HD), f32)],
    )
    return pl.pallas_call(
        functools.partial(_sbs_body, n_steps=n_steps, t=t),
        out_shape=jax.ShapeDtypeStruct((nb * t, NH * HD), bf16),
        grid_spec=grid_spec,
        compiler_params=_cparams(2),
        name="sb_sample",
    )(page_table, z, z, z, *([_pages2d(cache_k)] * SBS_PAGES), *([_pages2d(cache_v)] * SBS_PAGES))


def _t5_bucket(dist):
    n = jnp.maximum(dist, 0)
    exact = N_BUCKETS // 2
    nf = jnp.maximum(n, 1).astype(f32)
    large = exact + (jnp.log(nf / exact) / math.log(MAX_DISTANCE / exact) * (N_BUCKETS - exact)).astype(i32)
    return jnp.where(n < exact, n, jnp.minimum(large, N_BUCKETS - 1))


def _bias_of_bucket(bucket, rel_ref, h):
    out = jnp.zeros(bucket.shape, f32)
    for k in range(N_BUCKETS):
        out = jnp.where(bucket == k, rel_ref[k, h], out)
    return out


def _softmax_step(m, l, acc, logits, mask, v):
    lg = jnp.where(mask, logits, NEG)
    m_new = jnp.maximum(m, jnp.max(lg, axis=-1, keepdims=True))
    alpha = jnp.exp(m - m_new)
    e = jnp.where(mask, jnp.exp(lg - m_new), 0.0)
    return m_new, alpha * l + jnp.sum(e, axis=-1, keepdims=True), alpha * acc + _mm(e, v)


def _softmax_full(logits, mask):
    lg = jnp.where(mask, logits, NEG)
    e = jnp.where(mask, jnp.exp(lg - jnp.max(lg, axis=-1, keepdims=True)), 0.0)
    return e / jnp.maximum(jnp.sum(e, axis=-1, keepdims=True), 1e-30)


def _select_blocks(imp_sel, cur, n_sel):
    rows = imp_sel.shape[0]
    imp_pad = jnp.concatenate([imp_sel, jnp.zeros((rows, HD - imp_sel.shape[1]), f32)], axis=1)
    blk = _iota2((rows, HD), 1)
    forced = jnp.where(blk == cur, 2.0 * FORCE_SCORE, jnp.where(blk == 0, FORCE_SCORE, -FORCE_SCORE))
    score = jnp.where((blk < cur) & (blk > 0), imp_pad, forced)
    score = jnp.where(blk < n_sel, score, -jnp.inf)
    sel = jnp.zeros((rows, HD), f32)
    for _ in range(min(TOP_N, n_sel)):
        mx = jnp.max(score, axis=-1, keepdims=True)
        idx = jnp.min(jnp.where(score == mx, blk, 1 << 30), axis=-1, keepdims=True)
        hit = blk == idx
        sel = jnp.where(hit, 1.0, sel)
        score = jnp.where(hit, -jnp.inf, score)
    return sel


def _compress(get_rows, cw_ref, cb, ckg, n_groups_rows):
    del n_groups_rows
    xk = jnp.concatenate([get_rows(0, t).astype(bf16) for t in range(CMP_BLOCK)], axis=1)
    xv = jnp.concatenate([get_rows(1, t).astype(bf16) for t in range(CMP_BLOCK)], axis=1)
    return _rms(_mm(xk, cw_ref[0]) + cb[0:1], ckg), _mm(xv, cw_ref[1]) + cb[1:2]


def _cmp_order(n_cmp, shape, axis):
    c = _iota2(shape, axis)
    half = n_cmp // 2
    return 2 * (c % half) + c // half


def _nsap_body(q_ref, gt_ref, ck0_ref, ck1_ref, cv0_ref, cv1_ref, sk_ref, sv_ref, wk_ref, wv_ref, cw_ref, cb_ref, ckg_ref,
               qg_ref, rel_ref, o_ref, kc_scr, vc_scr, bias_scr, *, t):
    b = pl.program_id(0)
    i = pl.program_id(1)
    n_cmp = t // CMP_BLOCK
    half = n_cmp // 2
    n_sel = t // SEL_BLOCK
    QB = HD
    cmp_refs = ((ck0_ref, ck1_ref), (cv0_ref, cv1_ref))

    @pl.when(i == 0)
    def _():
        for g in range(NKV):
            def get_rows(kind, tt, g=g):
                ref = cmp_refs[kind][g]
                return jnp.concatenate([ref[pl.ds(tt, half, stride=2 * CMP_BLOCK), :],
                                        ref[pl.ds(CMP_BLOCK + tt, half, stride=2 * CMP_BLOCK), :]], axis=0)
            kc, vc = _compress(get_rows, cw_ref, cb_ref[...], ckg_ref[...], n_cmp)
            kc_scr[g] = kc
            vc_scr[g] = vc

    @pl.when((b == 0) & (i == 0))
    def _():
        r_ = _iota2((QB, QB), 0)
        c_ = _iota2((QB, QB), 1)
        for kk in range(3):
            bucket = _t5_bucket(r_ - c_ + QB * kk)
            for h in range(NH):
                bias_scr[kk, h] = _bias_of_bucket(bucket, rel_ref, h)

    qall = q_ref[...]
    qg = qg_ref[...]
    qs = [jnp.concatenate([_rms(qall[:, (g * NREP + r) * HD:(g * NREP + r + 1) * HD], qg) for r in range(NREP)], axis=0)
          for g in range(NKV)]
    rep = lambda x: jnp.concatenate([x] * NREP, axis=0)

    qpos_c = i * QB + _iota2((QB, n_cmp), 0)
    dist_c = qpos_c - (_cmp_order(n_cmp, (QB, n_cmp), 1) * CMP_BLOCK + CMP_BLOCK - 1)
    bucket_c = _t5_bucket(dist_c)
    mask_c = rep(dist_c >= 0)
    cur = (i * QB + _iota2((QB, 1), 0)) // SEL_BLOCK
    o_cmp, sels = [], []
    for g in range(NKV):
        bias = jnp.concatenate([_bias_of_bucket(bucket_c, rel_ref, g * NREP + r) for r in range(NREP)], axis=0)
        p = _softmax_full(_mm(qs[g], kc_scr[g], NT) * SCALE + bias, mask_c)
        o_cmp.append(_mm(p, vc_scr[g]))
        imp = p[0:QB] + p[QB:2 * QB] + p[2 * QB:3 * QB] + p[3 * QB:4 * QB]
        sels.append(_select_blocks(imp[:, :half] + imp[:, half:], cur, n_sel))

    KP = 2 * QB
    krow = _iota2((QB, KP), 0)
    kcol = _iota2((QB, KP), 1)

    def init():
        return tuple(x for _ in range(NKV) for x in (jnp.full((NREP * QB, 1), NEG, f32), jnp.zeros((NREP * QB, 1), f32),
                                                     jnp.zeros((NREP * QB, HD), f32)))

    def attend(carry, p, k_ref, v_ref, mask_of):
        off = pl.multiple_of(p * KP, KP)
        dist = i * QB + krow - (p * KP + kcol)
        d0 = jnp.clip(i - 2 * p, 0, 2)
        d1 = jnp.clip(i - 2 * p - 1, 0, 2)
        out = []
        for g in range(NKV):
            m, l, acc = carry[3 * g:3 * g + 3]
            bias = jnp.concatenate([jnp.concatenate([bias_scr[d0, g * NREP + r], bias_scr[d1, g * NREP + r]], axis=1)
                                    for r in range(NREP)], axis=0)
            logits = _mm(qs[g], k_ref[pl.ds(off, KP), g * HD:(g + 1) * HD], NT) * SCALE + bias
            out.extend(_softmax_step(m, l, acc, logits, rep(mask_of(g, p, dist)), v_ref[pl.ds(off, KP), g * HD:(g + 1) * HD]))
        return tuple(out)

    def sel_mask(g, p, dist):
        er = _iota2((HD, KP), 0)
        ec = _iota2((HD, KP), 1)
        expand = jnp.where(er == (KP // SEL_BLOCK) * p + ec // SEL_BLOCK, 1.0, 0.0)
        return (_mm(sels[g], expand) > 0.5) & (dist >= 0)

    def win_mask(g, p, dist):
        return (dist >= 0) & (dist < WINDOW)

    last_pair = i // 2
    first_win = jnp.maximum(i - WINDOW // QB, 0) // 2
    c_sel = lax.fori_loop(0, last_pair + 1, lambda p, c: attend(c, p, sk_ref, sv_ref, sel_mask), init())
    c_win = lax.fori_loop(first_win, last_pair + 1, lambda p, c: attend(c, p, wk_ref, wv_ref, win_mask), init())

    gts = jax.nn.sigmoid(gt_ref[...])
    for h in range(NH):
        g, r = divmod(h, NREP)
        sl = slice(r * QB, (r + 1) * QB)
        o_s = c_sel[3 * g + 2][sl] / jnp.maximum(c_sel[3 * g + 1][sl], 1e-30)
        o_w = c_win[3 * g + 2][sl] / jnp.maximum(c_win[3 * g + 1][sl], 1e-30)
        o = gts[:, 3 * h:3 * h + 1] * o_cmp[g][sl] + gts[:, 3 * h + 1:3 * h + 2] * o_s + gts[:, 3 * h + 2:3 * h + 3] * o_w
        o_ref[:, h * HD:(h + 1) * HD] = o.astype(bf16)


def _nsa_prompt(z, sk, wk, nb, t, cmp_w, cmp_b, cmp_k_g, q_g, rel_bias):
    nq = t // HD
    n_cmp = t // CMP_BLOCK
    w = NKV * HD
    full = lambda j: pl.BlockSpec((t, w), lambda b, i: (b, j))
    head = lambda j: pl.BlockSpec((t, HD), lambda b, i: (b, j))
    return pl.pallas_call(
        functools.partial(_nsap_body, t=t),
        out_shape=jax.ShapeDtypeStruct((nb * t, NH * HD), bf16),
        grid=(nb, nq),
        in_specs=[pl.BlockSpec((HD, NH * HD), lambda b, i: (b * nq + i, 3)),
                  pl.BlockSpec((HD, HD), lambda b, i: (b * nq + i, 44)),
                  head(32), head(33), head(34), head(35), full(0), full(19), full(0), full(21),
                  pl.BlockSpec((2, CMP_BLOCK * HD, HD), lambda b, i: (0, 0, 0)),
                  pl.BlockSpec((2, HD), lambda b, i: (0, 0)),
                  pl.BlockSpec((1, HD), lambda b, i: (0, 0)),
                  pl.BlockSpec((1, HD), lambda b, i: (0, 0)),
                  pl.BlockSpec(memory_space=pltpu.SMEM)],
        out_specs=pl.BlockSpec((HD, NH * HD), lambda b, i: (b * nq + i, 0)),
        scratch_shapes=[pltpu.VMEM((NKV, n_cmp, HD), f32), pltpu.VMEM((NKV, n_cmp, HD), f32),
                        pltpu.VMEM((3, NH, HD, HD), f32)],
        compiler_params=_cparams(2),
        name="nsa_prompt",
    )(z, z, z, z, z, z, sk, z, wk, z, cmp_w.astype(bf16), cmp_b, cmp_k_g.reshape(1, HD), q_g.reshape(1, HD), rel_bias)


def _nsa_q_groups(q_ref, qg):
    qall = q_ref[...]
    return [jnp.concatenate([_rms(qall[:, (g * NREP + r) * HD:(g * NREP + r + 1) * HD], qg) for r in range(NREP)], axis=0)
            for g in range(NKV)]


CMP_PAGES = 8


def _nsasa_body(pt_ref, q_ref, *rest, n_pages, t):
    del pt_ref
    ckp_refs, cvp_refs = rest[:CMP_PAGES], rest[CMP_PAGES:2 * CMP_PAGES]
    cw_ref, cb_ref, ckg_ref, qg_ref, rel_ref, oc_ref, sel_ref, xk_scr, xv_scr = rest[2 * CMP_PAGES:]
    p = pl.program_id(1)
    pp = 2 * PAGE
    pr_ = _iota2((pp, pp), 0)
    pc_ = _iota2((pp, pp), 1)
    perm = jnp.where(pc_ == (pr_ % 8) * CMP_BLOCK + pr_ // 8, 1.0, 0.0).astype(bf16)
    for pair in range(CMP_PAGES // 2):
        row0 = pl.multiple_of((p * (CMP_PAGES // 2) + pair) * 8, 8)
        for refs, scr in ((ckp_refs, xk_scr), (cvp_refs, xv_scr)):
            for g in range(NKV):
                x2 = jnp.concatenate([_head_rows(refs[2 * pair], g, NKV), _head_rows(refs[2 * pair + 1], g, NKV)], axis=0)
                moved = jnp.dot(perm, x2.astype(bf16), preferred_element_type=f32)
                for tt in range(CMP_BLOCK):
                    scr[g, pl.ds(row0, 8), tt * HD:(tt + 1) * HD] = moved[8 * tt:8 * tt + 8, :]

    @pl.when(p == n_pages // CMP_PAGES - 1)
    def _():
        past = n_pages * PAGE
        n_cmp = (past + t) // CMP_BLOCK
        n_sel = -(-(past + t) // SEL_BLOCK)
        cb = cb_ref[...]
        kc = _rms(_mm(jnp.concatenate([xk_scr[g] for g in range(NKV)], axis=0), cw_ref[0]) + cb[0:1], ckg_ref[...])
        vc = _mm(jnp.concatenate([xv_scr[g] for g in range(NKV)], axis=0), cw_ref[1]) + cb[1:2]
        qs = _nsa_q_groups(q_ref, qg_ref[...])
        rows = NREP * t
        tq = _iota2((rows, n_cmp), 0) % t
        dist_c = past + tq - (_iota2((rows, n_cmp), 1) * CMP_BLOCK + CMP_BLOCK - 1)
        bucket_c = _t5_bucket(dist_c)
        mask_c = dist_c >= 0
        cur = (past + _iota2((t, 1), 0)) // SEL_BLOCK
        ratio = SEL_BLOCK // CMP_BLOCK
        pair_sum = jnp.where(_iota2((n_cmp, n_cmp // ratio), 0) // ratio == _iota2((n_cmp, n_cmp // ratio), 1), 1.0, 0.0)
        for g in range(NKV):
            bias = jnp.concatenate([_bias_of_bucket(bucket_c[r * t:(r + 1) * t], rel_ref, g * NREP + r)
                                    for r in range(NREP)], axis=0)
            pr = _softmax_full(_mm(qs[g], kc[g * n_cmp:(g + 1) * n_cmp], NT) * SCALE + bias, mask_c)
            oc_ref[g * rows:(g + 1) * rows, :] = _mm(pr, vc[g * n_cmp:(g + 1) * n_cmp])
            imp = pr[0:t] + pr[t:2 * t] + pr[2 * t:3 * t] + pr[3 * t:4 * t]
            sel_ref[g * t:(g + 1) * t, :] = _select_blocks(_mm_exact_b(imp, pair_sum), cur, n_sel)


def _nsa_sample_cmp(z, row_off, nb, t, cache_ck, cache_cv, page_table, cmp_w, cmp_b, cmp_k_g, q_g, rel_bias):
    n_pages = page_table.shape[1]
    assert n_pages % CMP_PAGES == 0 and CMP_PAGES % 2 == 0 and t < CMP_BLOCK
    rb = row_off // t
    page = lambda jj: pl.BlockSpec((None, PAGE * NKV, HD), lambda b, p, pt: (pt[b, p * CMP_PAGES + jj], 0, 0))
    pages = [page(jj) for jj in range(CMP_PAGES)]
    cst = lambda *shape: pl.BlockSpec(shape, lambda b, p, pt: (0,) * len(shape))
    grid_spec = pltpu.PrefetchScalarGridSpec(
        num_scalar_prefetch=1,
        grid=(nb, n_pages // CMP_PAGES),
        in_specs=[pl.BlockSpec((t, NH * HD), lambda b, p, pt: (rb + b, 3))] + pages + pages + [
                  cst(2, CMP_BLOCK * HD, HD), cst(2, HD), cst(1, HD), cst(1, HD),
                  pl.BlockSpec(memory_space=pltpu.SMEM)],
        out_specs=(pl.BlockSpec((None, NH * t, HD), lambda b, p, pt: (b, 0, 0)),
                   pl.BlockSpec((None, NKV * t, HD), lambda b, p, pt: (b, 0, 0))),
        scratch_shapes=[pltpu.VMEM((NKV, n_pages * PAGE // CMP_BLOCK, CMP_BLOCK * HD), f32)] * 2,
    )
    return pl.pallas_call(
        functools.partial(_nsasa_body, n_pages=n_pages, t=t),
        out_shape=(jax.ShapeDtypeStruct((nb, NH * t, HD), f32), jax.ShapeDtypeStruct((nb, NKV * t, HD), f32)),
        grid_spec=grid_spec,
        compiler_params=_cparams(2),
        name="nsa_sample_cmp",
    )(page_table, z, *([_pages2d(cache_ck)] * CMP_PAGES), *([_pages2d(cache_cv)] * CMP_PAGES), cmp_w.astype(bf16), cmp_b,
      cmp_k_g.reshape(1, HD), q_g.reshape(1, HD), rel_bias)


NSA_PAGES = 15


def _nsasb_body(pt_ref, q_ref, gt_ref, oc_ref, sel_ref, skn_ref, svn_ref, wkn_ref, wvn_ref, wk_ref, wv_ref,
                skl_ref, svl_ref, *rest, n_pages, t):
    del pt_ref
    skp_refs, svp_refs = rest[:NSA_PAGES], rest[NSA_PAGES:2 * NSA_PAGES]
    qg_ref, rel_ref, o_ref, m_scr, l_scr, acc_scr, ow_scr = rest[2 * NSA_PAGES:]
    s = pl.program_id(1)
    n_steps = (n_pages - 1) // NSA_PAGES
    past = n_pages * PAGE
    rows = NREP * t
    qs = _nsa_q_groups(q_ref, qg_ref[...])
    tq = _iota2((t, HD), 0)
    col = _iota2((t, HD), 1)
    krow = _iota2((HD, HD), 0)
    kcol = _iota2((HD, HD), 1)
    rep = lambda x: jnp.concatenate([x] * NREP, axis=0)

    def bias_of(dist, g):
        bucket = _t5_bucket(dist)
        return jnp.concatenate([_bias_of_bucket(bucket, rel_ref, g * NREP + r) for r in range(NREP)], axis=0)

    def far_bias(g):
        return jnp.concatenate([jnp.full((t, 1), rel_ref[N_BUCKETS - 1, g * NREP + r], f32) for r in range(NREP)], axis=0)

    def attend(state, g, ks, vs, bias, mask):
        m, l, acc = state
        logits = jnp.concatenate([_mm(qs[g], k, NT) for k in ks], axis=1) * SCALE + bias
        mask = rep(mask)
        lg = jnp.where(mask, logits, NEG)
        m_new = jnp.maximum(m, jnp.max(lg, axis=-1, keepdims=True))
        alpha = jnp.exp(m - m_new)
        e = jnp.where(mask, jnp.exp(lg - m_new), 0.0)
        acc = alpha * acc
        for n, v in enumerate(vs):
            acc = acc + _mm(e[:, n * HD:(n + 1) * HD], v)
        return m_new, alpha * l + jnp.sum(e, axis=-1, keepdims=True), acc

    def sel_update(g, ks, vs, bias, mask):
        sl = slice(g * rows, (g + 1) * rows)
        m, l, acc = attend((m_scr[sl, 0:1], l_scr[sl, 0:1], acc_scr[sl, :]), g, ks, vs, bias, mask)
        m_scr[sl, :] = jnp.broadcast_to(m, (rows, HD))
        l_scr[sl, :] = jnp.broadcast_to(l, (rows, HD))
        acc_scr[sl, :] = acc

    def picked(g, first_block, n_blocks):
        er = _iota2((HD, n_blocks * HD), 0)
        ec = _iota2((HD, n_blocks * HD), 1)
        expand = jnp.where(er == first_block + ec // SEL_BLOCK, 1.0, 0.0)
        return _mm(sel_ref[g * t:(g + 1) * t, :], expand) > 0.5

    @pl.when(s == 0)
    def _():
        m_scr[...] = jnp.full(m_scr.shape, NEG, f32)
        l_scr[...] = jnp.zeros_like(l_scr)
        acc_scr[...] = jnp.zeros_like(acc_scr)
        nwin = WINDOW // HD
        tqw = _iota2((t, WINDOW + HD), 0)
        colw = _iota2((t, WINDOW + HD), 1)
        dist_w = WINDOW + tqw - colw
        ok_w = (dist_w >= 0) & (dist_w < WINDOW) & (colw < WINDOW + t)
        tqs = _iota2((t, 2 * HD), 0)
        cols = _iota2((t, 2 * HD), 1)
        dist_s = PAGE + tqs - cols
        ok_s = (dist_s >= 0) & (cols < PAGE + t)
        for g in range(NKV):
            gs = slice(g * HD, (g + 1) * HD)
            ks = [wk_ref[pl.ds(jj * HD * NKV + g, HD, stride=NKV), :] for jj in range(nwin)] + [_pad_rows(wkn_ref[:, gs], HD)]
            vs = [wv_ref[pl.ds(jj * HD * NKV + g, HD, stride=NKV), :] for jj in range(nwin)] + [_pad_rows(wvn_ref[:, gs], HD)]
            st = (jnp.full((rows, 1), NEG, f32), jnp.zeros((rows, 1), f32), jnp.zeros((rows, HD), f32))
            st = attend(st, g, ks, vs, bias_of(dist_w, g), ok_w)
            ow_scr[g * rows:(g + 1) * rows, :] = st[2] / jnp.maximum(st[1], 1e-30)
            sel_update(g, [_head_rows(skl_ref, g, NKV), _pad_rows(skn_ref[:, gs], HD)],
                       [_head_rows(svl_ref, g, NKV), _pad_rows(svn_ref[:, gs], HD)], bias_of(dist_s, g),
                       picked(g, (past - PAGE) // SEL_BLOCK, 2) & ok_s)

    for g in range(NKV):
        sel_update(g, [_head_rows(skp_refs[jj], g, NKV) for jj in range(NSA_PAGES)],
                   [_head_rows(svp_refs[jj], g, NKV) for jj in range(NSA_PAGES)], far_bias(g),
                   picked(g, (PAGE // SEL_BLOCK) * NSA_PAGES * s, NSA_PAGES))

    @pl.when(s == n_steps - 1)
    def _():
        gts = jax.nn.sigmoid(gt_ref[...])
        for h in range(NH):
            g, r = divmod(h, NREP)
            sl = slice(g * rows + r * t, g * rows + (r + 1) * t)
            o_s = acc_scr[sl, :] / jnp.maximum(l_scr[sl, 0:1], 1e-30)
            o = (gts[:, 3 * h:3 * h + 1] * oc_ref[sl, :] + gts[:, 3 * h + 1:3 * h + 2] * o_s
                 + gts[:, 3 * h + 2:3 * h + 3] * ow_scr[sl, :])
            o_ref[:, h * HD:(h + 1) * HD] = o.astype(bf16)


def _nsa_sample_attn(z, sk, wk, row_off, nb, t, o_cmp, sel, win_k, win_v, cache_sk, cache_sv, page_table, q_g, rel_bias):
    n_pages = page_table.shape[1]
    assert (n_pages - 1) % NSA_PAGES == 0 and PAGE >= MAX_DISTANCE
    rb = row_off // t
    w = NKV * HD
    page = lambda jj: pl.BlockSpec((None, PAGE * NKV, HD), lambda b, s, pt: (pt[b, s * NSA_PAGES + jj], 0, 0))
    pages = [page(jj) for jj in range(NSA_PAGES)]
    last = pl.BlockSpec((None, PAGE * NKV, HD), lambda b, s, pt: (pt[b, n_pages - 1], 0, 0))
    rowsp = lambda width, j: pl.BlockSpec((t, width), lambda b, s, pt: (rb + b, j))
    grid_spec = pltpu.PrefetchScalarGridSpec(
        num_scalar_prefetch=1,
        grid=(nb, (n_pages - 1) // NSA_PAGES),
        in_specs=[rowsp(NH * HD, 3), rowsp(HD, 44),
                  pl.BlockSpec((None, NH * t, HD), lambda b, s, pt: (b, 0, 0)),
                  pl.BlockSpec((None, NKV * t, HD), lambda b, s, pt: (b, 0, 0)),
                  rowsp(w, 0), rowsp(w, 19), rowsp(w, 0), rowsp(w, 21),
                  pl.BlockSpec((None, WINDOW * NKV, HD), lambda b, s, pt: (b, 0, 0)),
                  pl.BlockSpec((None, WINDOW * NKV, HD), lambda b, s, pt: (b, 0, 0)),
                  last, last] + pages + pages + [
                  pl.BlockSpec((1, HD), lambda b, s, pt: (0, 0)),
                  pl.BlockSpec(memory_space=pltpu.SMEM)],
        out_specs=pl.BlockSpec((t, NH * HD), lambda b, s, pt: (b, 0)),
        scratch_shapes=[pltpu.VMEM((NH * t, HD), f32)] * 4,
    )
    return pl.pallas_call(
        functools.partial(_nsasb_body, n_pages=n_pages, t=t),
        out_shape=jax.ShapeDtypeStruct((nb * t, NH * HD), bf16),
        grid_spec=grid_spec,
        compiler_params=_cparams(2),
        name="nsa_sample_attn",
    )(page_table, z, z, o_cmp, sel, sk, z, wk, z, _pages2d(win_k), _pages2d(win_v), _pages2d(cache_sk), _pages2d(cache_sv),
      *([_pages2d(cache_sk)] * NSA_PAGES), *([_pages2d(cache_sv)] * NSA_PAGES), q_g.reshape(1, HD), rel_bias)


def _even_mixer(z, mp, bp, tp, bs, ts, conv_w, a_log, dt_bias, gdn_g, gate_bias, mlstm_g, s0, conv0, c0, n0, m0):
    w3 = 3 * NH * HD
    go_p, gs_p = _gdn(z, 0, bp, tp, conv_w, a_log, dt_bias, gdn_g)
    prev8 = jnp.concatenate([jnp.zeros((bs, 8 - conv0.shape[1], w3), f32), conv0], axis=1)
    go_s, gs_s = _gdn_decode(z, mp, bs, ts, conv_w, a_log, dt_bias, gdn_g, s0, prev8)
    mh_p, mc_p, mn_p, mm_p = _mlstm(z, 0, bp, tp, gate_bias, mlstm_g)
    mh_s, mc_s, mn_s, mm_s = _mlstm_decode(z, mp, bs, ts, gate_bias, mlstm_g, c0, n0, m0)
    mix = ((go_p, mh_p), (go_s, mh_s))
    keep = conv0.shape[1]
    conv_p = jnp.stack([z[(b + 1) * tp - keep:(b + 1) * tp, :w3] for b in range(bp)])
    conv_s = z[mp:, :w3].reshape(bs, ts, w3)[:, ts - keep:]
    states = (gs_p, gs_s, conv_p, conv_s, mc_p, mc_s, mn_p, mn_s, mm_p[..., 0], mm_s[..., 0])
    return mix, states


def _odd_mixer(z, mp, bp, tp, bs, ts, page_table, caches, win_k0, win_v0, q_g, k_g, cmp_w, cmp_b, rel_bias):
    sb_k, sb_v, cmp_k, cmp_v, sel_k, sel_v = caches
    sk, wk = _kvnorm(z, k_g)
    sb_p = _sb_prompt(z, bp, tp)
    sb_s = _sb_sample(z, mp, bs, ts, sb_k, sb_v, page_table)
    ns_p = _nsa_prompt(z, sk, wk, bp, tp, cmp_w, cmp_b, k_g[0], q_g, rel_bias)
    o_cmp, sel = _nsa_sample_cmp(z, mp, bs, ts, cmp_k, cmp_v, page_table, cmp_w, cmp_b, k_g[0], q_g, rel_bias)
    ns_s = _nsa_sample_attn(z, sk, wk, mp, bs, ts, o_cmp, sel, win_k0, win_v0, sel_k, sel_v, page_table, q_g, rel_bias)
    mix = ((sb_p, ns_p), (sb_s, ns_s))

    w = NH * HD
    kw = NKV * HD

    def rows(arr, lo, width, heads):
        sl = arr[:, lo:lo + width]
        return sl[:mp].reshape(bp, tp, heads, HD), sl[mp:].reshape(bs, ts, heads, HD)

    sbk_p, sbk_s = rows(z, w, w, NH)
    sbv_p, sbv_s = rows(z, 2 * w, w, NH)
    ck_p, ck_s = rows(z, 4 * w, kw, NKV)
    cv_p, cv_s = rows(z, 4 * w + kw, kw, NKV)
    sk_p, sk_s = rows(sk, 0, kw, NKV)
    sv_p, sv_s = rows(z, 4 * w + 3 * kw, kw, NKV)
    wk_p, wk_s = rows(wk, 0, kw, NKV)
    wv_p, wv_s = rows(z, 4 * w + 5 * kw, kw, NKV)
    keep_p = min(WINDOW, tp)
    win = lambda old, new: jnp.concatenate([old, new], axis=1)[:, -min(WINDOW, old.shape[1] + ts):]
    states = (sbk_p, sbk_s, sbv_p, sbv_s, ck_p, ck_s, cv_p, cv_s, sk_p, sk_s, sv_p, sv_s,
              wk_p[:, tp - keep_p:], win(win_k0, wk_s), wv_p[:, tp - keep_p:], win(win_v0, wv_s))
    return mix, states


def kernel(x_prompt, x_sample, state_gdn_s, state_gdn_conv, state_mlstm_c, state_mlstm_n, state_mlstm_m, cache_sb_k, cache_sb_v, cache_nsa_cmp_k, cache_nsa_cmp_v, cache_nsa_sel_k, cache_nsa_sel_v, state_nsa_win_k, state_nsa_win_v, page_table, norm_g, ffn_w_gate, ffn_w_up, ffn_w_down, even_w_in, gdn_conv_w, gdn_a_log, gdn_dt_bias, gdn_norm_g, mlstm_gate_bias, mlstm_norm_g, even_w_out, odd_w_in, nsa_q_norm_g, nsa_k_norm_g, nsa_cmp_w, nsa_cmp_b, odd_w_out, rel_bias):
    bp, tp, d = x_prompt.shape
    bs, ts, _ = x_sample.shape
    mp = bp * tp
    xs = jnp.concatenate([x_prompt.reshape(mp, d), x_sample.reshape(bs * ts, d)], axis=0)
    depth = norm_g.shape[0]
    even_states, odd_states = [], []
    for layer in range(depth):
        j = layer // 2
        ffn = lambda x, n, i: _ffn(x, norm_g[layer, n], ffn_w_gate, ffn_w_up, ffn_w_down, layer, i)
        xs = ffn(xs, 0, 0)
        if layer % 2 == 0:
            w_in = even_w_in[j][:, _even_perm()]
            w_in = _pad_cols(w_in.astype(bf16), PROJ_TN)
            z = _rms_matmul(xs, norm_g[layer, 1], w_in)
            mix, st = _even_mixer(z, mp, bp, tp, bs, ts, gdn_conv_w[j], gdn_a_log[j], gdn_dt_bias[j], gdn_norm_g[j],
                                  mlstm_gate_bias[j], mlstm_norm_g[j], state_gdn_s[j], state_gdn_conv[j],
                                  state_mlstm_c[j], state_mlstm_n[j], state_mlstm_m[j])
            even_states.append(st)
            w_out = even_w_out[j]
        else:
            w_in = _pad_cols(odd_w_in[j].astype(bf16), PROJ_TN)
            z = _rms_matmul(xs, norm_g[layer, 1], w_in)
            caches = (cache_sb_k[j], cache_sb_v[j], cache_nsa_cmp_k[j], cache_nsa_cmp_v[j], cache_nsa_sel_k[j],
                      cache_nsa_sel_v[j])
            mix, st = _odd_mixer(z, mp, bp, tp, bs, ts, page_table, caches, state_nsa_win_k[j], state_nsa_win_v[j],
                                 nsa_q_norm_g[j], nsa_k_norm_g[j], nsa_cmp_w[j], nsa_cmp_b[j], rel_bias)
            odd_states.append(st)
            w_out = odd_w_out[j]
        xs = _out_proj(mix[0], mix[1], w_out.astype(bf16), xs)
        xs = ffn(xs, 2, 1)
    stack = lambda sts, i: jnp.stack([s[i] for s in sts])
    outs = [xs[:mp].reshape(bp, tp, d), xs[mp:].reshape(bs, ts, d)]
    outs += [stack(even_states, i) for i in range(10)]
    outs += [stack(odd_states, i) for i in range(16)]
    return tuple(outs)
```

```python
import functools
import math

import jax
import jax.numpy as jnp
import numpy as np
from jax import lax
from jax.experimental import pallas as pl
from jax.experimental.pallas import tpu as pltpu

f32 = jnp.float32
bf16 = jnp.bfloat16
i32 = jnp.int32

HD = 128
NH = 8
NKV = 2
NREP = 4
RMS_EPS = 1e-6
MLSTM_EPS = 1e-6
CHUNK = 128
PAGE = 128
CMP_BLOCK = 32
SEL_BLOCK = 64
TOP_N = 8
WINDOW = 512
FORCE_SCORE = 1.0e4
N_BUCKETS = 32
MAX_DISTANCE = 128
NEG = -1e30
VMEM_LIMIT = 56 * 1024 * 1024

NN = (((1,), (0,)), ((), ()))
NT = (((1,), (1,)), ((), ()))
TN = (((0,), (0,)), ((), ()))


FFN_VMEM_LIMIT = 60 * 1024 * 1024


def _cparams(n_axes, vmem_limit=None):
    return pltpu.CompilerParams(dimension_semantics=("arbitrary",) * n_axes,
                                vmem_limit_bytes=VMEM_LIMIT if vmem_limit is None else vmem_limit)


def _mm(a, b, dims=NN):
    return lax.dot_general(a.astype(bf16), b.astype(bf16), dims, preferred_element_type=f32)


def _split3(a):
    a0 = a.astype(bf16)
    r = a - a0.astype(f32)
    a1 = r.astype(bf16)
    a2 = (r - a1.astype(f32)).astype(bf16)
    return a0, a1, a2


def _mm_exact_b(a, b01, dims=NN):
    a0, a1, a2 = _split3(a)
    b01 = b01.astype(bf16)
    d = lambda x: lax.dot_general(x, b01, dims, preferred_element_type=f32)
    return d(a0) + d(a1) + d(a2)


def _mm_exact_a(a01, b, dims=NN):
    b0, b1, b2 = _split3(b)
    a01 = a01.astype(bf16)
    d = lambda x: lax.dot_general(a01, x, dims, preferred_element_type=f32)
    return d(b0) + d(b1) + d(b2)


def _split2(a):
    ah = a.astype(bf16)
    return ah, (a - ah.astype(f32)).astype(bf16)


def _mm_split(a, bs):
    n = a.shape[0]
    rhs = jnp.concatenate([x for b in bs for x in _split2(b)], axis=1)
    r = jnp.dot(jnp.concatenate(_split2(a), axis=0), rhs, preferred_element_type=f32)
    outs, off = [], 0
    for b in bs:
        w = b.shape[1]
        outs.append((r[:n, off:off + w] + r[:n, off + w:off + 2 * w]) + (r[n:, off:off + w] + r[n:, off + w:off + 2 * w]))
        off += 2 * w
    return outs


def _tri_inverse(a_mats, eye, levels):
    ys = [eye - a for a in a_mats]
    if levels < 2:
        return ys
    qs = [_mm_split(a, [a])[0] for a in a_mats]
    for lvl in range(levels - 1):
        last = lvl == levels - 2
        res = [_mm_split(q, [y] if last else [y, q]) for q, y in zip(qs, ys)]
        ys = [y + r[0] for y, r in zip(ys, res)]
        if not last:
            qs = [r[1] for r in res]
    return ys


def _softplus(x):
    return jnp.maximum(x, 0.0) + jnp.log1p(jnp.exp(-jnp.abs(x)))


def _log_sigmoid(x):
    return -_softplus(-x)


def _rms(x, g):
    return x * lax.rsqrt(jnp.mean(x * x, axis=-1, keepdims=True) + RMS_EPS) * g


def _pad_rows(x, rows, value=0.0):
    if x.shape[0] == rows:
        return x
    return jnp.concatenate([x, jnp.full((rows - x.shape[0],) + x.shape[1:], value, x.dtype)], axis=0)


def _iota2(shape, axis):
    return lax.broadcasted_iota(i32, shape, axis)


def _ffn_body(x_ref, g_ref, wg_ref, wu_ref, wd_ref, o_ref, h_ref, *, nf):
    f = pl.program_id(1)

    del nf

    @pl.when(f == 0)
    def _():
        x = x_ref[...]
        h_ref[...] = _rms(x, g_ref[...]).astype(bf16)
        o_ref[...] = x

    h = h_ref[...]
    a = jnp.dot(h, wg_ref[...].astype(bf16), preferred_element_type=f32)
    u = jnp.dot(h, wu_ref[...].astype(bf16), preferred_element_type=f32)
    act = (0.5 * a * jax.nn.sigmoid(a) * u).astype(bf16)
    o_ref[...] += jnp.dot(act, wd_ref[...].astype(bf16), preferred_element_type=f32)


def _ffn(x, g, wg, wu, wd, layer, idx, *, tm=1024, tf=512):
    m, d = x.shape
    fdim = wg.shape[-1]
    nf = fdim // tf
    once = pl.Buffered(1)
    return pl.pallas_call(
        functools.partial(_ffn_body, nf=nf),
        out_shape=jax.ShapeDtypeStruct((m, d), f32),
        grid=(m // tm, nf),
        in_specs=[
            pl.BlockSpec((tm, d), lambda i, f: (i, 0), pipeline_mode=once),
            pl.BlockSpec((1, d), lambda i, f: (0, 0)),
            pl.BlockSpec((None, None, d, tf), lambda i, f: (layer, idx, 0, f)),
            pl.BlockSpec((None, None, d, tf), lambda i, f: (layer, idx, 0, f)),
            pl.BlockSpec((None, None, tf, d), lambda i, f: (layer, idx, f, 0)),
        ],
        out_specs=pl.BlockSpec((tm, d), lambda i, f: (i, 0), pipeline_mode=once),
        scratch_shapes=[pltpu.VMEM((tm, d), bf16)],
        compiler_params=_cparams(2, FFN_VMEM_LIMIT),
        name="ffn",
    )(x, g.reshape(1, d), wg, wu, wd)


def _rmsmm_body(x_ref, g_ref, w_ref, o_ref, h_ref):
    @pl.when(pl.program_id(1) == 0)
    def _():
        h_ref[...] = _rms(x_ref[...], g_ref[...]).astype(bf16)

    o_ref[...] = jnp.dot(h_ref[...], w_ref[...], preferred_element_type=f32)


PROJ_TN = 768


def _pad_cols(w, mult):
    pad = -w.shape[1] % mult
    return w if pad == 0 else jnp.concatenate([w, jnp.zeros((w.shape[0], pad), w.dtype)], axis=1)


def _rms_matmul(x, g, w, *, tm=1024, tn=PROJ_TN):
    m, d = x.shape
    n = w.shape[1]
    return pl.pallas_call(
        _rmsmm_body,
        out_shape=jax.ShapeDtypeStruct((m, n), f32),
        grid=(m // tm, n // tn),
        in_specs=[
            pl.BlockSpec((tm, d), lambda i, j: (i, 0), pipeline_mode=pl.Buffered(1)),
            pl.BlockSpec((1, d), lambda i, j: (0, 0)),
            pl.BlockSpec((d, tn), lambda i, j: (0, j)),
        ],
        out_specs=pl.BlockSpec((tm, tn), lambda i, j: (i, j)),
        scratch_shapes=[pltpu.VMEM((tm, d), bf16)],
        compiler_params=_cparams(2),
        name="rms_matmul",
    )(x, g.reshape(1, d), w)


def _outproj_body(ap_ref, bp_ref, as_ref, bs_ref, w_ref, r_ref, o_ref, *, n_prompt_tiles):
    i = pl.program_id(0)
    half = ap_ref.shape[1]

    def run(a_ref, b_ref):
        o_ref[...] = (r_ref[...] + jnp.dot(a_ref[...], w_ref[:half, :], preferred_element_type=f32)
                      + jnp.dot(b_ref[...], w_ref[half:, :], preferred_element_type=f32))

    @pl.when(i < n_prompt_tiles)
    def _():
        run(ap_ref, bp_ref)

    @pl.when(i >= n_prompt_tiles)
    def _():
        run(as_ref, bs_ref)


def _out_proj(prompt_parts, sample_parts, w, res, *, tm=512, tn=512):
    (a_p, b_p), (a_s, b_s) = prompt_parts, sample_parts
    m, n = res.shape
    half = a_p.shape[1]
    npt = a_p.shape[0] // tm
    assert a_p.shape[0] % tm == 0 and a_s.shape[0] % tm == 0 and m == a_p.shape[0] + a_s.shape[0]
    pspec = pl.BlockSpec((tm, half), lambda i, j: (jnp.minimum(i, npt - 1), 0))
    sspec = pl.BlockSpec((tm, half), lambda i, j: (jnp.maximum(i - npt, 0), 0))
    return pl.pallas_call(
        functools.partial(_outproj_body, n_prompt_tiles=npt),
        out_shape=jax.ShapeDtypeStruct((m, n), f32),
        grid=(m // tm, n // tn),
        in_specs=[pspec, pspec, sspec, sspec,
                  pl.BlockSpec((2 * half, tn), lambda i, j: (0, j)),
                  pl.BlockSpec((tm, tn), lambda i, j: (i, j))],
        out_specs=pl.BlockSpec((tm, tn), lambda i, j: (i, j)),
        compiler_params=_cparams(2),
        name="out_proj",
    )(a_p, b_p, a_s, b_s, w, res)


def _tri_masks(c):
    row = _iota2((c, c), 0)
    col = _iota2((c, c), 1)
    return row, col


def _gdn_body(*refs, c_in, n_chunks, levels, has_state, first_chunk_zero_prev):
    if has_state:
        (u_ref, p8_ref, gate_ref, sm_ref, cw_ref, alog_ref, dtb_ref, ng_ref, s0_ref, o_ref, so_ref, s_scr) = refs
    else:
        (u_ref, p8_ref, gate_ref, sm_ref, cw_ref, alog_ref, dtb_ref, ng_ref, o_ref, so_ref, s_scr) = refs
    c = pl.program_id(1)
    C = CHUNK

    @pl.when(c == 0)
    def _():
        if has_state:
            s_scr[...] = s0_ref[...]
        else:
            s_scr[...] = jnp.zeros_like(s_scr)

    u = u_ref[...]
    p8 = p8_ref[...]
    if first_chunk_zero_prev:
        p8 = jnp.where(c == 0, 0.0, p8)
    cw = cw_ref[...]
    row8 = _iota2((8, u.shape[1]), 0)
    acc = u * cw[3:4]
    for k in range(1, 4):
        rolled = pltpu.roll(u, k, axis=0)
        first8 = jnp.where(row8 < k, pltpu.roll(p8, k, axis=0), rolled[0:8])
        sh = first8 if c_in == 8 else jnp.concatenate([first8, rolled[8:]], axis=0)
        acc = acc + sh * cw[3 - k:4 - k]
    qkv = acc * jax.nn.sigmoid(acc)

    sm = sm_ref[...]
    gfull = _pad_rows(-jnp.exp(alog_ref[...]) * _softplus(sm + dtb_ref[...]), C)
    beta = _pad_rows(jax.nn.sigmoid(sm), C)
    row, col = _tri_masks(C)
    causal = col <= row
    strict = col < row
    eye = jnp.where(row == col, 1.0, 0.0).astype(f32)
    gc = _mm_exact_a(jnp.where(causal, 1.0, 0.0), gfull)
    gate = gate_ref[...]
    ng = ng_ref[...]

    heads = range(NH)
    qn, kn, vb, kbe, decay, gch, a_mats, qk = [], [], [], [], [], [], [], []
    for h in heads:
        q = _pad_rows(qkv[:, h * HD:(h + 1) * HD], C)
        k = _pad_rows(qkv[:, (NH + h) * HD:(NH + h + 1) * HD], C)
        v = _pad_rows(qkv[:, (2 * NH + h) * HD:(2 * NH + h + 1) * HD], C)
        qn.append(q * lax.rsqrt(jnp.sum(q * q, axis=-1, keepdims=True) + 1e-6) * (HD ** -0.5))
        kn.append(k * lax.rsqrt(jnp.sum(k * k, axis=-1, keepdims=True) + 1e-6))
        gch.append(gc[:, h:h + 1])
        bh = beta[:, NH + h:NH + h + 1]
        gcb = jnp.broadcast_to(gch[h], (C, C))
        decay.append(jnp.where(causal, jnp.exp(jnp.where(causal, gcb - jnp.transpose(gcb), 0.0)), 0.0))
        kb = kn[h] * bh
        vb.append(v * bh)
        kbe.append(kb * jnp.exp(gch[h]))
        kq = _mm(jnp.concatenate([kb, qn[h]], axis=0), kn[h], NT)
        a_mats.append(jnp.where(strict, kq[:C] * decay[h], 0.0))
        qk.append(kq[C:] * decay[h])
    x_inv = _tri_inverse(a_mats, eye, levels)
    sol = [_mm_split(x_inv[h], [jnp.concatenate([vb[h], kbe[h]], axis=1)])[0] for h in heads]
    from_s = [_mm(jnp.concatenate([sol[h][:, HD:], qn[h] * jnp.exp(gch[h])], axis=0), s_scr[h]) for h in heads]
    uu = [sol[h][:, :HD] - from_s[h][:C] for h in heads]
    g_end = [gch[h][C - 1:C, :] for h in heads]
    from_u = [_mm(jnp.concatenate([qk[h], jnp.transpose(kn[h] * jnp.exp(g_end[h] - gch[h]))], axis=0), uu[h])
              for h in heads]
    for h in heads:
        s_scr[h] = s_scr[h] * jnp.exp(g_end[h]) + from_u[h][C:]
        gt = gate[:, h * HD:(h + 1) * HD]
        y = _rms((from_s[h][C:] + from_u[h][:C])[:c_in], ng) * (gt * jax.nn.sigmoid(gt))
        o_ref[:, h * HD:(h + 1) * HD] = y.astype(bf16)

    @pl.when(c == n_chunks - 1)
    def _():
        so_ref[...] = s_scr[...]


def _gdn(z, row_off, nb, t, conv_w, a_log, dt_bias, norm_g, s0=None, prev8=None):
    c_in = min(t, CHUNK)
    n_chunks = t // c_in
    has_state = s0 is not None
    levels = max(1, int(math.log2(c_in)))
    rb = row_off // c_in
    w3 = 3 * NH * HD
    if prev8 is None:
        prev_arr = z
        prev_spec = pl.BlockSpec((8, w3), lambda b, c: (jnp.maximum((row_off + b * t + c * c_in) // 8 - 1, 0), 0))
    else:
        prev_arr = prev8
        prev_spec = pl.BlockSpec((None, 8, w3), lambda b, c: (b, 0, 0))
    pad128 = lambda v: jnp.zeros((1, HD), f32).at[0, :NH].set(v.astype(f32))
    in_specs = [
        pl.BlockSpec((c_in, w3), lambda b, c: (rb + b * n_chunks + c, 0)),
        prev_spec,
        pl.BlockSpec((c_in, NH * HD), lambda b, c: (rb + b * n_chunks + c, 3)),
        pl.BlockSpec((c_in, HD), lambda b, c: (rb + b * n_chunks + c, 64)),
        pl.BlockSpec((4, w3), lambda b, c: (0, 0)),
        pl.BlockSpec((1, HD), lambda b, c: (0, 0)),
        pl.BlockSpec((1, HD), lambda b, c: (0, 0)),
        pl.BlockSpec((1, HD), lambda b, c: (0, 0)),
    ]
    args = [z, prev_arr, z, z, conv_w, pad128(a_log), pad128(dt_bias), norm_g.reshape(1, HD)]
    if has_state:
        in_specs.append(pl.BlockSpec((None, NH, HD, HD), lambda b, c: (b, 0, 0, 0)))
        args.append(s0)
    return pl.pallas_call(
        functools.partial(_gdn_body, c_in=c_in, n_chunks=n_chunks, levels=levels, has_state=has_state,
                          first_chunk_zero_prev=prev8 is None),
        out_shape=(jax.ShapeDtypeStruct((nb * t, NH * HD), bf16), jax.ShapeDtypeStruct((nb, NH, HD, HD), f32)),
        grid=(nb, n_chunks),
        in_specs=in_specs,
        out_specs=(pl.BlockSpec((c_in, NH * HD), lambda b, c: (b * n_chunks + c, 0)),
                   pl.BlockSpec((None, NH, HD, HD), lambda b, c: (b, 0, 0, 0))),
        scratch_shapes=[pltpu.VMEM((NH, HD, HD), f32)],
        compiler_params=_cparams(2),
        name="gdn",
    )(*args)


def _mlstm_body(*refs, c_in, n_chunks, has_state):
    if has_state:
        (q_ref, k_ref, v_ref, og_ref, sm_ref, ib_ref, fb_ref, ng_ref, c0_ref, n0_ref, m0_ref,
         o_ref, co_ref, no_ref, mo_ref, c_scr, n_scr, m_scr) = refs
    else:
        (q_ref, k_ref, v_ref, og_ref, sm_ref, ib_ref, fb_ref, ng_ref,
         o_ref, co_ref, no_ref, mo_ref, c_scr, n_scr, m_scr) = refs
    c = pl.program_id(1)
    C = CHUNK

    @pl.when(c == 0)
    def _():
        if has_state:
            c_scr[...] = c0_ref[...]
            n_scr[...] = n0_ref[...]
            m_scr[...] = m0_ref[...]
        else:
            c_scr[...] = jnp.zeros_like(c_scr)
            n_scr[...] = jnp.zeros_like(n_scr)
            m_scr[...] = jnp.zeros_like(m_scr)

    sm = sm_ref[...]
    log_i = _pad_rows(sm + ib_ref[...], C, NEG)
    log_f = _pad_rows(_log_sigmoid(sm + fb_ref[...]), C)
    row, col = _tri_masks(C)
    causal = col <= row
    bcum = _mm_exact_a(jnp.where(causal, 1.0, 0.0), log_f)
    ng = ng_ref[...]
    qa, ka, va, oga = q_ref[...], k_ref[...], v_ref[...], og_ref[...]

    heads = range(NH)
    hsl = [slice(h * HD, (h + 1) * HD) for h in heads]
    q = [_pad_rows(qa[:, hsl[h]], C) for h in heads]
    k = [_pad_rows(ka[:, hsl[h]], C) * (HD ** -0.5) for h in heads]
    v = [_pad_rows(va[:, hsl[h]], C) for h in heads]
    qk = [_mm(q[h], k[h], NT) for h in heads]
    inter = [_mm(q[h], c_scr[h]) for h in heads]
    m, w_inter, sc, a, b, ms = [], [], [], [], [], []
    for h in heads:
        b.append(bcum[:, 3 * NH + h:3 * NH + h + 1])
        a.append(log_i[:, 2 * NH + h:2 * NH + h + 1] - b[h])
        arow = jnp.transpose(jnp.broadcast_to(a[h], (C, C)))
        cm = jnp.max(jnp.where(causal, arow, -jnp.inf), axis=-1, keepdims=True)
        ms.append(m_scr[h:h + 1, 0:1])
        m.append(jnp.maximum(b[h] + ms[h], b[h] + cm))
        w_inter.append(jnp.exp(b[h] + ms[h] - m[h]))
        sc.append(qk[h] * jnp.where(causal, jnp.exp(jnp.where(causal, b[h] + arow - m[h], 0.0)), 0.0))
    kw, d_end = [], []
    for h in heads:
        m_end = m[h][C - 1:C, :]
        b_end = b[h][C - 1:C, :]
        kw.append(k[h] * jnp.exp(b_end + a[h] - m_end))
        d_end.append(jnp.exp(b_end + ms[h] - m_end))
    from_v = [_mm(jnp.concatenate([sc[h], jnp.transpose(kw[h])], axis=0), v[h]) for h in heads]
    for h in heads:
        ns = n_scr[h:h + 1, :]
        num = w_inter[h] * inter[h] + from_v[h][:C]
        den = w_inter[h] * jnp.sum(q[h] * ns, axis=-1, keepdims=True) + jnp.sum(sc[h], axis=-1, keepdims=True)
        hc = num / (jnp.maximum(jnp.abs(den), jnp.exp(-m[h])) + MLSTM_EPS)
        c_scr[h] = d_end[h] * c_scr[h] + from_v[h][C:]
        n_scr[h:h + 1, :] = d_end[h] * ns + jnp.sum(kw[h], axis=0, keepdims=True)
        m_scr[h:h + 1, :] = jnp.broadcast_to(m[h][C - 1:C, :], (1, HD))
        o_ref[:, hsl[h]] = (_rms(hc[:c_in], ng) * jax.nn.sigmoid(oga[:, hsl[h]])).astype(bf16)

    @pl.when(c == n_chunks - 1)
    def _():
        co_ref[...] = c_scr[...]
        no_ref[...] = n_scr[...]
        mo_ref[...] = m_scr[...]


def _mlstm(z, row_off, nb, t, gate_bias, norm_g, c0=None, n0=None, m0=None):
    c_in = min(t, CHUNK)
    n_chunks = t // c_in
    has_state = c0 is not None
    rb = row_off // c_in
    w = NH * HD
    ib = jnp.zeros((1, HD), f32).at[0, 2 * NH:3 * NH].set(gate_bias[0].astype(f32))
    fb = jnp.zeros((1, HD), f32).at[0, 3 * NH:4 * NH].set(gate_bias[1].astype(f32))
    zspec = lambda j: pl.BlockSpec((c_in, w), lambda b, c: (rb + b * n_chunks + c, j))
    in_specs = [zspec(4), zspec(5), zspec(6), zspec(7),
                pl.BlockSpec((c_in, HD), lambda b, c: (rb + b * n_chunks + c, 64)),
                pl.BlockSpec((1, HD), lambda b, c: (0, 0)),
                pl.BlockSpec((1, HD), lambda b, c: (0, 0)),
                pl.BlockSpec((1, HD), lambda b, c: (0, 0))]
    args = [z, z, z, z, z, ib, fb, norm_g.reshape(1, HD)]
    if has_state:
        in_specs += [pl.BlockSpec((None, NH, HD, HD), lambda b, c: (b, 0, 0, 0)),
                     pl.BlockSpec((None, NH, HD), lambda b, c: (b, 0, 0)),
                     pl.BlockSpec((None, NH, HD), lambda b, c: (b, 0, 0))]
        args += [c0, n0, jnp.broadcast_to(m0[..., None], m0.shape + (HD,))]
    return pl.pallas_call(
        functools.partial(_mlstm_body, c_in=c_in, n_chunks=n_chunks, has_state=has_state),
        out_shape=(jax.ShapeDtypeStruct((nb * t, w), bf16), jax.ShapeDtypeStruct((nb, NH, HD, HD), f32),
                   jax.ShapeDtypeStruct((nb, NH, HD), f32), jax.ShapeDtypeStruct((nb, NH, HD), f32)),
        grid=(nb, n_chunks),
        in_specs=in_specs,
        out_specs=(pl.BlockSpec((c_in, w), lambda b, c: (b * n_chunks + c, 0)),
                   pl.BlockSpec((None, NH, HD, HD), lambda b, c: (b, 0, 0, 0)),
                   pl.BlockSpec((None, NH, HD), lambda b, c: (b, 0, 0)),
                   pl.BlockSpec((None, NH, HD), lambda b, c: (b, 0, 0))),
        scratch_shapes=[pltpu.VMEM((NH, HD, HD), f32), pltpu.VMEM((NH, HD), f32), pltpu.VMEM((NH, HD), f32)],
        compiler_params=_cparams(2),
        name="mlstm",
    )(*args)


def _block_masks(c, t):
    row, col = _tri_masks(c)
    same = (row // t) == (col // t)
    return row, col, same


def _gdns_body(u_ref, p_ref, gate_ref, sm_ref, cw_ref, alog_ref, dtb_ref, ng_ref, s0_ref, o_ref, so_ref, *, t, levels):
    C = CHUNK
    nseq = C // t
    u = u_ref[...]
    p = p_ref[...]
    cw = cw_ref[...]
    rowm = _iota2(u.shape, 0) % t
    acc = u * cw[3:4]
    for k in range(1, 4):
        sh = jnp.where(rowm < k, pltpu.roll(p, (k - t) % C, axis=0), pltpu.roll(u, k, axis=0))
        acc = acc + sh * cw[3 - k:4 - k]
    qkv = acc * jax.nn.sigmoid(acc)

    sm = sm_ref[...]
    gfull = -jnp.exp(alog_ref[...]) * _softplus(sm + dtb_ref[...])
    beta = jax.nn.sigmoid(sm)
    row, col, same = _block_masks(C, t)
    causal = (col <= row) & same
    strict = (col < row) & same
    eye = jnp.where(row == col, 1.0, 0.0).astype(f32)
    gc = _mm_exact_a(jnp.where(causal, 1.0, 0.0), gfull)
    gsum = _mm_exact_a(jnp.where(same, 1.0, 0.0), gfull)
    colseq = _iota2((HD, C), 1) // t
    gate = gate_ref[...]
    ng = ng_ref[...]

    for h in range(NH):
        q = qkv[:, h * HD:(h + 1) * HD]
        k = qkv[:, (NH + h) * HD:(NH + h + 1) * HD]
        v = qkv[:, (2 * NH + h) * HD:(2 * NH + h + 1) * HD]
        qn = q * lax.rsqrt(jnp.sum(q * q, axis=-1, keepdims=True) + 1e-6) * (HD ** -0.5)
        kn = k * lax.rsqrt(jnp.sum(k * k, axis=-1, keepdims=True) + 1e-6)
        gch = gc[:, h:h + 1]
        gend = gsum[:, h:h + 1]
        bh = beta[:, NH + h:NH + h + 1]
        gcb = jnp.broadcast_to(gch, (C, C))
        decay = jnp.where(causal, jnp.exp(jnp.where(causal, gcb - jnp.transpose(gcb), 0.0)), 0.0)
        eg = jnp.exp(gch)
        kb = kn * bh
        kq = _mm(jnp.concatenate([kb, qn], axis=0), kn, NT)
        a_mat = jnp.where(strict, kq[:C] * decay, 0.0)
        (x_inv,) = _tri_inverse([a_mat], eye, levels)
        (sol,) = _mm_split(x_inv, [jnp.concatenate([v * bh, kb * eg], axis=1)])
        qe = qn * eg
        u_parts, o_parts = [], []
        for b in range(nseq):
            rs = slice(b * t, (b + 1) * t)
            r2 = _mm(jnp.concatenate([sol[rs, HD:], qe[rs]], axis=0), s0_ref[b, h])
            u_parts.append(sol[rs, :HD] - r2[:t])
            o_parts.append(r2[t:])
        uu = jnp.concatenate(u_parts, axis=0)
        o = jnp.concatenate(o_parts, axis=0) + _mm(kq[C:] * decay, uu)
        kdec_t = jnp.transpose(kn * jnp.exp(gend - gch))
        for b in range(nseq):
            so_ref[b, h] = (s0_ref[b, h] * jnp.exp(gend[b * t:b * t + 1, :])
                            + _mm(jnp.where(colseq == b, kdec_t, 0.0), uu))
        gt = gate[:, h * HD:(h + 1) * HD]
        o_ref[:, h * HD:(h + 1) * HD] = (_rms(o, ng) * (gt * jax.nn.sigmoid(gt))).astype(bf16)


def _gdn_decode(z, row_off, nb, t, conv_w, a_log, dt_bias, norm_g, s0, prev):
    nseq = CHUNK // t
    assert t == 8 and nb % nseq == 0 and row_off % CHUNK == 0
    rb = row_off // CHUNK
    w3 = 3 * NH * HD
    pad128 = lambda v: jnp.zeros((1, HD), f32).at[0, :NH].set(v.astype(f32))
    return pl.pallas_call(
        functools.partial(_gdns_body, t=t, levels=int(math.log2(t))),
        out_shape=(jax.ShapeDtypeStruct((nb * t, NH * HD), bf16), jax.ShapeDtypeStruct((nb, NH, HD, HD), f32)),
        grid=(nb // nseq,),
        in_specs=[pl.BlockSpec((CHUNK, w3), lambda i: (rb + i, 0)),
                  pl.BlockSpec((CHUNK, w3), lambda i: (i, 0)),
                  pl.BlockSpec((CHUNK, NH * HD), lambda i: (rb + i, 3)),
                  pl.BlockSpec((CHUNK, HD), lambda i: (rb + i, 64)),
                  pl.BlockSpec((4, w3), lambda i: (0, 0)),
                  pl.BlockSpec((1, HD), lambda i: (0, 0)),
                  pl.BlockSpec((1, HD), lambda i: (0, 0)),
                  pl.BlockSpec((1, HD), lambda i: (0, 0)),
                  pl.BlockSpec((nseq, NH, HD, HD), lambda i: (i, 0, 0, 0))],
        out_specs=(pl.BlockSpec((CHUNK, NH * HD), lambda i: (i, 0)),
                   pl.BlockSpec((nseq, NH, HD, HD), lambda i: (i, 0, 0, 0))),
        compiler_params=_cparams(1),
        name="gdn_decode",
    )(z, prev.reshape(nb * t, w3), z, z, conv_w, pad128(a_log), pad128(dt_bias), norm_g.reshape(1, HD), s0)


def _mlstms_body(q_ref, k_ref, v_ref, og_ref, sm_ref, ib_ref, fb_ref, ng_ref, c0_ref, n0_ref, m0_ref,
                 o_ref, co_ref, no_ref, mo_ref, *, t):
    C = CHUNK
    nseq = C // t
    sm = sm_ref[...]
    log_i = sm + ib_ref[...]
    log_f = _log_sigmoid(sm + fb_ref[...])
    m0r = m0_ref[...]
    row, col, same = _block_masks(C, t)
    causal = (col <= row) & same
    bcum = _mm_exact_a(jnp.where(causal, 1.0, 0.0), log_f)
    last_of_row = jnp.where(col == (row // t) * t + (t - 1), 1.0, 0.0)
    last_of_seq = jnp.where(_iota2((nseq, C), 1) == _iota2((nseq, C), 0) * t + (t - 1), 1.0, 0.0)
    seq_rows = jnp.where(_iota2((nseq, C), 1) // t == _iota2((nseq, C), 0), 1.0, 0.0)
    own_seq = _iota2((C, nseq), 1) == _iota2((C, nseq), 0) // t
    colseq = _iota2((HD, C), 1) // t
    lane = _iota2((C, HD), 1)
    ng = ng_ref[...]
    qa, ka, va, oga = q_ref[...], k_ref[...], v_ref[...], og_ref[...]

    for h in range(NH):
        sl = slice(h * HD, (h + 1) * HD)
        q = qa[:, sl]
        k = ka[:, sl] * (HD ** -0.5)
        v = va[:, sl]
        b = bcum[:, 3 * NH + h:3 * NH + h + 1]
        a = log_i[:, 2 * NH + h:2 * NH + h + 1] - b
        arow = jnp.transpose(jnp.broadcast_to(a, (C, C)))
        cm = jnp.max(jnp.where(causal, arow, -jnp.inf), axis=-1, keepdims=True)
        ms = m0r[:, h:h + 1]
        m = jnp.maximum(b + ms, b + cm)
        w_inter = jnp.exp(b + ms - m)
        dmat = jnp.where(causal, jnp.exp(jnp.where(causal, b + arow - m, 0.0)), 0.0)
        sc = _mm(q, k, NT) * dmat
        ends = _mm_exact_a(last_of_row, jnp.where(lane == 0, b, jnp.where(lane == 1, m, 0.0)))
        b_end, m_end = ends[:, 0:1], ends[:, 1:2]
        w_end = jnp.exp(b_end + a - m_end)
        d_end = jnp.exp(b_end + ms - m_end)
        kw = k * w_end
        kw_t = jnp.transpose(kw)
        nh = n0_ref[h]
        qn = jnp.sum(jnp.where(own_seq, _mm(q, nh, NT), 0.0), axis=-1, keepdims=True)
        inter = jnp.concatenate([_mm(q[b_ * t:(b_ + 1) * t], c0_ref[b_, h]) for b_ in range(nseq)], axis=0)
        num = w_inter * inter + _mm(sc, v)
        den = w_inter * qn + jnp.sum(sc, axis=-1, keepdims=True)
        hc = num / (jnp.maximum(jnp.abs(den), jnp.exp(-m)) + MLSTM_EPS)
        for b_ in range(nseq):
            co_ref[b_, h] = d_end[b_ * t:b_ * t + 1, :] * c0_ref[b_, h] + _mm(jnp.where(colseq == b_, kw_t, 0.0), v)
        per_seq = _mm_exact_a(last_of_seq, jnp.where(lane == 0, d_end, jnp.where(lane == 1, m_end, 0.0)))
        no_ref[h] = per_seq[:, 0:1] * nh + _mm_exact_a(seq_rows, kw)
        mo_ref[h] = jnp.broadcast_to(per_seq[:, 1:2], (nseq, HD))
        o_ref[:, sl] = (_rms(hc, ng) * jax.nn.sigmoid(oga[:, sl])).astype(bf16)


def _mlstm_decode(z, row_off, nb, t, gate_bias, norm_g, c0, n0, m0):
    nseq = CHUNK // t
    assert nb % nseq == 0 and row_off % CHUNK == 0
    rb = row_off // CHUNK
    w = NH * HD
    ib = jnp.zeros((1, HD), f32).at[0, 2 * NH:3 * NH].set(gate_bias[0].astype(f32))
    fb = jnp.zeros((1, HD), f32).at[0, 3 * NH:4 * NH].set(gate_bias[1].astype(f32))
    m_rows = jnp.zeros((nb, t, HD), f32).at[:, :, :NH].set(jnp.broadcast_to(m0[:, None, :], (nb, t, NH))).reshape(nb * t, HD)
    zspec = lambda j: pl.BlockSpec((CHUNK, w), lambda i: (rb + i, j))
    hspec = pl.BlockSpec((NH, nseq, HD), lambda i: (0, i, 0))
    out, c_new, n_new, m_new = pl.pallas_call(
        functools.partial(_mlstms_body, t=t),
        out_shape=(jax.ShapeDtypeStruct((nb * t, w), bf16), jax.ShapeDtypeStruct((nb, NH, HD, HD), f32),
                   jax.ShapeDtypeStruct((NH, nb, HD), f32), jax.ShapeDtypeStruct((NH, nb, HD), f32)),
        grid=(nb // nseq,),
        in_specs=[zspec(4), zspec(5), zspec(6), zspec(7),
                  pl.BlockSpec((CHUNK, HD), lambda i: (rb + i, 64)),
                  pl.BlockSpec((1, HD), lambda i: (0, 0)),
                  pl.BlockSpec((1, HD), lambda i: (0, 0)),
                  pl.BlockSpec((1, HD), lambda i: (0, 0)),
                  pl.BlockSpec((nseq, NH, HD, HD), lambda i: (i, 0, 0, 0)),
                  hspec,
                  pl.BlockSpec((CHUNK, HD), lambda i: (i, 0))],
        out_specs=(pl.BlockSpec((CHUNK, w), lambda i: (i, 0)),
                   pl.BlockSpec((nseq, NH, HD, HD), lambda i: (i, 0, 0, 0)), hspec, hspec),
        compiler_params=_cparams(1),
        name="mlstm_decode",
    )(z, z, z, z, z, ib, fb, norm_g.reshape(1, HD), c0, jnp.swapaxes(n0, 0, 1), m_rows)
    return out, c_new, jnp.swapaxes(n_new, 0, 1), jnp.swapaxes(m_new, 0, 1)


def _even_perm():
    w = NH * HD
    offs = np.cumsum([0, 3 * w, NH, NH, w, w, w, w, NH, NH, w])
    seg = lambda i: np.arange(offs[i], offs[i + 1])
    return np.concatenate([seg(0), seg(3), seg(4), seg(5), seg(6), seg(9), seg(1), seg(2), seg(7), seg(8)])


SCALE = HD ** -0.5


def _kvnorm_body(sk_ref, wk_ref, g_ref, so_ref, wo_ref):
    g = g_ref[...]
    for src, dst, gi in ((sk_ref, so_ref, 1), (wk_ref, wo_ref, 2)):
        x = src[...]
        for j in range(NKV):
            dst[:, j * HD:(j + 1) * HD] = _rms(x[:, j * HD:(j + 1) * HD], g[gi:gi + 1])


def _kvnorm(z, k_g, *, tm=1024):
    m = z.shape[0]
    w = NKV * HD
    return pl.pallas_call(
        _kvnorm_body,
        out_shape=(jax.ShapeDtypeStruct((m, w), f32), jax.ShapeDtypeStruct((m, w), f32)),
        grid=(m // tm,),
        in_specs=[pl.BlockSpec((tm, w), lambda i: (i, 18)), pl.BlockSpec((tm, w), lambda i: (i, 20)),
                  pl.BlockSpec((3, HD), lambda i: (0, 0))],
        out_specs=(pl.BlockSpec((tm, w), lambda i: (i, 0)), pl.BlockSpec((tm, w), lambda i: (i, 0))),
        compiler_params=_cparams(1),
        name="kvnorm",
    )(z, z, k_g)


SBP_HEADS = 4
SBP_QB = 256


def _sb_block_multi(zs, mask, mstrict, rs):
    n = zs[0].shape[0]
    log_betas, log_rests, parts = [], [], []
    for z in zs:
        sp = jnp.log1p(jnp.exp(-jnp.abs(z)))
        log_betas.append(jnp.minimum(z, 0.0) - sp)
        lr = -(jnp.maximum(z, 0.0) + sp)
        if mask is not None:
            lr = jnp.where(mask, lr, 0.0)
        log_rests.append(lr)
        parts.extend(_split2(lr))
    after = jnp.dot(jnp.concatenate(parts, axis=0), mstrict, preferred_element_type=f32)
    atts, new_rs = [], []
    for h, (lb, lr, r) in enumerate(zip(log_betas, log_rests, rs)):
        att = jnp.exp(lb + after[2 * h * n:(2 * h + 1) * n] + after[(2 * h + 1) * n:(2 * h + 2) * n] + r)
        atts.append(att if mask is None else jnp.where(mask, att, 0.0))
        new_rs.append(r + jnp.sum(lr, axis=-1, keepdims=True))
    return atts, new_rs


def _sbp_body(q_ref, k_ref, v_ref, o_ref):
    i = pl.program_id(2)
    row = _iota2((SBP_QB, HD), 0)
    col = _iota2((SBP_QB, HD), 1)
    mstrict = jnp.where(_iota2((HD, HD), 0) > _iota2((HD, HD), 1), 1.0, 0.0).astype(bf16)
    qs = [q_ref[:, h * HD:(h + 1) * HD].astype(bf16) for h in range(SBP_HEADS)]
    n_kb = (i + 1) * (SBP_QB // HD)

    def block(j, carry, masked):
        off = pl.multiple_of(j * HD, HD)
        mask = ((col + j * HD) < (row + i * SBP_QB)) if masked else None
        zs = [_mm(qs[h], k_ref[pl.ds(off, HD), h * HD:(h + 1) * HD], NT) * SCALE for h in range(SBP_HEADS)]
        atts, rs = _sb_block_multi(zs, mask, mstrict, carry[SBP_HEADS:])
        accs = [carry[h] + _mm(atts[h], v_ref[pl.ds(off, HD), h * HD:(h + 1) * HD]) for h in range(SBP_HEADS)]
        return tuple(accs) + tuple(rs)

    n_diag = SBP_QB // HD
    carry = tuple([jnp.zeros((SBP_QB, HD), f32)] * SBP_HEADS + [jnp.zeros((SBP_QB, 1), f32)] * SBP_HEADS)
    for d in range(n_diag):
        carry = block(n_kb - 1 - d, carry, True)
    res = lax.fori_loop(0, n_kb - n_diag, lambda s, c: block(n_kb - n_diag - 1 - s, c, False), carry)
    for h in range(SBP_HEADS):
        o_ref[:, h * HD:(h + 1) * HD] = res[h].astype(bf16)


def _sb_prompt(z, nb, t):
    nq = t // SBP_QB
    w = SBP_HEADS * HD
    ng = NH // SBP_HEADS
    return pl.pallas_call(
        _sbp_body,
        out_shape=jax.ShapeDtypeStruct((nb * t, NH * HD), bf16),
        grid=(nb, ng, nq),
        in_specs=[pl.BlockSpec((SBP_QB, w), lambda b, h, i: (b * nq + i, h)),
                  pl.BlockSpec((t, w), lambda b, h, i: (b, ng + h)),
                  pl.BlockSpec((t, w), lambda b, h, i: (b, 2 * ng + h))],
        out_specs=pl.BlockSpec((SBP_QB, w), lambda b, h, i: (b * nq + i, h)),
        compiler_params=_cparams(3),
        name="sb_prompt",
    )(z, z, z)


SBS_PAGES = 8


def _head_rows(page_ref, h, n_heads):
    return page_ref[pl.ds(h, PAGE, stride=n_heads), :]


def _pages2d(cache):
    return cache.reshape(cache.shape[0], cache.shape[1] * cache.shape[2], cache.shape[3])


def _sbs_body(pt_ref, q_ref, kn_ref, vn_ref, *rest, n_steps, t):
    del pt_ref
    kp_refs, vp_refs = rest[:SBS_PAGES], rest[SBS_PAGES:2 * SBS_PAGES]
    o_ref, acc_scr, r_scr = rest[2 * SBS_PAGES:]
    s = pl.program_id(1)

    @pl.when(s == 0)
    def _():
        acc_scr[...] = jnp.zeros_like(acc_scr)
        r_scr[...] = jnp.zeros_like(r_scr)

    rows = NH * t
    row = _iota2((rows, HD), 0)
    col = _iota2((rows, HD), 1)
    mrow = _iota2((HD, HD), 0)
    mcol = _iota2((HD, HD), 1)
    mstrict = jnp.where(mrow > mcol, 1.0, 0.0).astype(bf16)
    qa = q_ref[...]

    def load():
        return r_scr[:, 0:1], [acc_scr[h * t:(h + 1) * t, :] for h in range(NH)]

    def store(r, accs):
        r_scr[...] = jnp.broadcast_to(r, r_scr.shape)
        for h in range(NH):
            acc_scr[h * t:(h + 1) * t, :] = accs[h]

    @pl.when(s == 0)
    def _():
        r, accs = load()
        z = jnp.concatenate([_mm(qa[:, h * HD:(h + 1) * HD], _pad_rows(kn_ref[:, h * HD:(h + 1) * HD], HD), NT)
                             for h in range(NH)], axis=0) * SCALE
        (att,), (r,) = _sb_block_multi([z], col < (row % t), mstrict, [r])
        store(r, [accs[h] + _mm(att[h * t:(h + 1) * t], _pad_rows(vn_ref[:, h * HD:(h + 1) * HD], HD))
                  for h in range(NH)])

    slots = range(SBS_PAGES - 1, -1, -1)
    r, accs = load()
    z = jnp.concatenate(
        [_mm(qa[:, h * HD:(h + 1) * HD],
             jnp.concatenate([_head_rows(kp_refs[jj], h, NH).astype(bf16) for jj in slots], axis=0), NT)
         for h in range(NH)], axis=0) * SCALE
    sp = jnp.log1p(jnp.exp(-jnp.abs(z)))
    log_beta = jnp.minimum(z, 0.0) - sp
    log_rest = -(jnp.maximum(z, 0.0) + sp)
    blocks = [log_rest[:, c * HD:(c + 1) * HD] for c in range(SBS_PAGES)]
    after = jnp.dot(jnp.concatenate([x for blk in blocks for x in _split2(blk)], axis=0), mstrict,
                    preferred_element_type=f32)
    atts = [None] * SBS_PAGES
    for c in range(SBS_PAGES - 1, -1, -1):
        aft = after[2 * c * rows:(2 * c + 1) * rows] + after[(2 * c + 1) * rows:(2 * c + 2) * rows]
        atts[c] = jnp.exp(log_beta[:, c * HD:(c + 1) * HD] + aft + r)
        r = r + jnp.sum(blocks[c], axis=-1, keepdims=True)
    att = jnp.concatenate(atts, axis=1)
    store(r, [accs[h] + _mm(att[h * t:(h + 1) * t],
                            jnp.concatenate([_head_rows(vp_refs[jj], h, NH).astype(bf16) for jj in slots], axis=0))
              for h in range(NH)])

    @pl.when(s == n_steps - 1)
    def _():
        for h in range(NH):
            o_ref[:, h * HD:(h + 1) * HD] = acc_scr[h * t:(h + 1) * t, :].astype(bf16)


def _sb_sample(z, row_off, nb, t, cache_k, cache_v, page_table):
    n_pages = page_table.shape[1]
    n_steps = n_pages // SBS_PAGES
    rb = row_off // t
    page = lambda jj: pl.BlockSpec((None, PAGE * NH, HD),
                                   lambda b, s, pt: (pt[b, n_pages - 1 - (s * SBS_PAGES + jj)], 0, 0))
    pages = [page(jj) for jj in range(SBS_PAGES)]
    grid_spec = pltpu.PrefetchScalarGridSpec(
        num_scalar_prefetch=1,
        grid=(nb, n_steps),
        in_specs=[pl.BlockSpec((t, NH * HD), lambda b, s, pt: (rb + b, 0)),
                  pl.BlockSpec((t, NH * HD), lambda b, s, pt: (rb + b, 1)),
                  pl.BlockSpec((t, NH * HD), lambda b, s, pt: (rb + b, 2))] + pages + pages,
        out_specs=pl.BlockSpec((t, NH * HD), lambda b, s, pt: (b, 0)),
        scratch_shapes=[pltpu.VMEM((NH * t, HD), f32), pltpu.VMEM((NH * t, HD), f32)],
    )
    return pl.pallas_call(
        functools.partial(_sbs_body, n_steps=n_steps, t=t),
        out_shape=jax.ShapeDtypeStruct((nb * t, NH * HD), bf16),
        grid_spec=grid_spec,
        compiler_params=_cparams(2),
        name="sb_sample",
    )(page_table, z, z, z, *([_pages2d(cache_k)] * SBS_PAGES), *([_pages2d(cache_v)] * SBS_PAGES))


def _t5_bucket(dist):
    n = jnp.maximum(dist, 0)
    exact = N_BUCKETS // 2
    nf = jnp.maximum(n, 1).astype(f32)
    large = exact + (jnp.log(nf / exact) / math.log(MAX_DISTANCE / exact) * (N_BUCKETS - exact)).astype(i32)
    return jnp.where(n < exact, n, jnp.minimum(large, N_BUCKETS - 1))


def _bias_of_bucket(bucket, rel_ref, h):
    vals = [rel_ref[k, h] for k in range(N_BUCKETS)]
    bit = 1
    while len(vals) > 1:
        on = (bucket & bit) != 0
        vals = [jnp.where(on, vals[2 * n + 1], vals[2 * n]) for n in range(len(vals) // 2)]
        bit *= 2
    return vals[0]


def _softmax_step(m, l, acc, logits, mask, v):
    lg = jnp.where(mask, logits, NEG)
    m_new = jnp.maximum(m, jnp.max(lg, axis=-1, keepdims=True))
    alpha = jnp.exp(m - m_new)
    e = jnp.where(mask, jnp.exp(lg - m_new), 0.0)
    return m_new, alpha * l + jnp.sum(e, axis=-1, keepdims=True), alpha * acc + _mm(e, v)


def _softmax_full(logits, mask):
    lg = jnp.where(mask, logits, NEG)
    e = jnp.where(mask, jnp.exp(lg - jnp.max(lg, axis=-1, keepdims=True)), 0.0)
    return e / jnp.maximum(jnp.sum(e, axis=-1, keepdims=True), 1e-30)


def _block_scores(imp_sel, cur, n_sel):
    rows = imp_sel.shape[0]
    imp_pad = jnp.concatenate([imp_sel, jnp.zeros((rows, HD - imp_sel.shape[1]), f32)], axis=1)
    blk = _iota2((rows, HD), 1)
    forced = jnp.where(blk == cur, 2.0 * FORCE_SCORE, jnp.where(blk == 0, FORCE_SCORE, -FORCE_SCORE))
    score = jnp.where((blk < cur) & (blk > 0), imp_pad, forced)
    return jnp.where(blk < n_sel, score, -jnp.inf), blk


def _select_blocks(imp_sel, cur, n_sel):
    score, blk = _block_scores(imp_sel, cur, n_sel)
    sel = jnp.zeros(score.shape, f32)
    for _ in range(min(TOP_N, n_sel)):
        mx = jnp.max(score, axis=-1, keepdims=True)
        idx = jnp.min(jnp.where(score == mx, blk, 1 << 30), axis=-1, keepdims=True)
        hit = blk == idx
        sel = jnp.where(hit, 1.0, sel)
        score = jnp.where(hit, -jnp.inf, score)
    return sel


def _select_blocks_by_rank(imp_sel, cur, n_sel):
    score, _ = _block_scores(imp_sel, cur, n_sel)
    rows = score.shape[0]
    score_t = jnp.transpose(_pad_rows(score, HD, -jnp.inf))
    j_idx = _iota2((HD, HD), 0)
    i_idx = _iota2((HD, HD), 1)
    out = []
    for r in range(rows):
        s_j = score_t[:, r:r + 1]
        s_i = score[r:r + 1, :]
        ahead = (s_j > s_i) | ((s_j == s_i) & (j_idx < i_idx))
        rank = jnp.sum(jnp.where(ahead, 1.0, 0.0), axis=0, keepdims=True)
        out.append(jnp.where(rank < min(TOP_N, n_sel), 1.0, 0.0))
    return jnp.concatenate(out, axis=0)


def _compress(get_rows, cw_ref, cb, ckg, n_groups_rows):
    del n_groups_rows
    xk = jnp.concatenate([get_rows(0, t).astype(bf16) for t in range(CMP_BLOCK)], axis=1)
    xv = jnp.concatenate([get_rows(1, t).astype(bf16) for t in range(CMP_BLOCK)], axis=1)
    return _rms(_mm(xk, cw_ref[0]) + cb[0:1], ckg), _mm(xv, cw_ref[1]) + cb[1:2]


def _cmp_order(n_cmp, shape, axis):
    c = _iota2(shape, axis)
    half = n_cmp // 2
    return 2 * (c % half) + c // half


def _nsap_body(q_ref, gt_ref, ck0_ref, ck1_ref, cv0_ref, cv1_ref, sk_ref, sv_ref, wk_ref, wv_ref, cw_ref, cb_ref, ckg_ref,
               qg_ref, rel_ref, o_ref, kc_scr, vc_scr, bias_scr, *, t):
    b = pl.program_id(0)
    i = pl.program_id(1)
    n_cmp = t // CMP_BLOCK
    half = n_cmp // 2
    n_sel = t // SEL_BLOCK
    QB = HD
    cmp_refs = ((ck0_ref, ck1_ref), (cv0_ref, cv1_ref))

    @pl.when(i == 0)
    def _():
        for g in range(NKV):
            def get_rows(kind, tt, g=g):
                ref = cmp_refs[kind][g]
                return jnp.concatenate([ref[pl.ds(tt, half, stride=2 * CMP_BLOCK), :],
                                        ref[pl.ds(CMP_BLOCK + tt, half, stride=2 * CMP_BLOCK), :]], axis=0)
            kc, vc = _compress(get_rows, cw_ref, cb_ref[...], ckg_ref[...], n_cmp)
            kc_scr[g] = kc
            vc_scr[g] = vc

    @pl.when((b == 0) & (i == 0))
    def _():
        r_ = _iota2((QB, QB), 0)
        c_ = _iota2((QB, QB), 1)
        for kk in range(3):
            bucket = _t5_bucket(r_ - c_ + QB * kk)
            for h in range(NH):
                bias_scr[kk, h] = _bias_of_bucket(bucket, rel_ref, h)

    qall = q_ref[...]
    qg = qg_ref[...]
    qs = [jnp.concatenate([_rms(qall[:, (g * NREP + r) * HD:(g * NREP + r + 1) * HD], qg) for r in range(NREP)], axis=0)
          for g in range(NKV)]
    rep = lambda x: jnp.concatenate([x] * NREP, axis=0)

    qpos_c = i * QB + _iota2((QB, n_cmp), 0)
    dist_c = qpos_c - (_cmp_order(n_cmp, (QB, n_cmp), 1) * CMP_BLOCK + CMP_BLOCK - 1)
    bucket_c = _t5_bucket(dist_c)
    mask_c = rep(dist_c >= 0)
    cur = (i * QB + _iota2((QB, 1), 0)) // SEL_BLOCK
    o_cmp, imps = [], []
    for g in range(NKV):
        bias = jnp.concatenate([_bias_of_bucket(bucket_c, rel_ref, g * NREP + r) for r in range(NREP)], axis=0)
        p = _softmax_full(_mm(qs[g], kc_scr[g], NT) * SCALE + bias, mask_c)
        o_cmp.append(_mm(p, vc_scr[g]))
        imp = p[0:QB] + p[QB:2 * QB] + p[2 * QB:3 * QB] + p[3 * QB:4 * QB]
        imps.append(imp[:, :half] + imp[:, half:])
    sel_all = _select_blocks(jnp.concatenate(imps, axis=0), jnp.concatenate([cur] * NKV, axis=0), n_sel)
    sels = [sel_all[g * QB:(g + 1) * QB] for g in range(NKV)]

    KP = 2 * QB
    krow = _iota2((QB, KP), 0)
    kcol = _iota2((QB, KP), 1)

    def init():
        return tuple(x for _ in range(NKV) for x in (jnp.full((NREP * QB, 1), NEG, f32), jnp.zeros((NREP * QB, 1), f32),
                                                     jnp.zeros((NREP * QB, HD), f32)))

    def attend(carry, p, k_ref, v_ref, mask_of):
        off = pl.multiple_of(p * KP, KP)
        dist = i * QB + krow - (p * KP + kcol)
        d0 = jnp.clip(i - 2 * p, 0, 2)
        d1 = jnp.clip(i - 2 * p - 1, 0, 2)
        out = []
        for g in range(NKV):
            m, l, acc = carry[3 * g:3 * g + 3]
            bias = jnp.concatenate([jnp.concatenate([bias_scr[d0, g * NREP + r], bias_scr[d1, g * NREP + r]], axis=1)
                                    for r in range(NREP)], axis=0)
            logits = _mm(qs[g], k_ref[pl.ds(off, KP), g * HD:(g + 1) * HD], NT) * SCALE + bias
            out.extend(_softmax_step(m, l, acc, logits, rep(mask_of(g, p, dist)), v_ref[pl.ds(off, KP), g * HD:(g + 1) * HD]))
        return tuple(out)

    def sel_mask(g, p, dist):
        er = _iota2((HD, KP), 0)
        ec = _iota2((HD, KP), 1)
        expand = jnp.where(er == (KP // SEL_BLOCK) * p + ec // SEL_BLOCK, 1.0, 0.0)
        return (_mm(sels[g], expand) > 0.5) & (dist >= 0)

    def win_mask(g, p, dist):
        return (dist >= 0) & (dist < WINDOW)

    last_pair = i // 2
    first_win = jnp.maximum(i - WINDOW // QB, 0) // 2
    c_sel = lax.fori_loop(0, last_pair + 1, lambda p, c: attend(c, p, sk_ref, sv_ref, sel_mask), init())
    c_win = lax.fori_loop(first_win, last_pair + 1, lambda p, c: attend(c, p, wk_ref, wv_ref, win_mask), init())

    gts = jax.nn.sigmoid(gt_ref[...])
    for h in range(NH):
        g, r = divmod(h, NREP)
        sl = slice(r * QB, (r + 1) * QB)
        o_s = c_sel[3 * g + 2][sl] / jnp.maximum(c_sel[3 * g + 1][sl], 1e-30)
        o_w = c_win[3 * g + 2][sl] / jnp.maximum(c_win[3 * g + 1][sl], 1e-30)
        o = gts[:, 3 * h:3 * h + 1] * o_cmp[g][sl] + gts[:, 3 * h + 1:3 * h + 2] * o_s + gts[:, 3 * h + 2:3 * h + 3] * o_w
        o_ref[:, h * HD:(h + 1) * HD] = o.astype(bf16)


def _nsa_prompt(z, sk, wk, nb, t, cmp_w, cmp_b, cmp_k_g, q_g, rel_bias):
    nq = t // HD
    n_cmp = t // CMP_BLOCK
    w = NKV * HD
    full = lambda j: pl.BlockSpec((t, w), lambda b, i: (b, j))
    head = lambda j: pl.BlockSpec((t, HD), lambda b, i: (b, j))
    return pl.pallas_call(
        functools.partial(_nsap_body, t=t),
        out_shape=jax.ShapeDtypeStruct((nb * t, NH * HD), bf16),
        grid=(nb, nq),
        in_specs=[pl.BlockSpec((HD, NH * HD), lambda b, i: (b * nq + i, 3)),
                  pl.BlockSpec((HD, HD), lambda b, i: (b * nq + i, 44)),
                  head(32), head(33), head(34), head(35), full(0), full(19), full(0), full(21),
                  pl.BlockSpec((2, CMP_BLOCK * HD, HD), lambda b, i: (0, 0, 0)),
                  pl.BlockSpec((2, HD), lambda b, i: (0, 0)),
                  pl.BlockSpec((1, HD), lambda b, i: (0, 0)),
                  pl.BlockSpec((1, HD), lambda b, i: (0, 0)),
                  pl.BlockSpec(memory_space=pltpu.SMEM)],
        out_specs=pl.BlockSpec((HD, NH * HD), lambda b, i: (b * nq + i, 0)),
        scratch_shapes=[pltpu.VMEM((NKV, n_cmp, HD), f32), pltpu.VMEM((NKV, n_cmp, HD), f32),
                        pltpu.VMEM((3, NH, HD, HD), f32)],
        compiler_params=_cparams(2),
        name="nsa_prompt",
    )(z, z, z, z, z, z, sk, z, wk, z, cmp_w.astype(bf16), cmp_b, cmp_k_g.reshape(1, HD), q_g.reshape(1, HD), rel_bias)


def _nsa_q_groups(q_ref, qg):
    qall = q_ref[...]
    return [jnp.concatenate([_rms(qall[:, (g * NREP + r) * HD:(g * NREP + r + 1) * HD], qg) for r in range(NREP)], axis=0)
            for g in range(NKV)]


CMP_PAGES = 8


def _nsasa_body(pt_ref, q_ref, *rest, n_pages, t):
    del pt_ref
    ckp_refs, cvp_refs = rest[:CMP_PAGES], rest[CMP_PAGES:2 * CMP_PAGES]
    cw_ref, cb_ref, ckg_ref, qg_ref, rel_ref, oc_ref, sel_ref, xk_scr, xv_scr = rest[2 * CMP_PAGES:]
    p = pl.program_id(1)
    pp = 2 * PAGE
    pr_ = _iota2((pp, pp), 0)
    pc_ = _iota2((pp, pp), 1)
    perm = jnp.where(pc_ == (pr_ % 8) * CMP_BLOCK + pr_ // 8, 1.0, 0.0).astype(bf16)
    for pair in range(CMP_PAGES // 2):
        row0 = pl.multiple_of((p * (CMP_PAGES // 2) + pair) * 8, 8)
        for refs, scr in ((ckp_refs, xk_scr), (cvp_refs, xv_scr)):
            for g in range(NKV):
                x2 = jnp.concatenate([_head_rows(refs[2 * pair], g, NKV), _head_rows(refs[2 * pair + 1], g, NKV)], axis=0)
                moved = jnp.dot(perm, x2.astype(bf16), preferred_element_type=f32)
                for tt in range(CMP_BLOCK):
                    scr[g, pl.ds(row0, 8), tt * HD:(tt + 1) * HD] = moved[8 * tt:8 * tt + 8, :]

    @pl.when(p == n_pages // CMP_PAGES - 1)
    def _():
        past = n_pages * PAGE
        n_cmp = (past + t) // CMP_BLOCK
        n_sel = -(-(past + t) // SEL_BLOCK)
        cb = cb_ref[...]
        kc = _rms(_mm(jnp.concatenate([xk_scr[g] for g in range(NKV)], axis=0), cw_ref[0]) + cb[0:1], ckg_ref[...])
        vc = _mm(jnp.concatenate([xv_scr[g] for g in range(NKV)], axis=0), cw_ref[1]) + cb[1:2]
        qs = _nsa_q_groups(q_ref, qg_ref[...])
        rows = NREP * t
        tq = _iota2((rows, n_cmp), 0) % t
        dist_c = past + tq - (_iota2((rows, n_cmp), 1) * CMP_BLOCK + CMP_BLOCK - 1)
        bucket_c = _t5_bucket(dist_c)
        mask_c = dist_c >= 0
        cur = (past + _iota2((t, 1), 0)) // SEL_BLOCK
        ratio = SEL_BLOCK // CMP_BLOCK
        pair_sum = jnp.where(_iota2((n_cmp, n_cmp // ratio), 0) // ratio == _iota2((n_cmp, n_cmp // ratio), 1), 1.0, 0.0)
        imps = []
        for g in range(NKV):
            bias = jnp.concatenate([_bias_of_bucket(bucket_c[r * t:(r + 1) * t], rel_ref, g * NREP + r)
                                    for r in range(NREP)], axis=0)
            pr = _softmax_full(_mm(qs[g], kc[g * n_cmp:(g + 1) * n_cmp], NT) * SCALE + bias, mask_c)
            oc_ref[g * rows:(g + 1) * rows, :] = _mm(pr, vc[g * n_cmp:(g + 1) * n_cmp])
            imps.append(pr[0:t] + pr[t:2 * t] + pr[2 * t:3 * t] + pr[3 * t:4 * t])
        imp_sel = _mm_exact_b(jnp.concatenate(imps, axis=0), pair_sum)
        sel_ref[...] = _select_blocks_by_rank(imp_sel, jnp.concatenate([cur] * NKV, axis=0), n_sel)


def _nsa_sample_cmp(z, row_off, nb, t, cache_ck, cache_cv, page_table, cmp_w, cmp_b, cmp_k_g, q_g, rel_bias):
    n_pages = page_table.shape[1]
    assert n_pages % CMP_PAGES == 0 and CMP_PAGES % 2 == 0 and t < CMP_BLOCK
    rb = row_off // t
    page = lambda jj: pl.BlockSpec((None, PAGE * NKV, HD), lambda b, p, pt: (pt[b, p * CMP_PAGES + jj], 0, 0))
    pages = [page(jj) for jj in range(CMP_PAGES)]
    cst = lambda *shape: pl.BlockSpec(shape, lambda b, p, pt: (0,) * len(shape))
    grid_spec = pltpu.PrefetchScalarGridSpec(
        num_scalar_prefetch=1,
        grid=(nb, n_pages // CMP_PAGES),
        in_specs=[pl.BlockSpec((t, NH * HD), lambda b, p, pt: (rb + b, 3))] + pages + pages + [
                  cst(2, CMP_BLOCK * HD, HD), cst(2, HD), cst(1, HD), cst(1, HD),
                  pl.BlockSpec(memory_space=pltpu.SMEM)],
        out_specs=(pl.BlockSpec((None, NH * t, HD), lambda b, p, pt: (b, 0, 0)),
                   pl.BlockSpec((None, NKV * t, HD), lambda b, p, pt: (b, 0, 0))),
        scratch_shapes=[pltpu.VMEM((NKV, n_pages * PAGE // CMP_BLOCK, CMP_BLOCK * HD), f32)] * 2,
    )
    return pl.pallas_call(
        functools.partial(_nsasa_body, n_pages=n_pages, t=t),
        out_shape=(jax.ShapeDtypeStruct((nb, NH * t, HD), f32), jax.ShapeDtypeStruct((nb, NKV * t, HD), f32)),
        grid_spec=grid_spec,
        compiler_params=_cparams(2),
        name="nsa_sample_cmp",
    )(page_table, z, *([_pages2d(cache_ck)] * CMP_PAGES), *([_pages2d(cache_cv)] * CMP_PAGES), cmp_w.astype(bf16), cmp_b,
      cmp_k_g.reshape(1, HD), q_g.reshape(1, HD), rel_bias)


def _nsa_decode_body(pt_ref, q_ref, gt_ref, oc_ref, sel_ref, skn_ref, svn_ref, wkn_ref, wvn_ref, wk_ref, wv_ref,
                     *rest, n_pages, t):
    del pt_ref
    skp_refs, svp_refs = rest[:n_pages], rest[n_pages:2 * n_pages]
    qg_ref, rel_ref, o_ref = rest[2 * n_pages:]
    past = n_pages * PAGE
    rows = NREP * t
    qs = _nsa_q_groups(q_ref, qg_ref[...])
    rep = lambda x: jnp.concatenate([x] * NREP, axis=0)

    def bias_of(dist, g):
        bucket = _t5_bucket(dist)
        return jnp.concatenate([_bias_of_bucket(bucket, rel_ref, g * NREP + r) for r in range(NREP)], axis=0)

    nwin = WINDOW // HD
    dist_w = WINDOW + _iota2((t, WINDOW + HD), 0) - _iota2((t, WINDOW + HD), 1)
    ok_w = (dist_w >= 0) & (dist_w < WINDOW)
    n_far = (n_pages - 1) * HD
    n_sel_keys = (n_pages + 1) * HD
    dist_near = PAGE + _iota2((t, 2 * HD), 0) - _iota2((t, 2 * HD), 1)
    ok_s = (past + _iota2((t, n_sel_keys), 0) - _iota2((t, n_sel_keys), 1)) >= 0
    expand = jnp.where(_iota2((HD, n_sel_keys), 0) == _iota2((HD, n_sel_keys), 1) // SEL_BLOCK, 1.0, 0.0)

    jobs = []
    for g in range(NKV):
        gs = slice(g * HD, (g + 1) * HD)
        ks = [wk_ref[pl.ds(jj * HD * NKV + g, HD, stride=NKV), :] for jj in range(nwin)] + [_pad_rows(wkn_ref[:, gs], HD)]
        vs = [wv_ref[pl.ds(jj * HD * NKV + g, HD, stride=NKV), :] for jj in range(nwin)] + [_pad_rows(wvn_ref[:, gs], HD)]
        jobs.append((g, ks, vs, bias_of(dist_w, g), ok_w))
    for g in range(NKV):
        gs = slice(g * HD, (g + 1) * HD)
        ks = [_head_rows(r, g, NKV) for r in skp_refs] + [_pad_rows(skn_ref[:, gs], HD)]
        vs = [_head_rows(r, g, NKV) for r in svp_refs] + [_pad_rows(svn_ref[:, gs], HD)]
        far = jnp.concatenate([jnp.full((t, 1), rel_ref[N_BUCKETS - 1, g * NREP + r], f32) for r in range(NREP)], axis=0)
        bias = jnp.concatenate([jnp.broadcast_to(far, (rows, n_far)), bias_of(dist_near, g)], axis=1)
        picked = _mm(sel_ref[g * t:(g + 1) * t, :], expand) > 0.5
        jobs.append((g, ks, vs, bias, picked & ok_s))

    logits = [jnp.concatenate([_mm(qs[g], k, NT) for k in ks], axis=1) * SCALE + bias for g, ks, _, bias, _ in jobs]
    probs = [_softmax_full(lg, rep(job[4])) for lg, job in zip(logits, jobs)]
    outs = []
    for p, job in zip(probs, jobs):
        acc = jnp.zeros((rows, HD), f32)
        for n, v in enumerate(job[2]):
            acc = acc + _mm(p[:, n * HD:(n + 1) * HD], v)
        outs.append(acc)

    gts = jax.nn.sigmoid(gt_ref[...])
    for h in range(NH):
        g, r = divmod(h, NREP)
        sl = slice(r * t, (r + 1) * t)
        o = (gts[:, 3 * h:3 * h + 1] * oc_ref[g * rows + r * t:g * rows + (r + 1) * t, :]
             + gts[:, 3 * h + 1:3 * h + 2] * outs[NKV + g][sl] + gts[:, 3 * h + 2:3 * h + 3] * outs[g][sl])
        o_ref[:, h * HD:(h + 1) * HD] = o.astype(bf16)


def _nsa_decode_attn(z, sk, wk, row_off, nb, t, o_cmp, sel, win_k, win_v, cache_sk, cache_sv, page_table, q_g, rel_bias):
    n_pages = page_table.shape[1]
    assert PAGE >= MAX_DISTANCE and t <= HD
    rb = row_off // t
    w = NKV * HD
    pages = [pl.BlockSpec((None, PAGE * NKV, HD), lambda b, pt, jj=jj: (pt[b, jj], 0, 0)) for jj in range(n_pages)]
    rowsp = lambda width, j: pl.BlockSpec((t, width), lambda b, pt: (rb + b, j))
    grid_spec = pltpu.PrefetchScalarGridSpec(
        num_scalar_prefetch=1,
        grid=(nb,),
        in_specs=[rowsp(NH * HD, 3), rowsp(HD, 44),
                  pl.BlockSpec((None, NH * t, HD), lambda b, pt: (b, 0, 0)),
                  pl.BlockSpec((None, NKV * t, HD), lambda b, pt: (b, 0, 0)),
                  rowsp(w, 0), rowsp(w, 19), rowsp(w, 0), rowsp(w, 21),
                  pl.BlockSpec((None, WINDOW * NKV, HD), lambda b, pt: (b, 0, 0)),
                  pl.BlockSpec((None, WINDOW * NKV, HD), lambda b, pt: (b, 0, 0))] + pages + pages + [
                  pl.BlockSpec((1, HD), lambda b, pt: (0, 0)),
                  pl.BlockSpec(memory_space=pltpu.SMEM)],
        out_specs=pl.BlockSpec((t, NH * HD), lambda b, pt: (b, 0)),
    )
    return pl.pallas_call(
        functools.partial(_nsa_decode_body, n_pages=n_pages, t=t),
        out_shape=jax.ShapeDtypeStruct((nb * t, NH * HD), bf16),
        grid_spec=grid_spec,
        compiler_params=_cparams(1),
        name="nsa_sample_attn",
    )(page_table, z, z, o_cmp, sel, sk, z, wk, z, _pages2d(win_k), _pages2d(win_v),
      *([_pages2d(cache_sk)] * n_pages), *([_pages2d(cache_sv)] * n_pages), q_g.reshape(1, HD), rel_bias)


def _even_mixer(z, mp, bp, tp, bs, ts, conv_w, a_log, dt_bias, gdn_g, gate_bias, mlstm_g, s0, conv0, c0, n0, m0):
    w3 = 3 * NH * HD
    go_p, gs_p = _gdn(z, 0, bp, tp, conv_w, a_log, dt_bias, gdn_g)
    prev8 = jnp.concatenate([jnp.zeros((bs, 8 - conv0.shape[1], w3), f32), conv0], axis=1)
    go_s, gs_s = _gdn_decode(z, mp, bs, ts, conv_w, a_log, dt_bias, gdn_g, s0, prev8)
    mh_p, mc_p, mn_p, mm_p = _mlstm(z, 0, bp, tp, gate_bias, mlstm_g)
    mh_s, mc_s, mn_s, mm_s = _mlstm_decode(z, mp, bs, ts, gate_bias, mlstm_g, c0, n0, m0)
    mix = ((go_p, mh_p), (go_s, mh_s))
    keep = conv0.shape[1]
    conv_p = jnp.stack([z[(b + 1) * tp - keep:(b + 1) * tp, :w3] for b in range(bp)])
    conv_s = z[mp:, :w3].reshape(bs, ts, w3)[:, ts - keep:]
    states = (gs_p, gs_s, conv_p, conv_s, mc_p, mc_s, mn_p, mn_s, mm_p[..., 0], mm_s[..., 0])
    return mix, states


def _odd_mixer(z, mp, bp, tp, bs, ts, page_table, caches, win_k0, win_v0, q_g, k_g, cmp_w, cmp_b, rel_bias):
    sb_k, sb_v, cmp_k, cmp_v, sel_k, sel_v = caches
    sk, wk = _kvnorm(z, k_g)
    sb_p = _sb_prompt(z, bp, tp)
    sb_s = _sb_sample(z, mp, bs, ts, sb_k, sb_v, page_table)
    ns_p = _nsa_prompt(z, sk, wk, bp, tp, cmp_w, cmp_b, k_g[0], q_g, rel_bias)
    o_cmp, sel = _nsa_sample_cmp(z, mp, bs, ts, cmp_k, cmp_v, page_table, cmp_w, cmp_b, k_g[0], q_g, rel_bias)
    ns_s = _nsa_decode_attn(z, sk, wk, mp, bs, ts, o_cmp, sel, win_k0, win_v0, sel_k, sel_v, page_table, q_g, rel_bias)
    mix = ((sb_p, ns_p), (sb_s, ns_s))

    w = NH * HD
    kw = NKV * HD

    def rows(arr, lo, width, heads):
        sl = arr[:, lo:lo + width]
        return sl[:mp].reshape(bp, tp, heads, HD), sl[mp:].reshape(bs, ts, heads, HD)

    sbk_p, sbk_s = rows(z, w, w, NH)
    sbv_p, sbv_s = rows(z, 2 * w, w, NH)
    ck_p, ck_s = rows(z, 4 * w, kw, NKV)
    cv_p, cv_s = rows(z, 4 * w + kw, kw, NKV)
    sk_p, sk_s = rows(sk, 0, kw, NKV)
    sv_p, sv_s = rows(z, 4 * w + 3 * kw, kw, NKV)
    wk_p, wk_s = rows(wk, 0, kw, NKV)
    wv_p, wv_s = rows(z, 4 * w + 5 * kw, kw, NKV)
    keep_p = min(WINDOW, tp)
    win = lambda old, new: jnp.concatenate([old, new], axis=1)[:, -min(WINDOW, old.shape[1] + ts):]
    states = (sbk_p, sbk_s, sbv_p, sbv_s, ck_p, ck_s, cv_p, cv_s, sk_p, sk_s, sv_p, sv_s,
              wk_p[:, tp - keep_p:], win(win_k0, wk_s), wv_p[:, tp - keep_p:], win(win_v0, wv_s))
    return mix, states


def kernel(x_prompt, x_sample, state_gdn_s, state_gdn_conv, state_mlstm_c, state_mlstm_n, state_mlstm_m, cache_sb_k, cache_sb_v, cache_nsa_cmp_k, cache_nsa_cmp_v, cache_nsa_sel_k, cache_nsa_sel_v, state_nsa_win_k, state_nsa_win_v, page_table, norm_g, ffn_w_gate, ffn_w_up, ffn_w_down, even_w_in, gdn_conv_w, gdn_a_log, gdn_dt_bias, gdn_norm_g, mlstm_gate_bias, mlstm_norm_g, even_w_out, odd_w_in, nsa_q_norm_g, nsa_k_norm_g, nsa_cmp_w, nsa_cmp_b, odd_w_out, rel_bias):
    bp, tp, d = x_prompt.shape
    bs, ts, _ = x_sample.shape
    mp = bp * tp
    xs = jnp.concatenate([x_prompt.reshape(mp, d), x_sample.reshape(bs * ts, d)], axis=0)
    depth = norm_g.shape[0]
    even_states, odd_states = [], []
    for layer in range(depth):
        j = layer // 2
        ffn = lambda x, n, i: _ffn(x, norm_g[layer, n], ffn_w_gate, ffn_w_up, ffn_w_down, layer, i)
        xs = ffn(xs, 0, 0)
        if layer % 2 == 0:
            w_in = even_w_in[j][:, _even_perm()]
            w_in = _pad_cols(w_in.astype(bf16), PROJ_TN)
            z = _rms_matmul(xs, norm_g[layer, 1], w_in)
            mix, st = _even_mixer(z, mp, bp, tp, bs, ts, gdn_conv_w[j], gdn_a_log[j], gdn_dt_bias[j], gdn_norm_g[j],
                                  mlstm_gate_bias[j], mlstm_norm_g[j], state_gdn_s[j], state_gdn_conv[j],
                                  state_mlstm_c[j], state_mlstm_n[j], state_mlstm_m[j])
            even_states.append(st)
            w_out = even_w_out[j]
        else:
            w_in = _pad_cols(odd_w_in[j].astype(bf16), PROJ_TN)
            z = _rms_matmul(xs, norm_g[layer, 1], w_in)
            caches = (cache_sb_k[j], cache_sb_v[j], cache_nsa_cmp_k[j], cache_nsa_cmp_v[j], cache_nsa_sel_k[j],
                      cache_nsa_sel_v[j])
            mix, st = _odd_mixer(z, mp, bp, tp, bs, ts, page_table, caches, state_nsa_win_k[j], state_nsa_win_v[j],
                                 nsa_q_norm_g[j], nsa_k_norm_g[j], nsa_cmp_w[j], nsa_cmp_b[j], rel_bias)
            odd_states.append(st)
            w_out = odd_w_out[j]
        xs = _out_proj(mix[0], mix[1], w_out.astype(bf16), xs)
        xs = ffn(xs, 2, 1)
    stack = lambda sts, i: jnp.stack([s[i] for s in sts])
    outs = [xs[:mp].reshape(bp, tp, d), xs[mp:].reshape(bs, ts, d)]
    outs += [stack(even_states, i) for i in range(10)]
    outs += [stack(odd_states, i) for i in range(16)]
    return tuple(outs)
```

```python
import functools
import math

import jax
import jax.numpy as jnp
import numpy as np
from jax import lax
from jax.experimental import pallas as pl
from jax.experimental.pallas import tpu as pltpu

f32 = jnp.float32
bf16 = jnp.bfloat16
i32 = jnp.int32

HD = 128
NH = 8
NKV = 2
NREP = 4
RMS_EPS = 1e-6
MLSTM_EPS = 1e-6
CHUNK = 128
PAGE = 128
CMP_BLOCK = 32
SEL_BLOCK = 64
TOP_N = 8
WINDOW = 512
FORCE_SCORE = 1.0e4
N_BUCKETS = 32
MAX_DISTANCE = 128
NEG = -1e30
VMEM_LIMIT = 56 * 1024 * 1024

NN = (((1,), (0,)), ((), ()))
NT = (((1,), (1,)), ((), ()))
TN = (((0,), (0,)), ((), ()))


FFN_VMEM_LIMIT = 60 * 1024 * 1024


def _cparams(n_axes, vmem_limit=None):
    return pltpu.CompilerParams(dimension_semantics=("arbitrary",) * n_axes,
                                vmem_limit_bytes=VMEM_LIMIT if vmem_limit is None else vmem_limit)


def _mm(a, b, dims=NN):
    return lax.dot_general(a.astype(bf16), b.astype(bf16), dims, preferred_element_type=f32)


def _split3(a):
    a0 = a.astype(bf16)
    r = a - a0.astype(f32)
    a1 = r.astype(bf16)
    a2 = (r - a1.astype(f32)).astype(bf16)
    return a0, a1, a2


def _mm_exact_b(a, b01, dims=NN):
    a0, a1, a2 = _split3(a)
    b01 = b01.astype(bf16)
    d = lambda x: lax.dot_general(x, b01, dims, preferred_element_type=f32)
    return d(a0) + d(a1) + d(a2)


def _mm_exact_a(a01, b, dims=NN):
    b0, b1, b2 = _split3(b)
    a01 = a01.astype(bf16)
    d = lambda x: lax.dot_general(a01, x, dims, preferred_element_type=f32)
    return d(b0) + d(b1) + d(b2)


def _split2(a):
    ah = a.astype(bf16)
    return ah, (a - ah.astype(f32)).astype(bf16)


def _mm_split(a, bs):
    n = a.shape[0]
    rhs = jnp.concatenate([x for b in bs for x in _split2(b)], axis=1)
    r = jnp.dot(jnp.concatenate(_split2(a), axis=0), rhs, preferred_element_type=f32)
    outs, off = [], 0
    for b in bs:
        w = b.shape[1]
        outs.append((r[:n, off:off + w] + r[:n, off + w:off + 2 * w]) + (r[n:, off:off + w] + r[n:, off + w:off + 2 * w]))
        off += 2 * w
    return outs


def _tri_inverse(a_mats, eye, levels):
    ys = [eye - a for a in a_mats]
    if levels < 2:
        return ys
    qs = [_mm_split(a, [a])[0] for a in a_mats]
    for lvl in range(levels - 1):
        last = lvl == levels - 2
        res = [_mm_split(q, [y] if last else [y, q]) for q, y in zip(qs, ys)]
        ys = [y + r[0] for y, r in zip(ys, res)]
        if not last:
            qs = [r[1] for r in res]
    return ys


def _softplus(x):
    return jnp.maximum(x, 0.0) + jnp.log1p(jnp.exp(-jnp.abs(x)))


def _log_sigmoid(x):
    return -_softplus(-x)


def _rms(x, g):
    return x * lax.rsqrt(jnp.mean(x * x, axis=-1, keepdims=True) + RMS_EPS) * g


def _pad_rows(x, rows, value=0.0):
    if x.shape[0] == rows:
        return x
    return jnp.concatenate([x, jnp.full((rows - x.shape[0],) + x.shape[1:], value, x.dtype)], axis=0)


def _iota2(shape, axis):
    return lax.broadcasted_iota(i32, shape, axis)


def _ffn_body(x_ref, g_ref, wg_ref, wu_ref, wd_ref, o_ref, h_ref, *, nf):
    f = pl.program_id(1)

    del nf

    @pl.when(f == 0)
    def _():
        x = x_ref[...]
        h_ref[...] = _rms(x, g_ref[...]).astype(bf16)
        o_ref[...] = x

    h = h_ref[...]
    a = jnp.dot(h, wg_ref[...].astype(bf16), preferred_element_type=f32)
    u = jnp.dot(h, wu_ref[...].astype(bf16), preferred_element_type=f32)
    act = (0.5 * a * jax.nn.sigmoid(a) * u).astype(bf16)
    o_ref[...] += jnp.dot(act, wd_ref[...].astype(bf16), preferred_element_type=f32)


def _ffn(x, g, wg, wu, wd, layer, idx, *, tm=1024, tf=512):
    m, d = x.shape
    fdim = wg.shape[-1]
    nf = fdim // tf
    once = pl.Buffered(1)
    return pl.pallas_call(
        functools.partial(_ffn_body, nf=nf),
        out_shape=jax.ShapeDtypeStruct((m, d), f32),
        grid=(m // tm, nf),
        in_specs=[
            pl.BlockSpec((tm, d), lambda i, f: (i, 0), pipeline_mode=once),
            pl.BlockSpec((1, d), lambda i, f: (0, 0)),
            pl.BlockSpec((None, None, d, tf), lambda i, f: (layer, idx, 0, f)),
            pl.BlockSpec((None, None, d, tf), lambda i, f: (layer, idx, 0, f)),
            pl.BlockSpec((None, None, tf, d), lambda i, f: (layer, idx, f, 0)),
        ],
        out_specs=pl.BlockSpec((tm, d), lambda i, f: (i, 0), pipeline_mode=once),
        scratch_shapes=[pltpu.VMEM((tm, d), bf16)],
        compiler_params=_cparams(2, FFN_VMEM_LIMIT),
        name="ffn",
    )(x, g.reshape(1, d), wg, wu, wd)


def _rmsmm_body(x_ref, g_ref, w_ref, o_ref, h_ref):
    @pl.when(pl.program_id(1) == 0)
    def _():
        h_ref[...] = _rms(x_ref[...], g_ref[...]).astype(bf16)

    o_ref[...] = jnp.dot(h_ref[...], w_ref[...], preferred_element_type=f32)


PROJ_TN = 768


def _pad_cols(w, mult):
    pad = -w.shape[1] % mult
    return w if pad == 0 else jnp.concatenate([w, jnp.zeros((w.shape[0], pad), w.dtype)], axis=1)


def _rms_matmul(x, g, w, *, tm=1024, tn=PROJ_TN):
    m, d = x.shape
    n = w.shape[1]
    return pl.pallas_call(
        _rmsmm_body,
        out_shape=jax.ShapeDtypeStruct((m, n), f32),
        grid=(m // tm, n // tn),
        in_specs=[
            pl.BlockSpec((tm, d), lambda i, j: (i, 0), pipeline_mode=pl.Buffered(1)),
            pl.BlockSpec((1, d), lambda i, j: (0, 0)),
            pl.BlockSpec((d, tn), lambda i, j: (0, j)),
        ],
        out_specs=pl.BlockSpec((tm, tn), lambda i, j: (i, j)),
        scratch_shapes=[pltpu.VMEM((tm, d), bf16)],
        compiler_params=_cparams(2),
        name="rms_matmul",
    )(x, g.reshape(1, d), w)


def _outproj_body(ap_ref, bp_ref, as_ref, bs_ref, w_ref, r_ref, o_ref, *, n_prompt_tiles):
    i = pl.program_id(0)
    half = ap_ref.shape[1]

    def run(a_ref, b_ref):
        o_ref[...] = (r_ref[...] + jnp.dot(a_ref[...], w_ref[:half, :], preferred_element_type=f32)
                      + jnp.dot(b_ref[...], w_ref[half:, :], preferred_element_type=f32))

    @pl.when(i < n_prompt_tiles)
    def _():
        run(ap_ref, bp_ref)

    @pl.when(i >= n_prompt_tiles)
    def _():
        run(as_ref, bs_ref)


def _out_proj(prompt_parts, sample_parts, w, res, *, tm=512, tn=512):
    (a_p, b_p), (a_s, b_s) = prompt_parts, sample_parts
    m, n = res.shape
    half = a_p.shape[1]
    npt = a_p.shape[0] // tm
    assert a_p.shape[0] % tm == 0 and a_s.shape[0] % tm == 0 and m == a_p.shape[0] + a_s.shape[0]
    pspec = pl.BlockSpec((tm, half), lambda i, j: (jnp.minimum(i, npt - 1), 0))
    sspec = pl.BlockSpec((tm, half), lambda i, j: (jnp.maximum(i - npt, 0), 0))
    return pl.pallas_call(
        functools.partial(_outproj_body, n_prompt_tiles=npt),
        out_shape=jax.ShapeDtypeStruct((m, n), f32),
        grid=(m // tm, n // tn),
        in_specs=[pspec, pspec, sspec, sspec,
                  pl.BlockSpec((2 * half, tn), lambda i, j: (0, j)),
                  pl.BlockSpec((tm, tn), lambda i, j: (i, j))],
        out_specs=pl.BlockSpec((tm, tn), lambda i, j: (i, j)),
        compiler_params=_cparams(2),
        name="out_proj",
    )(a_p, b_p, a_s, b_s, w, res)


def _tri_masks(c):
    row = _iota2((c, c), 0)
    col = _iota2((c, c), 1)
    return row, col


def _gdn_body(*refs, c_in, n_chunks, levels, has_state, first_chunk_zero_prev):
    if has_state:
        (u_ref, p8_ref, gate_ref, sm_ref, cw_ref, alog_ref, dtb_ref, ng_ref, s0_ref, o_ref, so_ref, s_scr) = refs
    else:
        (u_ref, p8_ref, gate_ref, sm_ref, cw_ref, alog_ref, dtb_ref, ng_ref, o_ref, so_ref, s_scr) = refs
    c = pl.program_id(1)
    C = CHUNK

    @pl.when(c == 0)
    def _():
        if has_state:
            s_scr[...] = s0_ref[...]
        else:
            s_scr[...] = jnp.zeros_like(s_scr)

    u = u_ref[...]
    p8 = p8_ref[...]
    if first_chunk_zero_prev:
        p8 = jnp.where(c == 0, 0.0, p8)
    cw = cw_ref[...]
    row8 = _iota2((8, u.shape[1]), 0)
    acc = u * cw[3:4]
    for k in range(1, 4):
        rolled = pltpu.roll(u, k, axis=0)
        first8 = jnp.where(row8 < k, pltpu.roll(p8, k, axis=0), rolled[0:8])
        sh = first8 if c_in == 8 else jnp.concatenate([first8, rolled[8:]], axis=0)
        acc = acc + sh * cw[3 - k:4 - k]
    qkv = acc * jax.nn.sigmoid(acc)

    sm = sm_ref[...]
    gfull = _pad_rows(-jnp.exp(alog_ref[...]) * _softplus(sm + dtb_ref[...]), C)
    beta = _pad_rows(jax.nn.sigmoid(sm), C)
    row, col = _tri_masks(C)
    causal = col <= row
    strict = col < row
    eye = jnp.where(row == col, 1.0, 0.0).astype(f32)
    gc = _mm_exact_a(jnp.where(causal, 1.0, 0.0), gfull)
    gate = gate_ref[...]
    ng = ng_ref[...]

    heads = range(NH)
    qn, kn, vb, kbe, decay, gch, a_mats, qk = [], [], [], [], [], [], [], []
    for h in heads:
        q = _pad_rows(qkv[:, h * HD:(h + 1) * HD], C)
        k = _pad_rows(qkv[:, (NH + h) * HD:(NH + h + 1) * HD], C)
        v = _pad_rows(qkv[:, (2 * NH + h) * HD:(2 * NH + h + 1) * HD], C)
        qn.append(q * lax.rsqrt(jnp.sum(q * q, axis=-1, keepdims=True) + 1e-6) * (HD ** -0.5))
        kn.append(k * lax.rsqrt(jnp.sum(k * k, axis=-1, keepdims=True) + 1e-6))
        gch.append(gc[:, h:h + 1])
        bh = beta[:, NH + h:NH + h + 1]
        gcb = jnp.broadcast_to(gch[h], (C, C))
        decay.append(jnp.where(causal, jnp.exp(jnp.where(causal, gcb - jnp.transpose(gcb), 0.0)), 0.0))
        kb = kn[h] * bh
        vb.append(v * bh)
        kbe.append(kb * jnp.exp(gch[h]))
        kq = _mm(jnp.concatenate([kb, qn[h]], axis=0), kn[h], NT)
        a_mats.append(jnp.where(strict, kq[:C] * decay[h], 0.0))
        qk.append(kq[C:] * decay[h])
    x_inv = _tri_inverse(a_mats, eye, levels)
    sol = [_mm_split(x_inv[h], [jnp.concatenate([vb[h], kbe[h]], axis=1)])[0] for h in heads]
    from_s = [_mm(jnp.concatenate([sol[h][:, HD:], qn[h] * jnp.exp(gch[h])], axis=0), s_scr[h]) for h in heads]
    uu = [sol[h][:, :HD] - from_s[h][:C] for h in heads]
    g_end = [gch[h][C - 1:C, :] for h in heads]
    from_u = [_mm(jnp.concatenate([qk[h], jnp.transpose(kn[h] * jnp.exp(g_end[h] - gch[h]))], axis=0), uu[h])
              for h in heads]
    for h in heads:
        s_scr[h] = s_scr[h] * jnp.exp(g_end[h]) + from_u[h][C:]
        gt = gate[:, h * HD:(h + 1) * HD]
        y = _rms((from_s[h][C:] + from_u[h][:C])[:c_in], ng) * (gt * jax.nn.sigmoid(gt))
        o_ref[:, h * HD:(h + 1) * HD] = y.astype(bf16)

    @pl.when(c == n_chunks - 1)
    def _():
        so_ref[...] = s_scr[...]


def _gdn(z, row_off, nb, t, conv_w, a_log, dt_bias, norm_g, s0=None, prev8=None):
    c_in = min(t, CHUNK)
    n_chunks = t // c_in
    has_state = s0 is not None
    levels = max(1, int(math.log2(c_in)))
    rb = row_off // c_in
    w3 = 3 * NH * HD
    if prev8 is None:
        prev_arr = z
        prev_spec = pl.BlockSpec((8, w3), lambda b, c: (jnp.maximum((row_off + b * t + c * c_in) // 8 - 1, 0), 0))
    else:
        prev_arr = prev8
        prev_spec = pl.BlockSpec((None, 8, w3), lambda b, c: (b, 0, 0))
    pad128 = lambda v: jnp.zeros((1, HD), f32).at[0, :NH].set(v.astype(f32))
    in_specs = [
        pl.BlockSpec((c_in, w3), lambda b, c: (rb + b * n_chunks + c, 0)),
        prev_spec,
        pl.BlockSpec((c_in, NH * HD), lambda b, c: (rb + b * n_chunks + c, 3)),
        pl.BlockSpec((c_in, HD), lambda b, c: (rb + b * n_chunks + c, 64)),
        pl.BlockSpec((4, w3), lambda b, c: (0, 0)),
        pl.BlockSpec((1, HD), lambda b, c: (0, 0)),
        pl.BlockSpec((1, HD), lambda b, c: (0, 0)),
        pl.BlockSpec((1, HD), lambda b, c: (0, 0)),
    ]
    args = [z, prev_arr, z, z, conv_w, pad128(a_log), pad128(dt_bias), norm_g.reshape(1, HD)]
    if has_state:
        in_specs.append(pl.BlockSpec((None, NH, HD, HD), lambda b, c: (b, 0, 0, 0)))
        args.append(s0)
    return pl.pallas_call(
        functools.partial(_gdn_body, c_in=c_in, n_chunks=n_chunks, levels=levels, has_state=has_state,
                          first_chunk_zero_prev=prev8 is None),
        out_shape=(jax.ShapeDtypeStruct((nb * t, NH * HD), bf16), jax.ShapeDtypeStruct((nb, NH, HD, HD), f32)),
        grid=(nb, n_chunks),
        in_specs=in_specs,
        out_specs=(pl.BlockSpec((c_in, NH * HD), lambda b, c: (b * n_chunks + c, 0)),
                   pl.BlockSpec((None, NH, HD, HD), lambda b, c: (b, 0, 0, 0))),
        scratch_shapes=[pltpu.VMEM((NH, HD, HD), f32)],
        compiler_params=_cparams(2),
        name="gdn",
    )(*args)


def _mlstm_body(*refs, c_in, n_chunks, has_state):
    if has_state:
        (q_ref, k_ref, v_ref, og_ref, sm_ref, ib_ref, fb_ref, ng_ref, c0_ref, n0_ref, m0_ref,
         o_ref, co_ref, no_ref, mo_ref, c_scr, n_scr, m_scr) = refs
    else:
        (q_ref, k_ref, v_ref, og_ref, sm_ref, ib_ref, fb_ref, ng_ref,
         o_ref, co_ref, no_ref, mo_ref, c_scr, n_scr, m_scr) = refs
    c = pl.program_id(1)
    C = CHUNK

    @pl.when(c == 0)
    def _():
        if has_state:
            c_scr[...] = c0_ref[...]
            n_scr[...] = n0_ref[...]
            m_scr[...] = m0_ref[...]
        else:
            c_scr[...] = jnp.zeros_like(c_scr)
            n_scr[...] = jnp.zeros_like(n_scr)
            m_scr[...] = jnp.zeros_like(m_scr)

    sm = sm_ref[...]
    log_i = _pad_rows(sm + ib_ref[...], C, NEG)
    log_f = _pad_rows(_log_sigmoid(sm + fb_ref[...]), C)
    row, col = _tri_masks(C)
    causal = col <= row
    bcum = _mm_exact_a(jnp.where(causal, 1.0, 0.0), log_f)
    ng = ng_ref[...]
    qa, ka, va, oga = q_ref[...], k_ref[...], v_ref[...], og_ref[...]

    heads = range(NH)
    hsl = [slice(h * HD, (h + 1) * HD) for h in heads]
    q = [_pad_rows(qa[:, hsl[h]], C) for h in heads]
    k = [_pad_rows(ka[:, hsl[h]], C) * (HD ** -0.5) for h in heads]
    v = [_pad_rows(va[:, hsl[h]], C) for h in heads]
    qk = [_mm(q[h], k[h], NT) for h in heads]
    inter = [_mm(q[h], c_scr[h]) for h in heads]
    m, w_inter, sc, a, b, ms = [], [], [], [], [], []
    for h in heads:
        b.append(bcum[:, 3 * NH + h:3 * NH + h + 1])
        a.append(log_i[:, 2 * NH + h:2 * NH + h + 1] - b[h])
        arow = jnp.transpose(jnp.broadcast_to(a[h], (C, C)))
        cm = jnp.max(jnp.where(causal, arow, -jnp.inf), axis=-1, keepdims=True)
        ms.append(m_scr[h:h + 1, 0:1])
        m.append(jnp.maximum(b[h] + ms[h], b[h] + cm))
        w_inter.append(jnp.exp(b[h] + ms[h] - m[h]))
        sc.append(qk[h] * jnp.where(causal, jnp.exp(jnp.where(causal, b[h] + arow - m[h], 0.0)), 0.0))
    kw, d_end = [], []
    for h in heads:
        m_end = m[h][C - 1:C, :]
        b_end = b[h][C - 1:C, :]
        kw.append(k[h] * jnp.exp(b_end + a[h] - m_end))
        d_end.append(jnp.exp(b_end + ms[h] - m_end))
    from_v = [_mm(jnp.concatenate([sc[h], jnp.transpose(kw[h])], axis=0), v[h]) for h in heads]
    for h in heads:
        ns = n_scr[h:h + 1, :]
        num = w_inter[h] * inter[h] + from_v[h][:C]
        den = w_inter[h] * jnp.sum(q[h] * ns, axis=-1, keepdims=True) + jnp.sum(sc[h], axis=-1, keepdims=True)
        hc = num / (jnp.maximum(jnp.abs(den), jnp.exp(-m[h])) + MLSTM_EPS)
        c_scr[h] = d_end[h] * c_scr[h] + from_v[h][C:]
        n_scr[h:h + 1, :] = d_end[h] * ns + jnp.sum(kw[h], axis=0, keepdims=True)
        m_scr[h:h + 1, :] = jnp.broadcast_to(m[h][C - 1:C, :], (1, HD))
        o_ref[:, hsl[h]] = (_rms(hc[:c_in], ng) * jax.nn.sigmoid(oga[:, hsl[h]])).astype(bf16)

    @pl.when(c == n_chunks - 1)
    def _():
        co_ref[...] = c_scr[...]
        no_ref[...] = n_scr[...]
        mo_ref[...] = m_scr[...]


def _mlstm(z, row_off, nb, t, gate_bias, norm_g, c0=None, n0=None, m0=None):
    c_in = min(t, CHUNK)
    n_chunks = t // c_in
    has_state = c0 is not None
    rb = row_off // c_in
    w = NH * HD
    ib = jnp.zeros((1, HD), f32).at[0, 2 * NH:3 * NH].set(gate_bias[0].astype(f32))
    fb = jnp.zeros((1, HD), f32).at[0, 3 * NH:4 * NH].set(gate_bias[1].astype(f32))
    zspec = lambda j: pl.BlockSpec((c_in, w), lambda b, c: (rb + b * n_chunks + c, j))
    in_specs = [zspec(4), zspec(5), zspec(6), zspec(7),
                pl.BlockSpec((c_in, HD), lambda b, c: (rb + b * n_chunks + c, 64)),
                pl.BlockSpec((1, HD), lambda b, c: (0, 0)),
                pl.BlockSpec((1, HD), lambda b, c: (0, 0)),
                pl.BlockSpec((1, HD), lambda b, c: (0, 0))]
    args = [z, z, z, z, z, ib, fb, norm_g.reshape(1, HD)]
    if has_state:
        in_specs += [pl.BlockSpec((None, NH, HD, HD), lambda b, c: (b, 0, 0, 0)),
                     pl.BlockSpec((None, NH, HD), lambda b, c: (b, 0, 0)),
                     pl.BlockSpec((None, NH, HD), lambda b, c: (b, 0, 0))]
        args += [c0, n0, jnp.broadcast_to(m0[..., None], m0.shape + (HD,))]
    return pl.pallas_call(
        functools.partial(_mlstm_body, c_in=c_in, n_chunks=n_chunks, has_state=has_state),
        out_shape=(jax.ShapeDtypeStruct((nb * t, w), bf16), jax.ShapeDtypeStruct((nb, NH, HD, HD), f32),
                   jax.ShapeDtypeStruct((nb, NH, HD), f32), jax.ShapeDtypeStruct((nb, NH, HD), f32)),
        grid=(nb, n_chunks),
        in_specs=in_specs,
        out_specs=(pl.BlockSpec((c_in, w), lambda b, c: (b * n_chunks + c, 0)),
                   pl.BlockSpec((None, NH, HD, HD), lambda b, c: (b, 0, 0, 0)),
                   pl.BlockSpec((None, NH, HD), lambda b, c: (b, 0, 0)),
                   pl.BlockSpec((None, NH, HD), lambda b, c: (b, 0, 0))),
        scratch_shapes=[pltpu.VMEM((NH, HD, HD), f32), pltpu.VMEM((NH, HD), f32), pltpu.VMEM((NH, HD), f32)],
        compiler_params=_cparams(2),
        name="mlstm",
    )(*args)


def _block_masks(c, t):
    row, col = _tri_masks(c)
    same = (row // t) == (col // t)
    return row, col, same


def _gdns_body(u_ref, p_ref, gate_ref, sm_ref, cw_ref, alog_ref, dtb_ref, ng_ref, s0_ref, o_ref, so_ref, *, t, levels):
    C = CHUNK
    nseq = C // t
    u = u_ref[...]
    p = p_ref[...]
    cw = cw_ref[...]
    rowm = _iota2(u.shape, 0) % t
    acc = u * cw[3:4]
    for k in range(1, 4):
        sh = jnp.where(rowm < k, pltpu.roll(p, (k - t) % C, axis=0), pltpu.roll(u, k, axis=0))
        acc = acc + sh * cw[3 - k:4 - k]
    qkv = acc * jax.nn.sigmoid(acc)

    sm = sm_ref[...]
    gfull = -jnp.exp(alog_ref[...]) * _softplus(sm + dtb_ref[...])
    beta = jax.nn.sigmoid(sm)
    row, col, same = _block_masks(C, t)
    causal = (col <= row) & same
    strict = (col < row) & same
    eye = jnp.where(row == col, 1.0, 0.0).astype(f32)
    gc = _mm_exact_a(jnp.where(causal, 1.0, 0.0), gfull)
    gsum = _mm_exact_a(jnp.where(same, 1.0, 0.0), gfull)
    colseq = _iota2((HD, C), 1) // t
    gate = gate_ref[...]
    ng = ng_ref[...]

    for h in range(NH):
        q = qkv[:, h * HD:(h + 1) * HD]
        k = qkv[:, (NH + h) * HD:(NH + h + 1) * HD]
        v = qkv[:, (2 * NH + h) * HD:(2 * NH + h + 1) * HD]
        qn = q * lax.rsqrt(jnp.sum(q * q, axis=-1, keepdims=True) + 1e-6) * (HD ** -0.5)
        kn = k * lax.rsqrt(jnp.sum(k * k, axis=-1, keepdims=True) + 1e-6)
        gch = gc[:, h:h + 1]
        gend = gsum[:, h:h + 1]
        bh = beta[:, NH + h:NH + h + 1]
        gcb = jnp.broadcast_to(gch, (C, C))
        decay = jnp.where(causal, jnp.exp(jnp.where(causal, gcb - jnp.transpose(gcb), 0.0)), 0.0)
        eg = jnp.exp(gch)
        kb = kn * bh
        kq = _mm(jnp.concatenate([kb, qn], axis=0), kn, NT)
        a_mat = jnp.where(strict, kq[:C] * decay, 0.0)
        (x_inv,) = _tri_inverse([a_mat], eye, levels)
        (sol,) = _mm_split(x_inv, [jnp.concatenate([v * bh, kb * eg], axis=1)])
        qe = qn * eg
        u_parts, o_parts = [], []
        for b in range(nseq):
            rs = slice(b * t, (b + 1) * t)
            r2 = _mm(jnp.concatenate([sol[rs, HD:], qe[rs]], axis=0), s0_ref[b, h])
            u_parts.append(sol[rs, :HD] - r2[:t])
            o_parts.append(r2[t:])
        uu = jnp.concatenate(u_parts, axis=0)
        o = jnp.concatenate(o_parts, axis=0) + _mm(kq[C:] * decay, uu)
        kdec_t = jnp.transpose(kn * jnp.exp(gend - gch))
        for b in range(nseq):
            so_ref[b, h] = (s0_ref[b, h] * jnp.exp(gend[b * t:b * t + 1, :])
                            + _mm(jnp.where(colseq == b, kdec_t, 0.0), uu))
        gt = gate[:, h * HD:(h + 1) * HD]
        o_ref[:, h * HD:(h + 1) * HD] = (_rms(o, ng) * (gt * jax.nn.sigmoid(gt))).astype(bf16)


def _gdn_decode(z, row_off, nb, t, conv_w, a_log, dt_bias, norm_g, s0, prev):
    nseq = CHUNK // t
    assert t == 8 and nb % nseq == 0 and row_off % CHUNK == 0
    rb = row_off // CHUNK
    w3 = 3 * NH * HD
    pad128 = lambda v: jnp.zeros((1, HD), f32).at[0, :NH].set(v.astype(f32))
    return pl.pallas_call(
        functools.partial(_gdns_body, t=t, levels=int(math.log2(t))),
        out_shape=(jax.ShapeDtypeStruct((nb * t, NH * HD), bf16), jax.ShapeDtypeStruct((nb, NH, HD, HD), f32)),
        grid=(nb // nseq,),
        in_specs=[pl.BlockSpec((CHUNK, w3), lambda i: (rb + i, 0)),
                  pl.BlockSpec((CHUNK, w3), lambda i: (i, 0)),
                  pl.BlockSpec((CHUNK, NH * HD), lambda i: (rb + i, 3)),
                  pl.BlockSpec((CHUNK, HD), lambda i: (rb + i, 64)),
                  pl.BlockSpec((4, w3), lambda i: (0, 0)),
                  pl.BlockSpec((1, HD), lambda i: (0, 0)),
                  pl.BlockSpec((1, HD), lambda i: (0, 0)),
                  pl.BlockSpec((1, HD), lambda i: (0, 0)),
                  pl.BlockSpec((nseq, NH, HD, HD), lambda i: (i, 0, 0, 0))],
        out_specs=(pl.BlockSpec((CHUNK, NH * HD), lambda i: (i, 0)),
                   pl.BlockSpec((nseq, NH, HD, HD), lambda i: (i, 0, 0, 0))),
        compiler_params=_cparams(1),
        name="gdn_decode",
    )(z, prev.reshape(nb * t, w3), z, z, conv_w, pad128(a_log), pad128(dt_bias), norm_g.reshape(1, HD), s0)


def _mlstms_body(q_ref, k_ref, v_ref, og_ref, sm_ref, ib_ref, fb_ref, ng_ref, c0_ref, n0_ref, m0_ref,
                 o_ref, co_ref, no_ref, mo_ref, *, t):
    C = CHUNK
    nseq = C // t
    sm = sm_ref[...]
    log_i = sm + ib_ref[...]
    log_f = _log_sigmoid(sm + fb_ref[...])
    m0r = m0_ref[...]
    row, col, same = _block_masks(C, t)
    causal = (col <= row) & same
    bcum = _mm_exact_a(jnp.where(causal, 1.0, 0.0), log_f)
    last_of_row = jnp.where(col == (row // t) * t + (t - 1), 1.0, 0.0)
    last_of_seq = jnp.where(_iota2((nseq, C), 1) == _iota2((nseq, C), 0) * t + (t - 1), 1.0, 0.0)
    seq_rows = jnp.where(_iota2((nseq, C), 1) // t == _iota2((nseq, C), 0), 1.0, 0.0)
    own_seq = _iota2((C, nseq), 1) == _iota2((C, nseq), 0) // t
    colseq = _iota2((HD, C), 1) // t
    lane = _iota2((C, HD), 1)
    ng = ng_ref[...]
    qa, ka, va, oga = q_ref[...], k_ref[...], v_ref[...], og_ref[...]

    for h in range(NH):
        sl = slice(h * HD, (h + 1) * HD)
        q = qa[:, sl]
        k = ka[:, sl] * (HD ** -0.5)
        v = va[:, sl]
        b = bcum[:, 3 * NH + h:3 * NH + h + 1]
        a = log_i[:, 2 * NH + h:2 * NH + h + 1] - b
        arow = jnp.transpose(jnp.broadcast_to(a, (C, C)))
        cm = jnp.max(jnp.where(causal, arow, -jnp.inf), axis=-1, keepdims=True)
        ms = m0r[:, h:h + 1]
        m = jnp.maximum(b + ms, b + cm)
        w_inter = jnp.exp(b + ms - m)
        dmat = jnp.where(causal, jnp.exp(jnp.where(causal, b + arow - m, 0.0)), 0.0)
        sc = _mm(q, k, NT) * dmat
        ends = _mm_exact_a(last_of_row, jnp.where(lane == 0, b, jnp.where(lane == 1, m, 0.0)))
        b_end, m_end = ends[:, 0:1], ends[:, 1:2]
        w_end = jnp.exp(b_end + a - m_end)
        d_end = jnp.exp(b_end + ms - m_end)
        kw = k * w_end
        kw_t = jnp.transpose(kw)
        nh = n0_ref[h]
        qn = jnp.sum(jnp.where(own_seq, _mm(q, nh, NT), 0.0), axis=-1, keepdims=True)
        inter = jnp.concatenate([_mm(q[b_ * t:(b_ + 1) * t], c0_ref[b_, h]) for b_ in range(nseq)], axis=0)
        num = w_inter * inter + _mm(sc, v)
        den = w_inter * qn + jnp.sum(sc, axis=-1, keepdims=True)
        hc = num / (jnp.maximum(jnp.abs(den), jnp.exp(-m)) + MLSTM_EPS)
        for b_ in range(nseq):
            co_ref[b_, h] = d_end[b_ * t:b_ * t + 1, :] * c0_ref[b_, h] + _mm(jnp.where(colseq == b_, kw_t, 0.0), v)
        per_seq = _mm_exact_a(last_of_seq, jnp.where(lane == 0, d_end, jnp.where(lane == 1, m_end, 0.0)))
        no_ref[h] = per_seq[:, 0:1] * nh + _mm_exact_a(seq_rows, kw)
        mo_ref[h] = jnp.broadcast_to(per_seq[:, 1:2], (nseq, HD))
        o_ref[:, sl] = (_rms(hc, ng) * jax.nn.sigmoid(oga[:, sl])).astype(bf16)


def _mlstm_decode(z, row_off, nb, t, gate_bias, norm_g, c0, n0, m0):
    nseq = CHUNK // t
    assert nb % nseq == 0 and row_off % CHUNK == 0
    rb = row_off // CHUNK
    w = NH * HD
    ib = jnp.zeros((1, HD), f32).at[0, 2 * NH:3 * NH].set(gate_bias[0].astype(f32))
    fb = jnp.zeros((1, HD), f32).at[0, 3 * NH:4 * NH].set(gate_bias[1].astype(f32))
    m_rows = jnp.zeros((nb, t, HD), f32).at[:, :, :NH].set(jnp.broadcast_to(m0[:, None, :], (nb, t, NH))).reshape(nb * t, HD)
    zspec = lambda j: pl.BlockSpec((CHUNK, w), lambda i: (rb + i, j))
    hspec = pl.BlockSpec((NH, nseq, HD), lambda i: (0, i, 0))
    out, c_new, n_new, m_new = pl.pallas_call(
        functools.partial(_mlstms_body, t=t),
        out_shape=(jax.ShapeDtypeStruct((nb * t, w), bf16), jax.ShapeDtypeStruct((nb, NH, HD, HD), f32),
                   jax.ShapeDtypeStruct((NH, nb, HD), f32), jax.ShapeDtypeStruct((NH, nb, HD), f32)),
        grid=(nb // nseq,),
        in_specs=[zspec(4), zspec(5), zspec(6), zspec(7),
                  pl.BlockSpec((CHUNK, HD), lambda i: (rb + i, 64)),
                  pl.BlockSpec((1, HD), lambda i: (0, 0)),
                  pl.BlockSpec((1, HD), lambda i: (0, 0)),
                  pl.BlockSpec((1, HD), lambda i: (0, 0)),
                  pl.BlockSpec((nseq, NH, HD, HD), lambda i: (i, 0, 0, 0)),
                  hspec,
                  pl.BlockSpec((CHUNK, HD), lambda i: (i, 0))],
        out_specs=(pl.BlockSpec((CHUNK, w), lambda i: (i, 0)),
                   pl.BlockSpec((nseq, NH, HD, HD), lambda i: (i, 0, 0, 0)), hspec, hspec),
        compiler_params=_cparams(1),
        name="mlstm_decode",
    )(z, z, z, z, z, ib, fb, norm_g.reshape(1, HD), c0, jnp.swapaxes(n0, 0, 1), m_rows)
    return out, c_new, jnp.swapaxes(n_new, 0, 1), jnp.swapaxes(m_new, 0, 1)


def _even_perm():
    w = NH * HD
    offs = np.cumsum([0, 3 * w, NH, NH, w, w, w, w, NH, NH, w])
    seg = lambda i: np.arange(offs[i], offs[i + 1])
    return np.concatenate([seg(0), seg(3), seg(4), seg(5), seg(6), seg(9), seg(1), seg(2), seg(7), seg(8)])


SCALE = HD ** -0.5


def _kvnorm_body(sk_ref, wk_ref, g_ref, so_ref, wo_ref):
    g = g_ref[...]
    for src, dst, gi in ((sk_ref, so_ref, 1), (wk_ref, wo_ref, 2)):
        x = src[...]
        for j in range(NKV):
            dst[:, j * HD:(j + 1) * HD] = _rms(x[:, j * HD:(j + 1) * HD], g[gi:gi + 1])


def _kvnorm(z, k_g, *, tm=1024):
    m = z.shape[0]
    w = NKV * HD
    return pl.pallas_call(
        _kvnorm_body,
        out_shape=(jax.ShapeDtypeStruct((m, w), f32), jax.ShapeDtypeStruct((m, w), f32)),
        grid=(m // tm,),
        in_specs=[pl.BlockSpec((tm, w), lambda i: (i, 18)), pl.BlockSpec((tm, w), lambda i: (i, 20)),
                  pl.BlockSpec((3, HD), lambda i: (0, 0))],
        out_specs=(pl.BlockSpec((tm, w), lambda i: (i, 0)), pl.BlockSpec((tm, w), lambda i: (i, 0))),
        compiler_params=_cparams(1),
        name="kvnorm",
    )(z, z, k_g)


SBP_HEADS = 4
SBP_QB = 256


def _sb_block_multi(zs, mask, mstrict, rs):
    n = zs[0].shape[0]
    log_betas, log_rests, parts = [], [], []
    for z in zs:
        sp = jnp.log1p(jnp.exp(-jnp.abs(z)))
        log_betas.append(jnp.minimum(z, 0.0) - sp)
        lr = -(jnp.maximum(z, 0.0) + sp)
        if mask is not None:
            lr = jnp.where(mask, lr, 0.0)
        log_rests.append(lr)
        parts.extend(_split2(lr))
    after = jnp.dot(jnp.concatenate(parts, axis=0), mstrict, preferred_element_type=f32)
    atts, new_rs = [], []
    for h, (lb, lr, r) in enumerate(zip(log_betas, log_rests, rs)):
        att = jnp.exp(lb + after[2 * h * n:(2 * h + 1) * n] + after[(2 * h + 1) * n:(2 * h + 2) * n] + r)
        atts.append(att if mask is None else jnp.where(mask, att, 0.0))
        new_rs.append(r + jnp.sum(lr, axis=-1, keepdims=True))
    return atts, new_rs


def _sbp_body(q_ref, k_ref, v_ref, o_ref):
    i = pl.program_id(2)
    row = _iota2((SBP_QB, HD), 0)
    col = _iota2((SBP_QB, HD), 1)
    mstrict = jnp.where(_iota2((HD, HD), 0) > _iota2((HD, HD), 1), 1.0, 0.0).astype(bf16)
    qs = [q_ref[:, h * HD:(h + 1) * HD].astype(bf16) for h in range(SBP_HEADS)]
    n_kb = (i + 1) * (SBP_QB // HD)

    def block(j, carry, masked):
        off = pl.multiple_of(j * HD, HD)
        mask = ((col + j * HD) < (row + i * SBP_QB)) if masked else None
        zs = [_mm(qs[h], k_ref[pl.ds(off, HD), h * HD:(h + 1) * HD], NT) * SCALE for h in range(SBP_HEADS)]
        atts, rs = _sb_block_multi(zs, mask, mstrict, carry[SBP_HEADS:])
        accs = [carry[h] + _mm(atts[h], v_ref[pl.ds(off, HD), h * HD:(h + 1) * HD]) for h in range(SBP_HEADS)]
        return tuple(accs) + tuple(rs)

    n_diag = SBP_QB // HD
    carry = tuple([jnp.zeros((SBP_QB, HD), f32)] * SBP_HEADS + [jnp.zeros((SBP_QB, 1), f32)] * SBP_HEADS)
    for d in range(n_diag):
        carry = block(n_kb - 1 - d, carry, True)
    res = lax.fori_loop(0, n_kb - n_diag, lambda s, c: block(n_kb - n_diag - 1 - s, c, False), carry)
    for h in range(SBP_HEADS):
        o_ref[:, h * HD:(h + 1) * HD] = res[h].astype(bf16)


def _sb_prompt(z, nb, t):
    nq = t // SBP_QB
    w = SBP_HEADS * HD
    ng = NH // SBP_HEADS
    return pl.pallas_call(
        _sbp_body,
        out_shape=jax.ShapeDtypeStruct((nb * t, NH * HD), bf16),
        grid=(nb, ng, nq),
        in_specs=[pl.BlockSpec((SBP_QB, w), lambda b, h, i: (b * nq + i, h)),
                  pl.BlockSpec((t, w), lambda b, h, i: (b, ng + h)),
                  pl.BlockSpec((t, w), lambda b, h, i: (b, 2 * ng + h))],
        out_specs=pl.BlockSpec((SBP_QB, w), lambda b, h, i: (b * nq + i, h)),
        compiler_params=_cparams(3),
        name="sb_prompt",
    )(z, z, z)


SBS_PAGES = 16


def _head_rows(page_ref, h, n_heads):
    return page_ref[pl.ds(h, PAGE, stride=n_heads), :]


def _pages2d(cache):
    return cache.reshape(cache.shape[0], cache.shape[1] * cache.shape[2], cache.shape[3])


def _sbs_body(pt_ref, q_ref, kn_ref, vn_ref, *rest, n_steps, t):
    del pt_ref
    kp_refs, vp_refs = rest[:SBS_PAGES], rest[SBS_PAGES:2 * SBS_PAGES]
    o_ref, acc_scr, r_scr = rest[2 * SBS_PAGES:]
    s = pl.program_id(1)

    @pl.when(s == 0)
    def _():
        acc_scr[...] = jnp.zeros_like(acc_scr)
        r_scr[...] = jnp.zeros_like(r_scr)

    rows = NH * t
    row = _iota2((rows, HD), 0)
    col = _iota2((rows, HD), 1)
    mrow = _iota2((HD, HD), 0)
    mcol = _iota2((HD, HD), 1)
    mstrict = jnp.where(mrow > mcol, 1.0, 0.0).astype(bf16)
    qa = q_ref[...]

    def load():
        return r_scr[:, 0:1], [acc_scr[h * t:(h + 1) * t, :] for h in range(NH)]

    def store(r, accs):
        r_scr[...] = jnp.broadcast_to(r, r_scr.shape)
        for h in range(NH):
            acc_scr[h * t:(h + 1) * t, :] = accs[h]

    @pl.when(s == 0)
    def _():
        r, accs = load()
        z = jnp.concatenate([_mm(qa[:, h * HD:(h + 1) * HD], _pad_rows(kn_ref[:, h * HD:(h + 1) * HD], HD), NT)
                             for h in range(NH)], axis=0) * SCALE
        (att,), (r,) = _sb_block_multi([z], col < (row % t), mstrict, [r])
        store(r, [accs[h] + _mm(att[h * t:(h + 1) * t], _pad_rows(vn_ref[:, h * HD:(h + 1) * HD], HD))
                  for h in range(NH)])

    slots = range(SBS_PAGES - 1, -1, -1)
    r, accs = load()
    z = jnp.concatenate(
        [_mm(qa[:, h * HD:(h + 1) * HD],
             jnp.concatenate([_head_rows(kp_refs[jj], h, NH).astype(bf16) for jj in slots], axis=0), NT)
         for h in range(NH)], axis=0) * SCALE
    sp = jnp.log1p(jnp.exp(-jnp.abs(z)))
    log_beta = jnp.minimum(z, 0.0) - sp
    log_rest = -(jnp.maximum(z, 0.0) + sp)
    blocks = [log_rest[:, c * HD:(c + 1) * HD] for c in range(SBS_PAGES)]
    after = jnp.dot(jnp.concatenate([x for blk in blocks for x in _split2(blk)], axis=0), mstrict,
                    preferred_element_type=f32)
    atts = [None] * SBS_PAGES
    for c in range(SBS_PAGES - 1, -1, -1):
        aft = after[2 * c * rows:(2 * c + 1) * rows] + after[(2 * c + 1) * rows:(2 * c + 2) * rows]
        atts[c] = jnp.exp(log_beta[:, c * HD:(c + 1) * HD] + aft + r)
        r = r + jnp.sum(blocks[c], axis=-1, keepdims=True)
    att = jnp.concatenate(atts, axis=1)
    store(r, [accs[h] + _mm(att[h * t:(h + 1) * t],
                            jnp.concatenate([_head_rows(vp_refs[jj], h, NH).astype(bf16) for jj in slots], axis=0))
              for h in range(NH)])

    @pl.when(s == n_steps - 1)
    def _():
        for h in range(NH):
            o_ref[:, h * HD:(h + 1) * HD] = acc_scr[h * t:(h + 1) * t, :].astype(bf16)


def _sb_sample(z, row_off, nb, t, cache_k, cache_v, page_table):
    n_pages = page_table.shape[1]
    n_steps = n_pages // SBS_PAGES
    rb = row_off // t
    page = lambda jj: pl.BlockSpec((None, PAGE * NH, HD),
                                   lambda b, s, pt: (pt[b, n_pages - 1 - (s * SBS_PAGES + jj)], 0, 0))
    pages = [page(jj) for jj in range(SBS_PAGES)]
    grid_spec = pltpu.PrefetchScalarGridSpec(
        num_scalar_prefetch=1,
        grid=(nb, n_steps),
        in_specs=[pl.BlockSpec((t, NH * HD), lambda b, s, pt: (rb + b, 0)),
                  pl.BlockSpec((t, NH * HD), lambda b, s, pt: (rb + b, 1)),
                  pl.BlockSpec((t, NH * HD), lambda b, s, pt: (rb + b, 2))] + pages + pages,
        out_specs=pl.BlockSpec((t, NH * HD), lambda b, s, pt: (b, 0)),
        scratch_shapes=[pltpu.VMEM((NH * t, HD), f32), pltpu.VMEM((NH * t, HD), f32)],
    )
    return pl.pallas_call(
        functools.partial(_sbs_body, n_steps=n_steps, t=t),
        out_shape=jax.ShapeDtypeStruct((nb * t, NH * HD), bf16),
        grid_spec=grid_spec,
        compiler_params=_cparams(2),
        name="sb_sample",
    )(page_table, z, z, z, *([_pages2d(cache_k)] * SBS_PAGES), *([_pages2d(cache_v)] * SBS_PAGES))


def _t5_bucket(dist):
    n = jnp.maximum(dist, 0)
    exact = N_BUCKETS // 2
    nf = jnp.maximum(n, 1).astype(f32)
    large = exact + (jnp.log(nf / exact) / math.log(MAX_DISTANCE / exact) * (N_BUCKETS - exact)).astype(i32)
    return jnp.where(n < exact, n, jnp.minimum(large, N_BUCKETS - 1))


def _bias_of_bucket(bucket, rel_ref, h):
    vals = [rel_ref[k, h] for k in range(N_BUCKETS)]
    bit = 1
    while len(vals) > 1:
        on = (bucket & bit) != 0
        vals = [jnp.where(on, vals[2 * n + 1], vals[2 * n]) for n in range(len(vals) // 2)]
        bit *= 2
    return vals[0]


def _softmax_step(m, l, acc, logits, mask, v):
    lg = jnp.where(mask, logits, NEG)
    m_new = jnp.maximum(m, jnp.max(lg, axis=-1, keepdims=True))
    alpha = jnp.exp(m - m_new)
    e = jnp.where(mask, jnp.exp(lg - m_new), 0.0)
    return m_new, alpha * l + jnp.sum(e, axis=-1, keepdims=True), alpha * acc + _mm(e, v)


def _softmax_full(logits, mask):
    lg = jnp.where(mask, logits, NEG)
    e = jnp.where(mask, jnp.exp(lg - jnp.max(lg, axis=-1, keepdims=True)), 0.0)
    return e / jnp.maximum(jnp.sum(e, axis=-1, keepdims=True), 1e-30)


def _block_scores(imp_sel, cur, n_sel):
    rows = imp_sel.shape[0]
    imp_pad = jnp.concatenate([imp_sel, jnp.zeros((rows, HD - imp_sel.shape[1]), f32)], axis=1)
    blk = _iota2((rows, HD), 1)
    forced = jnp.where(blk == cur, 2.0 * FORCE_SCORE, jnp.where(blk == 0, FORCE_SCORE, -FORCE_SCORE))
    score = jnp.where((blk < cur) & (blk > 0), imp_pad, forced)
    return jnp.where(blk < n_sel, score, -jnp.inf), blk


def _select_blocks(imp_sel, cur, n_sel):
    score, blk = _block_scores(imp_sel, cur, n_sel)
    sel = jnp.zeros(score.shape, f32)
    for _ in range(min(TOP_N, n_sel)):
        mx = jnp.max(score, axis=-1, keepdims=True)
        idx = jnp.min(jnp.where(score == mx, blk, 1 << 30), axis=-1, keepdims=True)
        hit = blk == idx
        sel = jnp.where(hit, 1.0, sel)
        score = jnp.where(hit, -jnp.inf, score)
    return sel


def _select_blocks_by_rank(imp_sel, cur, n_sel):
    score, _ = _block_scores(imp_sel, cur, n_sel)
    rows = score.shape[0]
    score_t = jnp.transpose(_pad_rows(score, HD, -jnp.inf))
    j_idx = _iota2((HD, HD), 0)
    i_idx = _iota2((HD, HD), 1)
    out = []
    for r in range(rows):
        s_j = score_t[:, r:r + 1]
        s_i = score[r:r + 1, :]
        ahead = (s_j > s_i) | ((s_j == s_i) & (j_idx < i_idx))
        rank = jnp.sum(jnp.where(ahead, 1.0, 0.0), axis=0, keepdims=True)
        out.append(jnp.where(rank < min(TOP_N, n_sel), 1.0, 0.0))
    return jnp.concatenate(out, axis=0)


def _compress(get_rows, cw_ref, cb, ckg, n_groups_rows):
    del n_groups_rows
    xk = jnp.concatenate([get_rows(0, t).astype(bf16) for t in range(CMP_BLOCK)], axis=1)
    xv = jnp.concatenate([get_rows(1, t).astype(bf16) for t in range(CMP_BLOCK)], axis=1)
    return _rms(_mm(xk, cw_ref[0]) + cb[0:1], ckg), _mm(xv, cw_ref[1]) + cb[1:2]


def _cmp_order(n_cmp, shape, axis):
    c = _iota2(shape, axis)
    half = n_cmp // 2
    return 2 * (c % half) + c // half


def _nsap_body(q_ref, gt_ref, ck0_ref, ck1_ref, cv0_ref, cv1_ref, sk_ref, sv_ref, wk_ref, wv_ref, cw_ref, cb_ref, ckg_ref,
               qg_ref, rel_ref, o_ref, kc_scr, vc_scr, bias_scr, *, t):
    b = pl.program_id(0)
    i = pl.program_id(1)
    n_cmp = t // CMP_BLOCK
    half = n_cmp // 2
    n_sel = t // SEL_BLOCK
    QB = HD
    cmp_refs = ((ck0_ref, ck1_ref), (cv0_ref, cv1_ref))

    @pl.when(i == 0)
    def _():
        for g in range(NKV):
            def get_rows(kind, tt, g=g):
                ref = cmp_refs[kind][g]
                return jnp.concatenate([ref[pl.ds(tt, half, stride=2 * CMP_BLOCK), :],
                                        ref[pl.ds(CMP_BLOCK + tt, half, stride=2 * CMP_BLOCK), :]], axis=0)
            kc, vc = _compress(get_rows, cw_ref, cb_ref[...], ckg_ref[...], n_cmp)
            kc_scr[g] = kc
            vc_scr[g] = vc

    @pl.when((b == 0) & (i == 0))
    def _():
        r_ = _iota2((QB, QB), 0)
        c_ = _iota2((QB, QB), 1)
        for kk in range(3):
            bucket = _t5_bucket(r_ - c_ + QB * kk)
            for h in range(NH):
                bias_scr[kk, h] = _bias_of_bucket(bucket, rel_ref, h)

    qall = q_ref[...]
    qg = qg_ref[...]
    qs = [jnp.concatenate([_rms(qall[:, (g * NREP + r) * HD:(g * NREP + r + 1) * HD], qg) for r in range(NREP)], axis=0)
          for g in range(NKV)]
    rep = lambda x: jnp.concatenate([x] * NREP, axis=0)

    qpos_c = i * QB + _iota2((QB, n_cmp), 0)
    dist_c = qpos_c - (_cmp_order(n_cmp, (QB, n_cmp), 1) * CMP_BLOCK + CMP_BLOCK - 1)
    bucket_c = _t5_bucket(dist_c)
    mask_c = rep(dist_c >= 0)
    cur = (i * QB + _iota2((QB, 1), 0)) // SEL_BLOCK
    o_cmp, imps = [], []
    for g in range(NKV):
        bias = jnp.concatenate([_bias_of_bucket(bucket_c, rel_ref, g * NREP + r) for r in range(NREP)], axis=0)
        p = _softmax_full(_mm(qs[g], kc_scr[g], NT) * SCALE + bias, mask_c)
        o_cmp.append(_mm(p, vc_scr[g]))
        imp = p[0:QB] + p[QB:2 * QB] + p[2 * QB:3 * QB] + p[3 * QB:4 * QB]
        imps.append(imp[:, :half] + imp[:, half:])
    sel_all = _select_blocks(jnp.concatenate(imps, axis=0), jnp.concatenate([cur] * NKV, axis=0), n_sel)
    sels = [sel_all[g * QB:(g + 1) * QB] for g in range(NKV)]

    KP = 2 * QB
    krow = _iota2((QB, KP), 0)
    kcol = _iota2((QB, KP), 1)

    def init():
        return tuple(x for _ in range(NKV) for x in (jnp.full((NREP * QB, 1), NEG, f32), jnp.zeros((NREP * QB, 1), f32),
                                                     jnp.zeros((NREP * QB, HD), f32)))

    def attend(carry, p, k_ref, v_ref, mask_of):
        off = pl.multiple_of(p * KP, KP)
        dist = i * QB + krow - (p * KP + kcol)
        d0 = jnp.clip(i - 2 * p, 0, 2)
        d1 = jnp.clip(i - 2 * p - 1, 0, 2)
        out = []
        for g in range(NKV):
            m, l, acc = carry[3 * g:3 * g + 3]
            bias = jnp.concatenate([jnp.concatenate([bias_scr[d0, g * NREP + r], bias_scr[d1, g * NREP + r]], axis=1)
                                    for r in range(NREP)], axis=0)
            logits = _mm(qs[g], k_ref[pl.ds(off, KP), g * HD:(g + 1) * HD], NT) * SCALE + bias
            out.extend(_softmax_step(m, l, acc, logits, rep(mask_of(g, p, dist)), v_ref[pl.ds(off, KP), g * HD:(g + 1) * HD]))
        return tuple(out)

    def sel_mask(g, p, dist):
        er = _iota2((HD, KP), 0)
        ec = _iota2((HD, KP), 1)
        expand = jnp.where(er == (KP // SEL_BLOCK) * p + ec // SEL_BLOCK, 1.0, 0.0)
        return (_mm(sels[g], expand) > 0.5) & (dist >= 0)

    def win_mask(g, p, dist):
        return (dist >= 0) & (dist < WINDOW)

    last_pair = i // 2
    first_win = jnp.maximum(i - WINDOW // QB, 0) // 2
    c_sel = lax.fori_loop(0, last_pair + 1, lambda p, c: attend(c, p, sk_ref, sv_ref, sel_mask), init())
    c_win = lax.fori_loop(first_win, last_pair + 1, lambda p, c: attend(c, p, wk_ref, wv_ref, win_mask), init())

    gts = jax.nn.sigmoid(gt_ref[...])
    for h in range(NH):
        g, r = divmod(h, NREP)
        sl = slice(r * QB, (r + 1) * QB)
        o_s = c_sel[3 * g + 2][sl] / jnp.maximum(c_sel[3 * g + 1][sl], 1e-30)
        o_w = c_win[3 * g + 2][sl] / jnp.maximum(c_win[3 * g + 1][sl], 1e-30)
        o = gts[:, 3 * h:3 * h + 1] * o_cmp[g][sl] + gts[:, 3 * h + 1:3 * h + 2] * o_s + gts[:, 3 * h + 2:3 * h + 3] * o_w
        o_ref[:, h * HD:(h + 1) * HD] = o.astype(bf16)


def _nsa_prompt(z, sk, wk, nb, t, cmp_w, cmp_b, cmp_k_g, q_g, rel_bias):
    nq = t // HD
    n_cmp = t // CMP_BLOCK
    w = NKV * HD
    full = lambda j: pl.BlockSpec((t, w), lambda b, i: (b, j))
    head = lambda j: pl.BlockSpec((t, HD), lambda b, i: (b, j))
    return pl.pallas_call(
        functools.partial(_nsap_body, t=t),
        out_shape=jax.ShapeDtypeStruct((nb * t, NH * HD), bf16),
        grid=(nb, nq),
        in_specs=[pl.BlockSpec((HD, NH * HD), lambda b, i: (b * nq + i, 3)),
                  pl.BlockSpec((HD, HD), lambda b, i: (b * nq + i, 44)),
                  head(32), head(33), head(34), head(35), full(0), full(19), full(0), full(21),
                  pl.BlockSpec((2, CMP_BLOCK * HD, HD), lambda b, i: (0, 0, 0)),
                  pl.BlockSpec((2, HD), lambda b, i: (0, 0)),
                  pl.BlockSpec((1, HD), lambda b, i: (0, 0)),
                  pl.BlockSpec((1, HD), lambda b, i: (0, 0)),
                  pl.BlockSpec(memory_space=pltpu.SMEM)],
        out_specs=pl.BlockSpec((HD, NH * HD), lambda b, i: (b * nq + i, 0)),
        scratch_shapes=[pltpu.VMEM((NKV, n_cmp, HD), f32), pltpu.VMEM((NKV, n_cmp, HD), f32),
                        pltpu.VMEM((3, NH, HD, HD), f32)],
        compiler_params=_cparams(2),
        name="nsa_prompt",
    )(z, z, z, z, z, z, sk, z, wk, z, cmp_w.astype(bf16), cmp_b, cmp_k_g.reshape(1, HD), q_g.reshape(1, HD), rel_bias)


def _nsa_q_groups(q_ref, qg):
    qall = q_ref[...]
    return [jnp.concatenate([_rms(qall[:, (g * NREP + r) * HD:(g * NREP + r + 1) * HD], qg) for r in range(NREP)], axis=0)
            for g in range(NKV)]


CMP_PAGES = 16


def _nsasa_body(pt_ref, q_ref, *rest, n_pages, t):
    del pt_ref
    ckp_refs, cvp_refs = rest[:CMP_PAGES], rest[CMP_PAGES:2 * CMP_PAGES]
    cw_ref, cb_ref, ckg_ref, qg_ref, rel_ref, oc_ref, sel_ref, xk_scr, xv_scr = rest[2 * CMP_PAGES:]
    p = pl.program_id(1)
    pp = 2 * PAGE
    pr_ = _iota2((pp, pp), 0)
    pc_ = _iota2((pp, pp), 1)
    perm = jnp.where(pc_ == (pr_ % 8) * CMP_BLOCK + pr_ // 8, 1.0, 0.0).astype(bf16)
    for pair in range(CMP_PAGES // 2):
        row0 = pl.multiple_of((p * (CMP_PAGES // 2) + pair) * 8, 8)
        targets = [(refs, scr, g) for refs, scr in ((ckp_refs, xk_scr), (cvp_refs, xv_scr)) for g in range(NKV)]
        x2 = jnp.concatenate(
            [jnp.concatenate([_head_rows(refs[2 * pair], g, NKV), _head_rows(refs[2 * pair + 1], g, NKV)], axis=0)
             for refs, _, g in targets], axis=1).astype(bf16)
        moved = jnp.dot(perm, x2, preferred_element_type=f32)
        for n, (_, scr, g) in enumerate(targets):
            for tt in range(CMP_BLOCK):
                scr[g, pl.ds(row0, 8), tt * HD:(tt + 1) * HD] = moved[8 * tt:8 * tt + 8, n * HD:(n + 1) * HD]

    @pl.when(p == n_pages // CMP_PAGES - 1)
    def _():
        past = n_pages * PAGE
        n_cmp = (past + t) // CMP_BLOCK
        n_sel = -(-(past + t) // SEL_BLOCK)
        cb = cb_ref[...]
        kc = _rms(_mm(jnp.concatenate([xk_scr[g] for g in range(NKV)], axis=0), cw_ref[0]) + cb[0:1], ckg_ref[...])
        vc = _mm(jnp.concatenate([xv_scr[g] for g in range(NKV)], axis=0), cw_ref[1]) + cb[1:2]
        qs = _nsa_q_groups(q_ref, qg_ref[...])
        rows = NREP * t
        tq = _iota2((rows, n_cmp), 0) % t
        dist_c = past + tq - (_iota2((rows, n_cmp), 1) * CMP_BLOCK + CMP_BLOCK - 1)
        bucket_c = _t5_bucket(dist_c)
        mask_c = dist_c >= 0
        cur = (past + _iota2((t, 1), 0)) // SEL_BLOCK
        ratio = SEL_BLOCK // CMP_BLOCK
        pair_sum = jnp.where(_iota2((n_cmp, n_cmp // ratio), 0) // ratio == _iota2((n_cmp, n_cmp // ratio), 1), 1.0, 0.0)
        imps = []
        for g in range(NKV):
            bias = jnp.concatenate([_bias_of_bucket(bucket_c[r * t:(r + 1) * t], rel_ref, g * NREP + r)
                                    for r in range(NREP)], axis=0)
            pr = _softmax_full(_mm(qs[g], kc[g * n_cmp:(g + 1) * n_cmp], NT) * SCALE + bias, mask_c)
            oc_ref[g * rows:(g + 1) * rows, :] = _mm(pr, vc[g * n_cmp:(g + 1) * n_cmp])
            imps.append(pr[0:t] + pr[t:2 * t] + pr[2 * t:3 * t] + pr[3 * t:4 * t])
        imp_sel = _mm_exact_b(jnp.concatenate(imps, axis=0), pair_sum)
        sel_ref[...] = _select_blocks_by_rank(imp_sel, jnp.concatenate([cur] * NKV, axis=0), n_sel)


def _nsa_sample_cmp(z, row_off, nb, t, cache_ck, cache_cv, page_table, cmp_w, cmp_b, cmp_k_g, q_g, rel_bias):
    n_pages = page_table.shape[1]
    assert n_pages % CMP_PAGES == 0 and CMP_PAGES % 2 == 0 and t < CMP_BLOCK
    rb = row_off // t
    page = lambda jj: pl.BlockSpec((None, PAGE * NKV, HD), lambda b, p, pt: (pt[b, p * CMP_PAGES + jj], 0, 0))
    pages = [page(jj) for jj in range(CMP_PAGES)]
    cst = lambda *shape: pl.BlockSpec(shape, lambda b, p, pt: (0,) * len(shape))
    grid_spec = pltpu.PrefetchScalarGridSpec(
        num_scalar_prefetch=1,
        grid=(nb, n_pages // CMP_PAGES),
        in_specs=[pl.BlockSpec((t, NH * HD), lambda b, p, pt: (rb + b, 3))] + pages + pages + [
                  cst(2, CMP_BLOCK * HD, HD), cst(2, HD), cst(1, HD), cst(1, HD),
                  pl.BlockSpec(memory_space=pltpu.SMEM)],
        out_specs=(pl.BlockSpec((None, NH * t, HD), lambda b, p, pt: (b, 0, 0)),
                   pl.BlockSpec((None, NKV * t, HD), lambda b, p, pt: (b, 0, 0))),
        scratch_shapes=[pltpu.VMEM((NKV, n_pages * PAGE // CMP_BLOCK, CMP_BLOCK * HD), f32)] * 2,
    )
    return pl.pallas_call(
        functools.partial(_nsasa_body, n_pages=n_pages, t=t),
        out_shape=(jax.ShapeDtypeStruct((nb, NH * t, HD), f32), jax.ShapeDtypeStruct((nb, NKV * t, HD), f32)),
        grid_spec=grid_spec,
        compiler_params=_cparams(2),
        name="nsa_sample_cmp",
    )(page_table, z, *([_pages2d(cache_ck)] * CMP_PAGES), *([_pages2d(cache_cv)] * CMP_PAGES), cmp_w.astype(bf16), cmp_b,
      cmp_k_g.reshape(1, HD), q_g.reshape(1, HD), rel_bias)


def _nsa_decode_body(pt_ref, q_ref, gt_ref, oc_ref, sel_ref, skn_ref, svn_ref, wkn_ref, wvn_ref, wk_ref, wv_ref,
                     *rest, n_pages, t):
    del pt_ref
    skp_refs, svp_refs = rest[:n_pages], rest[n_pages:2 * n_pages]
    qg_ref, rel_ref, o_ref = rest[2 * n_pages:]
    past = n_pages * PAGE
    rows = NREP * t
    qs = _nsa_q_groups(q_ref, qg_ref[...])
    rep = lambda x: jnp.concatenate([x] * NREP, axis=0)

    def bias_of(dist, g):
        bucket = _t5_bucket(dist)
        return jnp.concatenate([_bias_of_bucket(bucket, rel_ref, g * NREP + r) for r in range(NREP)], axis=0)

    nwin = WINDOW // HD
    dist_w = WINDOW + _iota2((t, WINDOW + HD), 0) - _iota2((t, WINDOW + HD), 1)
    ok_w = (dist_w >= 0) & (dist_w < WINDOW)
    n_far = (n_pages - 1) * HD
    n_sel_keys = (n_pages + 1) * HD
    dist_near = PAGE + _iota2((t, 2 * HD), 0) - _iota2((t, 2 * HD), 1)
    ok_s = (past + _iota2((t, n_sel_keys), 0) - _iota2((t, n_sel_keys), 1)) >= 0
    expand = jnp.where(_iota2((HD, n_sel_keys), 0) == _iota2((HD, n_sel_keys), 1) // SEL_BLOCK, 1.0, 0.0)

    jobs = []
    for g in range(NKV):
        gs = slice(g * HD, (g + 1) * HD)
        ks = [wk_ref[pl.ds(jj * HD * NKV + g, HD, stride=NKV), :] for jj in range(nwin)] + [_pad_rows(wkn_ref[:, gs], HD)]
        vs = [wv_ref[pl.ds(jj * HD * NKV + g, HD, stride=NKV), :] for jj in range(nwin)] + [_pad_rows(wvn_ref[:, gs], HD)]
        jobs.append((g, ks, vs, bias_of(dist_w, g), ok_w))
    for g in range(NKV):
        gs = slice(g * HD, (g + 1) * HD)
        ks = [_head_rows(r, g, NKV) for r in skp_refs] + [_pad_rows(skn_ref[:, gs], HD)]
        vs = [_head_rows(r, g, NKV) for r in svp_refs] + [_pad_rows(svn_ref[:, gs], HD)]
        far = jnp.concatenate([jnp.full((t, 1), rel_ref[N_BUCKETS - 1, g * NREP + r], f32) for r in range(NREP)], axis=0)
        bias = jnp.concatenate([jnp.broadcast_to(far, (rows, n_far)), bias_of(dist_near, g)], axis=1)
        picked = _mm(sel_ref[g * t:(g + 1) * t, :], expand) > 0.5
        jobs.append((g, ks, vs, bias, picked & ok_s))

    logits = [jnp.concatenate([_mm(qs[g], k, NT) for k in ks], axis=1) * SCALE + bias for g, ks, _, bias, _ in jobs]
    probs = [_softmax_full(lg, rep(job[4])) for lg, job in zip(logits, jobs)]
    outs = []
    for p, job in zip(probs, jobs):
        acc = jnp.zeros((rows, HD), f32)
        for n, v in enumerate(job[2]):
            acc = acc + _mm(p[:, n * HD:(n + 1) * HD], v)
        outs.append(acc)

    gts = jax.nn.sigmoid(gt_ref[...])
    for h in range(NH):
        g, r = divmod(h, NREP)
        sl = slice(r * t, (r + 1) * t)
        o = (gts[:, 3 * h:3 * h + 1] * oc_ref[g * rows + r * t:g * rows + (r + 1) * t, :]
             + gts[:, 3 * h + 1:3 * h + 2] * outs[NKV + g][sl] + gts[:, 3 * h + 2:3 * h + 3] * outs[g][sl])
        o_ref[:, h * HD:(h + 1) * HD] = o.astype(bf16)


def _nsa_decode_attn(z, sk, wk, row_off, nb, t, o_cmp, sel, win_k, win_v, cache_sk, cache_sv, page_table, q_g, rel_bias):
    n_pages = page_table.shape[1]
    assert PAGE >= MAX_DISTANCE and t <= HD
    rb = row_off // t
    w = NKV * HD
    pages = [pl.BlockSpec((None, PAGE * NKV, HD), lambda b, pt, jj=jj: (pt[b, jj], 0, 0)) for jj in range(n_pages)]
    rowsp = lambda width, j: pl.BlockSpec((t, width), lambda b, pt: (rb + b, j))
    grid_spec = pltpu.PrefetchScalarGridSpec(
        num_scalar_prefetch=1,
        grid=(nb,),
        in_specs=[rowsp(NH * HD, 3), rowsp(HD, 44),
                  pl.BlockSpec((None, NH * t, HD), lambda b, pt: (b, 0, 0)),
                  pl.BlockSpec((None, NKV * t, HD), lambda b, pt: (b, 0, 0)),
                  rowsp(w, 0), rowsp(w, 19), rowsp(w, 0), rowsp(w, 21),
                  pl.BlockSpec((None, WINDOW * NKV, HD), lambda b, pt: (b, 0, 0)),
                  pl.BlockSpec((None, WINDOW * NKV, HD), lambda b, pt: (b, 0, 0))] + pages + pages + [
                  pl.BlockSpec((1, HD), lambda b, pt: (0, 0)),
                  pl.BlockSpec(memory_space=pltpu.SMEM)],
        out_specs=pl.BlockSpec((t, NH * HD), lambda b, pt: (b, 0)),
    )
    return pl.pallas_call(
        functools.partial(_nsa_decode_body, n_pages=n_pages, t=t),
        out_shape=jax.ShapeDtypeStruct((nb * t, NH * HD), bf16),
        grid_spec=grid_spec,
        compiler_params=_cparams(1),
        name="nsa_sample_attn",
    )(page_table, z, z, o_cmp, sel, sk, z, wk, z, _pages2d(win_k), _pages2d(win_v),
      *([_pages2d(cache_sk)] * n_pages), *([_pages2d(cache_sv)] * n_pages), q_g.reshape(1, HD), rel_bias)


def _even_mixer(z, mp, bp, tp, bs, ts, conv_w, a_log, dt_bias, gdn_g, gate_bias, mlstm_g, s0, conv0, c0, n0, m0):
    w3 = 3 * NH * HD
    go_p, gs_p = _gdn(z, 0, bp, tp, conv_w, a_log, dt_bias, gdn_g)
    prev8 = jnp.concatenate([jnp.zeros((bs, 8 - conv0.shape[1], w3), f32), conv0], axis=1)
    go_s, gs_s = _gdn_decode(z, mp, bs, ts, conv_w, a_log, dt_bias, gdn_g, s0, prev8)
    mh_p, mc_p, mn_p, mm_p = _mlstm(z, 0, bp, tp, gate_bias, mlstm_g)
    mh_s, mc_s, mn_s, mm_s = _mlstm_decode(z, mp, bs, ts, gate_bias, mlstm_g, c0, n0, m0)
    mix = ((go_p, mh_p), (go_s, mh_s))
    keep = conv0.shape[1]
    conv_p = jnp.stack([z[(b + 1) * tp - keep:(b + 1) * tp, :w3] for b in range(bp)])
    conv_s = z[mp:, :w3].reshape(bs, ts, w3)[:, ts - keep:]
    states = (gs_p, gs_s, conv_p, conv_s, mc_p, mc_s, mn_p, mn_s, mm_p[..., 0], mm_s[..., 0])
    return mix, states


def _odd_mixer(z, mp, bp, tp, bs, ts, page_table, caches, win_k0, win_v0, q_g, k_g, cmp_w, cmp_b, rel_bias):
    sb_k, sb_v, cmp_k, cmp_v, sel_k, sel_v = caches
    sk, wk = _kvnorm(z, k_g)
    sb_p = _sb_prompt(z, bp, tp)
    sb_s = _sb_sample(z, mp, bs, ts, sb_k, sb_v, page_table)
    ns_p = _nsa_prompt(z, sk, wk, bp, tp, cmp_w, cmp_b, k_g[0], q_g, rel_bias)
    o_cmp, sel = _nsa_sample_cmp(z, mp, bs, ts, cmp_k, cmp_v, page_table, cmp_w, cmp_b, k_g[0], q_g, rel_bias)
    ns_s = _nsa_decode_attn(z, sk, wk, mp, bs, ts, o_cmp, sel, win_k0, win_v0, sel_k, sel_v, page_table, q_g, rel_bias)
    mix = ((sb_p, ns_p), (sb_s, ns_s))

    w = NH * HD
    kw = NKV * HD

    def rows(arr, lo, width, heads):
        sl = arr[:, lo:lo + width]
        return sl[:mp].reshape(bp, tp, heads, HD), sl[mp:].reshape(bs, ts, heads, HD)

    sbk_p, sbk_s = rows(z, w, w, NH)
    sbv_p, sbv_s = rows(z, 2 * w, w, NH)
    ck_p, ck_s = rows(z, 4 * w, kw, NKV)
    cv_p, cv_s = rows(z, 4 * w + kw, kw, NKV)
    sk_p, sk_s = rows(sk, 0, kw, NKV)
    sv_p, sv_s = rows(z, 4 * w + 3 * kw, kw, NKV)
    wk_p, wk_s = rows(wk, 0, kw, NKV)
    wv_p, wv_s = rows(z, 4 * w + 5 * kw, kw, NKV)
    keep_p = min(WINDOW, tp)
    win = lambda old, new: jnp.concatenate([old, new], axis=1)[:, -min(WINDOW, old.shape[1] + ts):]
    states = (sbk_p, sbk_s, sbv_p, sbv_s, ck_p, ck_s, cv_p, cv_s, sk_p, sk_s, sv_p, sv_s,
              wk_p[:, tp - keep_p:], win(win_k0, wk_s), wv_p[:, tp - keep_p:], win(win_v0, wv_s))
    return mix, states


def kernel(x_prompt, x_sample, state_gdn_s, state_gdn_conv, state_mlstm_c, state_mlstm_n, state_mlstm_m, cache_sb_k, cache_sb_v, cache_nsa_cmp_k, cache_nsa_cmp_v, cache_nsa_sel_k, cache_nsa_sel_v, state_nsa_win_k, state_nsa_win_v, page_table, norm_g, ffn_w_gate, ffn_w_up, ffn_w_down, even_w_in, gdn_conv_w, gdn_a_log, gdn_dt_bias, gdn_norm_g, mlstm_gate_bias, mlstm_norm_g, even_w_out, odd_w_in, nsa_q_norm_g, nsa_k_norm_g, nsa_cmp_w, nsa_cmp_b, odd_w_out, rel_bias):
    bp, tp, d = x_prompt.shape
    bs, ts, _ = x_sample.shape
    mp = bp * tp
    xs = jnp.concatenate([x_prompt.reshape(mp, d), x_sample.reshape(bs * ts, d)], axis=0)
    depth = norm_g.shape[0]
    even_states, odd_states = [], []
    for layer in range(depth):
        j = layer // 2
        ffn = lambda x, n, i: _ffn(x, norm_g[layer, n], ffn_w_gate, ffn_w_up, ffn_w_down, layer, i)
        xs = ffn(xs, 0, 0)
        if layer % 2 == 0:
            w_in = even_w_in[j][:, _even_perm()]
            w_in = _pad_cols(w_in.astype(bf16), PROJ_TN)
            z = _rms_matmul(xs, norm_g[layer, 1], w_in)
            mix, st = _even_mixer(z, mp, bp, tp, bs, ts, gdn_conv_w[j], gdn_a_log[j], gdn_dt_bias[j], gdn_norm_g[j],
                                  mlstm_gate_bias[j], mlstm_norm_g[j], state_gdn_s[j], state_gdn_conv[j],
                                  state_mlstm_c[j], state_mlstm_n[j], state_mlstm_m[j])
            even_states.append(st)
            w_out = even_w_out[j]
        else:
            w_in = _pad_cols(odd_w_in[j].astype(bf16), PROJ_TN)
            z = _rms_matmul(xs, norm_g[layer, 1], w_in)
            caches = (cache_sb_k[j], cache_sb_v[j], cache_nsa_cmp_k[j], cache_nsa_cmp_v[j], cache_nsa_sel_k[j],
                      cache_nsa_sel_v[j])
            mix, st = _odd_mixer(z, mp, bp, tp, bs, ts, page_table, caches, state_nsa_win_k[j], state_nsa_win_v[j],
                                 nsa_q_norm_g[j], nsa_k_norm_g[j], nsa_cmp_w[j], nsa_cmp_b[j], rel_bias)
            odd_states.append(st)
            w_out = odd_w_out[j]
        xs = _out_proj(mix[0], mix[1], w_out.astype(bf16), xs)
        xs = ffn(xs, 2, 1)
    stack = lambda sts, i: jnp.stack([s[i] for s in sts])
    outs = [xs[:mp].reshape(bp, tp, d), xs[mp:].reshape(bs, ts, d)]
    outs += [stack(even_states, i) for i in range(10)]
    outs += [stack(odd_states, i) for i in range(16)]
    return tuple(outs)
```

```python
import functools
import math

import jax
import jax.numpy as jnp
import numpy as np
from jax import lax
from jax.experimental import pallas as pl
from jax.experimental.pallas import tpu as pltpu

f32 = jnp.float32
bf16 = jnp.bfloat16
i32 = jnp.int32

HD = 128
NH = 8
NKV = 2
NREP = 4
RMS_EPS = 1e-6
MLSTM_EPS = 1e-6
CHUNK = 128
PAGE = 128
CMP_BLOCK = 32
SEL_BLOCK = 64
TOP_N = 8
WINDOW = 512
FORCE_SCORE = 1.0e4
N_BUCKETS = 32
MAX_DISTANCE = 128
NEG = -1e30
VMEM_LIMIT = 56 * 1024 * 1024

NN = (((1,), (0,)), ((), ()))
NT = (((1,), (1,)), ((), ()))
TN = (((0,), (0,)), ((), ()))


FFN_VMEM_LIMIT = 60 * 1024 * 1024


def _cparams(n_axes, vmem_limit=None):
    return pltpu.CompilerParams(dimension_semantics=("arbitrary",) * n_axes,
                                vmem_limit_bytes=VMEM_LIMIT if vmem_limit is None else vmem_limit)


def _mm(a, b, dims=NN):
    return lax.dot_general(a.astype(bf16), b.astype(bf16), dims, preferred_element_type=f32)


def _split3(a):
    a0 = a.astype(bf16)
    r = a - a0.astype(f32)
    a1 = r.astype(bf16)
    a2 = (r - a1.astype(f32)).astype(bf16)
    return a0, a1, a2


def _mm_exact_b(a, b01, dims=NN):
    a0, a1, a2 = _split3(a)
    b01 = b01.astype(bf16)
    d = lambda x: lax.dot_general(x, b01, dims, preferred_element_type=f32)
    return d(a0) + d(a1) + d(a2)


def _mm_exact_a(a01, b, dims=NN):
    b0, b1, b2 = _split3(b)
    a01 = a01.astype(bf16)
    d = lambda x: lax.dot_general(a01, x, dims, preferred_element_type=f32)
    return d(b0) + d(b1) + d(b2)


def _split2(a):
    ah = a.astype(bf16)
    return ah, (a - ah.astype(f32)).astype(bf16)


def _mm_split(a, bs):
    n = a.shape[0]
    rhs = jnp.concatenate([x for b in bs for x in _split2(b)], axis=1)
    r = jnp.dot(jnp.concatenate(_split2(a), axis=0), rhs, preferred_element_type=f32)
    outs, off = [], 0
    for b in bs:
        w = b.shape[1]
        outs.append((r[:n, off:off + w] + r[:n, off + w:off + 2 * w]) + (r[n:, off:off + w] + r[n:, off + w:off + 2 * w]))
        off += 2 * w
    return outs


def _tri_inverse(a_mats, eye, levels):
    ys = [eye - a for a in a_mats]
    if levels < 2:
        return ys
    qs = [_mm_split(a, [a])[0] for a in a_mats]
    for lvl in range(levels - 1):
        last = lvl == levels - 2
        res = [_mm_split(q, [y] if last else [y, q]) for q, y in zip(qs, ys)]
        ys = [y + r[0] for y, r in zip(ys, res)]
        if not last:
            qs = [r[1] for r in res]
    return ys


def _softplus(x):
    return jnp.maximum(x, 0.0) + jnp.log1p(jnp.exp(-jnp.abs(x)))


def _log_sigmoid(x):
    return -_softplus(-x)


def _rms(x, g):
    return x * lax.rsqrt(jnp.mean(x * x, axis=-1, keepdims=True) + RMS_EPS) * g


def _pad_rows(x, rows, value=0.0):
    if x.shape[0] == rows:
        return x
    return jnp.concatenate([x, jnp.full((rows - x.shape[0],) + x.shape[1:], value, x.dtype)], axis=0)


def _iota2(shape, axis):
    return lax.broadcasted_iota(i32, shape, axis)


def _ffn_body(x_ref, g_ref, wg_ref, wu_ref, wd_ref, o_ref, h_ref, *, nf):
    f = pl.program_id(1)

    del nf

    @pl.when(f == 0)
    def _():
        x = x_ref[...]
        h_ref[...] = _rms(x, g_ref[...]).astype(bf16)
        o_ref[...] = x

    h = h_ref[...]
    a = jnp.dot(h, wg_ref[...].astype(bf16), preferred_element_type=f32)
    u = jnp.dot(h, wu_ref[...].astype(bf16), preferred_element_type=f32)
    act = (0.5 * a * jax.nn.sigmoid(a) * u).astype(bf16)
    o_ref[...] += jnp.dot(act, wd_ref[...].astype(bf16), preferred_element_type=f32)


def _ffn(x, g, wg, wu, wd, layer, idx, *, tm=1024, tf=512):
    m, d = x.shape
    fdim = wg.shape[-1]
    nf = fdim // tf
    once = pl.Buffered(1)
    return pl.pallas_call(
        functools.partial(_ffn_body, nf=nf),
        out_shape=jax.ShapeDtypeStruct((m, d), f32),
        grid=(m // tm, nf),
        in_specs=[
            pl.BlockSpec((tm, d), lambda i, f: (i, 0), pipeline_mode=once),
            pl.BlockSpec((1, d), lambda i, f: (0, 0)),
            pl.BlockSpec((None, None, d, tf), lambda i, f: (layer, idx, 0, f)),
            pl.BlockSpec((None, None, d, tf), lambda i, f: (layer, idx, 0, f)),
            pl.BlockSpec((None, None, tf, d), lambda i, f: (layer, idx, f, 0)),
        ],
        out_specs=pl.BlockSpec((tm, d), lambda i, f: (i, 0), pipeline_mode=once),
        scratch_shapes=[pltpu.VMEM((tm, d), bf16)],
        compiler_params=_cparams(2, FFN_VMEM_LIMIT),
        name="ffn",
    )(x, g.reshape(1, d), wg, wu, wd)


def _rmsmm_body(x_ref, g_ref, w_ref, o_ref, h_ref):
    @pl.when(pl.program_id(1) == 0)
    def _():
        h_ref[...] = _rms(x_ref[...], g_ref[...]).astype(bf16)

    o_ref[...] = jnp.dot(h_ref[...], w_ref[...], preferred_element_type=f32)


PROJ_TN = 768


def _pad_cols(w, mult):
    pad = -w.shape[1] % mult
    return w if pad == 0 else jnp.concatenate([w, jnp.zeros((w.shape[0], pad), w.dtype)], axis=1)


def _rms_matmul(x, g, w, *, tm=1024, tn=PROJ_TN):
    m, d = x.shape
    n = w.shape[1]
    return pl.pallas_call(
        _rmsmm_body,
        out_shape=jax.ShapeDtypeStruct((m, n), f32),
        grid=(m // tm, n // tn),
        in_specs=[
            pl.BlockSpec((tm, d), lambda i, j: (i, 0), pipeline_mode=pl.Buffered(1)),
            pl.BlockSpec((1, d), lambda i, j: (0, 0)),
            pl.BlockSpec((d, tn), lambda i, j: (0, j)),
        ],
        out_specs=pl.BlockSpec((tm, tn), lambda i, j: (i, j)),
        scratch_shapes=[pltpu.VMEM((tm, d), bf16)],
        compiler_params=_cparams(2),
        name="rms_matmul",
    )(x, g.reshape(1, d), w)


def _outproj_body(ap_ref, bp_ref, as_ref, bs_ref, w_ref, r_ref, o_ref, *, n_prompt_tiles):
    i = pl.program_id(0)
    half = ap_ref.shape[1]

    def run(a_ref, b_ref):
        o_ref[...] = (r_ref[...] + jnp.dot(a_ref[...], w_ref[:half, :], preferred_element_type=f32)
                      + jnp.dot(b_ref[...], w_ref[half:, :], preferred_element_type=f32))

    @pl.when(i < n_prompt_tiles)
    def _():
        run(ap_ref, bp_ref)

    @pl.when(i >= n_prompt_tiles)
    def _():
        run(as_ref, bs_ref)


def _out_proj(prompt_parts, sample_parts, w, res, *, tm=512, tn=2048):
    (a_p, b_p), (a_s, b_s) = prompt_parts, sample_parts
    m, n = res.shape
    half = a_p.shape[1]
    npt = a_p.shape[0] // tm
    assert a_p.shape[0] % tm == 0 and a_s.shape[0] % tm == 0 and m == a_p.shape[0] + a_s.shape[0]
    pspec = pl.BlockSpec((tm, half), lambda i, j: (jnp.minimum(i, npt - 1), 0))
    sspec = pl.BlockSpec((tm, half), lambda i, j: (jnp.maximum(i - npt, 0), 0))
    return pl.pallas_call(
        functools.partial(_outproj_body, n_prompt_tiles=npt),
        out_shape=jax.ShapeDtypeStruct((m, n), f32),
        grid=(m // tm, n // tn),
        in_specs=[pspec, pspec, sspec, sspec,
                  pl.BlockSpec((2 * half, tn), lambda i, j: (0, j)),
                  pl.BlockSpec((tm, tn), lambda i, j: (i, j))],
        out_specs=pl.BlockSpec((tm, tn), lambda i, j: (i, j)),
        compiler_params=_cparams(2),
        name="out_proj",
    )(a_p, b_p, a_s, b_s, w, res)


def _tri_masks(c):
    row = _iota2((c, c), 0)
    col = _iota2((c, c), 1)
    return row, col


def _gdn_body(*refs, c_in, n_chunks, levels, has_state, first_chunk_zero_prev):
    if has_state:
        (u_ref, p8_ref, gate_ref, sm_ref, cw_ref, alog_ref, dtb_ref, ng_ref, s0_ref, o_ref, so_ref, s_scr) = refs
    else:
        (u_ref, p8_ref, gate_ref, sm_ref, cw_ref, alog_ref, dtb_ref, ng_ref, o_ref, so_ref, s_scr) = refs
    c = pl.program_id(1)
    C = CHUNK

    @pl.when(c == 0)
    def _():
        if has_state:
            s_scr[...] = s0_ref[...]
        else:
            s_scr[...] = jnp.zeros_like(s_scr)

    u = u_ref[...]
    p8 = p8_ref[...]
    if first_chunk_zero_prev:
        p8 = jnp.where(c == 0, 0.0, p8)
    cw = cw_ref[...]
    row8 = _iota2((8, u.shape[1]), 0)
    acc = u * cw[3:4]
    for k in range(1, 4):
        rolled = pltpu.roll(u, k, axis=0)
        first8 = jnp.where(row8 < k, pltpu.roll(p8, k, axis=0), rolled[0:8])
        sh = first8 if c_in == 8 else jnp.concatenate([first8, rolled[8:]], axis=0)
        acc = acc + sh * cw[3 - k:4 - k]
    qkv = acc * jax.nn.sigmoid(acc)

    sm = sm_ref[...]
    gfull = _pad_rows(-jnp.exp(alog_ref[...]) * _softplus(sm + dtb_ref[...]), C)
    beta = _pad_rows(jax.nn.sigmoid(sm), C)
    row, col = _tri_masks(C)
    causal = col <= row
    strict = col < row
    eye = jnp.where(row == col, 1.0, 0.0).astype(f32)
    gc = _mm_exact_a(jnp.where(causal, 1.0, 0.0), gfull)
    gate = gate_ref[...]
    ng = ng_ref[...]

    heads = range(NH)
    qn, kn, vb, kbe, decay, gch, a_mats, qk = [], [], [], [], [], [], [], []
    for h in heads:
        q = _pad_rows(qkv[:, h * HD:(h + 1) * HD], C)
        k = _pad_rows(qkv[:, (NH + h) * HD:(NH + h + 1) * HD], C)
        v = _pad_rows(qkv[:, (2 * NH + h) * HD:(2 * NH + h + 1) * HD], C)
        qn.append(q * lax.rsqrt(jnp.sum(q * q, axis=-1, keepdims=True) + 1e-6) * (HD ** -0.5))
        kn.append(k * lax.rsqrt(jnp.sum(k * k, axis=-1, keepdims=True) + 1e-6))
        gch.append(gc[:, h:h + 1])
        bh = beta[:, NH + h:NH + h + 1]
        gcb = jnp.broadcast_to(gch[h], (C, C))
        decay.append(jnp.where(causal, jnp.exp(jnp.where(causal, gcb - jnp.transpose(gcb), 0.0)), 0.0))
        kb = kn[h] * bh
        vb.append(v * bh)
        kbe.append(kb * jnp.exp(gch[h]))
        kq = _mm(jnp.concatenate([kb, qn[h]], axis=0), kn[h], NT)
        a_mats.append(jnp.where(strict, kq[:C] * decay[h], 0.0))
        qk.append(kq[C:] * decay[h])
    x_inv = _tri_inverse(a_mats, eye, levels)
    sol = [_mm_split(x_inv[h], [jnp.concatenate([vb[h], kbe[h]], axis=1)])[0] for h in heads]
    from_s = [_mm(jnp.concatenate([sol[h][:, HD:], qn[h] * jnp.exp(gch[h])], axis=0), s_scr[h]) for h in heads]
    uu = [sol[h][:, :HD] - from_s[h][:C] for h in heads]
    g_end = [gch[h][C - 1:C, :] for h in heads]
    from_u = [_mm(jnp.concatenate([qk[h], jnp.transpose(kn[h] * jnp.exp(g_end[h] - gch[h]))], axis=0), uu[h])
              for h in heads]
    for h in heads:
        s_scr[h] = s_scr[h] * jnp.exp(g_end[h]) + from_u[h][C:]
        gt = gate[:, h * HD:(h + 1) * HD]
        y = _rms((from_s[h][C:] + from_u[h][:C])[:c_in], ng) * (gt * jax.nn.sigmoid(gt))
        o_ref[:, h * HD:(h + 1) * HD] = y.astype(bf16)

    @pl.when(c == n_chunks - 1)
    def _():
        so_ref[...] = s_scr[...]


def _gdn(z, row_off, nb, t, conv_w, a_log, dt_bias, norm_g, s0=None, prev8=None):
    c_in = min(t, CHUNK)
    n_chunks = t // c_in
    has_state = s0 is not None
    levels = max(1, int(math.log2(c_in)))
    rb = row_off // c_in
    w3 = 3 * NH * HD
    if prev8 is None:
        prev_arr = z
        prev_spec = pl.BlockSpec((8, w3), lambda b, c: (jnp.maximum((row_off + b * t + c * c_in) // 8 - 1, 0), 0))
    else:
        prev_arr = prev8
        prev_spec = pl.BlockSpec((None, 8, w3), lambda b, c: (b, 0, 0))
    pad128 = lambda v: jnp.zeros((1, HD), f32).at[0, :NH].set(v.astype(f32))
    in_specs = [
        pl.BlockSpec((c_in, w3), lambda b, c: (rb + b * n_chunks + c, 0)),
        prev_spec,
        pl.BlockSpec((c_in, NH * HD), lambda b, c: (rb + b * n_chunks + c, 3)),
        pl.BlockSpec((c_in, HD), lambda b, c: (rb + b * n_chunks + c, 64)),
        pl.BlockSpec((4, w3), lambda b, c: (0, 0)),
        pl.BlockSpec((1, HD), lambda b, c: (0, 0)),
        pl.BlockSpec((1, HD), lambda b, c: (0, 0)),
        pl.BlockSpec((1, HD), lambda b, c: (0, 0)),
    ]
    args = [z, prev_arr, z, z, conv_w, pad128(a_log), pad128(dt_bias), norm_g.reshape(1, HD)]
    if has_state:
        in_specs.append(pl.BlockSpec((None, NH, HD, HD), lambda b, c: (b, 0, 0, 0)))
        args.append(s0)
    return pl.pallas_call(
        functools.partial(_gdn_body, c_in=c_in, n_chunks=n_chunks, levels=levels, has_state=has_state,
                          first_chunk_zero_prev=prev8 is None),
        out_shape=(jax.ShapeDtypeStruct((nb * t, NH * HD), bf16), jax.ShapeDtypeStruct((nb, NH, HD, HD), f32)),
        grid=(nb, n_chunks),
        in_specs=in_specs,
        out_specs=(pl.BlockSpec((c_in, NH * HD), lambda b, c: (b * n_chunks + c, 0)),
                   pl.BlockSpec((None, NH, HD, HD), lambda b, c: (b, 0, 0, 0))),
        scratch_shapes=[pltpu.VMEM((NH, HD, HD), f32)],
        compiler_params=_cparams(2),
        name="gdn",
    )(*args)


def _mlstm_body(*refs, c_in, n_chunks, has_state):
    if has_state:
        (q_ref, k_ref, v_ref, og_ref, sm_ref, ib_ref, fb_ref, ng_ref, c0_ref, n0_ref, m0_ref,
         o_ref, co_ref, no_ref, mo_ref, c_scr, n_scr, m_scr) = refs
    else:
        (q_ref, k_ref, v_ref, og_ref, sm_ref, ib_ref, fb_ref, ng_ref,
         o_ref, co_ref, no_ref, mo_ref, c_scr, n_scr, m_scr) = refs
    c = pl.program_id(1)
    C = CHUNK

    @pl.when(c == 0)
    def _():
        if has_state:
            c_scr[...] = c0_ref[...]
            n_scr[...] = n0_ref[...]
            m_scr[...] = m0_ref[...]
        else:
            c_scr[...] = jnp.zeros_like(c_scr)
            n_scr[...] = jnp.zeros_like(n_scr)
            m_scr[...] = jnp.zeros_like(m_scr)

    sm = sm_ref[...]
    log_i = _pad_rows(sm + ib_ref[...], C, NEG)
    log_f = _pad_rows(_log_sigmoid(sm + fb_ref[...]), C)
    row, col = _tri_masks(C)
    causal = col <= row
    bcum = _mm_exact_a(jnp.where(causal, 1.0, 0.0), log_f)
    ng = ng_ref[...]
    qa, ka, va, oga = q_ref[...], k_ref[...], v_ref[...], og_ref[...]

    heads = range(NH)
    hsl = [slice(h * HD, (h + 1) * HD) for h in heads]
    q = [_pad_rows(qa[:, hsl[h]], C) for h in heads]
    k = [_pad_rows(ka[:, hsl[h]], C) * (HD ** -0.5) for h in heads]
    v = [_pad_rows(va[:, hsl[h]], C) for h in heads]
    qk = [_mm(q[h], k[h], NT) for h in heads]
    inter = [_mm(q[h], c_scr[h]) for h in heads]
    m, w_inter, sc, a, b, ms = [], [], [], [], [], []
    for h in heads:
        b.append(bcum[:, 3 * NH + h:3 * NH + h + 1])
        a.append(log_i[:, 2 * NH + h:2 * NH + h + 1] - b[h])
        arow = jnp.transpose(jnp.broadcast_to(a[h], (C, C)))
        cm = jnp.max(jnp.where(causal, arow, -jnp.inf), axis=-1, keepdims=True)
        ms.append(m_scr[h:h + 1, 0:1])
        m.append(jnp.maximum(b[h] + ms[h], b[h] + cm))
        w_inter.append(jnp.exp(b[h] + ms[h] - m[h]))
        sc.append(qk[h] * jnp.where(causal, jnp.exp(jnp.where(causal, b[h] + arow - m[h], 0.0)), 0.0))
    kw, d_end = [], []
    for h in heads:
        m_end = m[h][C - 1:C, :]
        b_end = b[h][C - 1:C, :]
        kw.append(k[h] * jnp.exp(b_end + a[h] - m_end))
        d_end.append(jnp.exp(b_end + ms[h] - m_end))
    from_v = [_mm(jnp.concatenate([sc[h], jnp.transpose(kw[h])], axis=0), v[h]) for h in heads]
    for h in heads:
        ns = n_scr[h:h + 1, :]
        num = w_inter[h] * inter[h] + from_v[h][:C]
        den = w_inter[h] * jnp.sum(q[h] * ns, axis=-1, keepdims=True) + jnp.sum(sc[h], axis=-1, keepdims=True)
        hc = num / (jnp.maximum(jnp.abs(den), jnp.exp(-m[h])) + MLSTM_EPS)
        c_scr[h] = d_end[h] * c_scr[h] + from_v[h][C:]
        n_scr[h:h + 1, :] = d_end[h] * ns + jnp.sum(kw[h], axis=0, keepdims=True)
        m_scr[h:h + 1, :] = jnp.broadcast_to(m[h][C - 1:C, :], (1, HD))
        o_ref[:, hsl[h]] = (_rms(hc[:c_in], ng) * jax.nn.sigmoid(oga[:, hsl[h]])).astype(bf16)

    @pl.when(c == n_chunks - 1)
    def _():
        co_ref[...] = c_scr[...]
        no_ref[...] = n_scr[...]
        mo_ref[...] = m_scr[...]


def _mlstm(z, row_off, nb, t, gate_bias, norm_g, c0=None, n0=None, m0=None):
    c_in = min(t, CHUNK)
    n_chunks = t // c_in
    has_state = c0 is not None
    rb = row_off // c_in
    w = NH * HD
    ib = jnp.zeros((1, HD), f32).at[0, 2 * NH:3 * NH].set(gate_bias[0].astype(f32))
    fb = jnp.zeros((1, HD), f32).at[0, 3 * NH:4 * NH].set(gate_bias[1].astype(f32))
    zspec = lambda j: pl.BlockSpec((c_in, w), lambda b, c: (rb + b * n_chunks + c, j))
    in_specs = [zspec(4), zspec(5), zspec(6), zspec(7),
                pl.BlockSpec((c_in, HD), lambda b, c: (rb + b * n_chunks + c, 64)),
                pl.BlockSpec((1, HD), lambda b, c: (0, 0)),
                pl.BlockSpec((1, HD), lambda b, c: (0, 0)),
                pl.BlockSpec((1, HD), lambda b, c: (0, 0))]
    args = [z, z, z, z, z, ib, fb, norm_g.reshape(1, HD)]
    if has_state:
        in_specs += [pl.BlockSpec((None, NH, HD, HD), lambda b, c: (b, 0, 0, 0)),
                     pl.BlockSpec((None, NH, HD), lambda b, c: (b, 0, 0)),
                     pl.BlockSpec((None, NH, HD), lambda b, c: (b, 0, 0))]
        args += [c0, n0, jnp.broadcast_to(m0[..., None], m0.shape + (HD,))]
    return pl.pallas_call(
        functools.partial(_mlstm_body, c_in=c_in, n_chunks=n_chunks, has_state=has_state),
        out_shape=(jax.ShapeDtypeStruct((nb * t, w), bf16), jax.ShapeDtypeStruct((nb, NH, HD, HD), f32),
                   jax.ShapeDtypeStruct((nb, NH, HD), f32), jax.ShapeDtypeStruct((nb, NH, HD), f32)),
        grid=(nb, n_chunks),
        in_specs=in_specs,
        out_specs=(pl.BlockSpec((c_in, w), lambda b, c: (b * n_chunks + c, 0)),
                   pl.BlockSpec((None, NH, HD, HD), lambda b, c: (b, 0, 0, 0)),
                   pl.BlockSpec((None, NH, HD), lambda b, c: (b, 0, 0)),
                   pl.BlockSpec((None, NH, HD), lambda b, c: (b, 0, 0))),
        scratch_shapes=[pltpu.VMEM((NH, HD, HD), f32), pltpu.VMEM((NH, HD), f32), pltpu.VMEM((NH, HD), f32)],
        compiler_params=_cparams(2),
        name="mlstm",
    )(*args)


def _block_masks(c, t):
    row, col = _tri_masks(c)
    same = (row // t) == (col // t)
    return row, col, same


def _gdns_body(u_ref, p_ref, gate_ref, sm_ref, cw_ref, alog_ref, dtb_ref, ng_ref, s0_ref, o_ref, so_ref, *, t, levels):
    C = CHUNK
    nseq = C // t
    u = u_ref[...]
    p = p_ref[...]
    cw = cw_ref[...]
    rowm = _iota2(u.shape, 0) % t
    acc = u * cw[3:4]
    for k in range(1, 4):
        sh = jnp.where(rowm < k, pltpu.roll(p, (k - t) % C, axis=0), pltpu.roll(u, k, axis=0))
        acc = acc + sh * cw[3 - k:4 - k]
    qkv = acc * jax.nn.sigmoid(acc)

    sm = sm_ref[...]
    gfull = -jnp.exp(alog_ref[...]) * _softplus(sm + dtb_ref[...])
    beta = jax.nn.sigmoid(sm)
    row, col, same = _block_masks(C, t)
    causal = (col <= row) & same
    strict = (col < row) & same
    eye = jnp.where(row == col, 1.0, 0.0).astype(f32)
    gc = _mm_exact_a(jnp.where(causal, 1.0, 0.0), gfull)
    gsum = _mm_exact_a(jnp.where(same, 1.0, 0.0), gfull)
    colseq = _iota2((HD, C), 1) // t
    gate = gate_ref[...]
    ng = ng_ref[...]

    for h in range(NH):
        q = qkv[:, h * HD:(h + 1) * HD]
        k = qkv[:, (NH + h) * HD:(NH + h + 1) * HD]
        v = qkv[:, (2 * NH + h) * HD:(2 * NH + h + 1) * HD]
        qn = q * lax.rsqrt(jnp.sum(q * q, axis=-1, keepdims=True) + 1e-6) * (HD ** -0.5)
        kn = k * lax.rsqrt(jnp.sum(k * k, axis=-1, keepdims=True) + 1e-6)
        gch = gc[:, h:h + 1]
        gend = gsum[:, h:h + 1]
        bh = beta[:, NH + h:NH + h + 1]
        gcb = jnp.broadcast_to(gch, (C, C))
        decay = jnp.where(causal, jnp.exp(jnp.where(causal, gcb - jnp.transpose(gcb), 0.0)), 0.0)
        eg = jnp.exp(gch)
        kb = kn * bh
        kq = _mm(jnp.concatenate([kb, qn], axis=0), kn, NT)
        a_mat = jnp.where(strict, kq[:C] * decay, 0.0)
        (x_inv,) = _tri_inverse([a_mat], eye, levels)
        (sol,) = _mm_split(x_inv, [jnp.concatenate([v * bh, kb * eg], axis=1)])
        qe = qn * eg
        u_parts, o_parts = [], []
        for b in range(nseq):
            rs = slice(b * t, (b + 1) * t)
            r2 = _mm(jnp.concatenate([sol[rs, HD:], qe[rs]], axis=0), s0_ref[b, h])
            u_parts.append(sol[rs, :HD] - r2[:t])
            o_parts.append(r2[t:])
        uu = jnp.concatenate(u_parts, axis=0)
        o = jnp.concatenate(o_parts, axis=0) + _mm(kq[C:] * decay, uu)
        kdec_t = jnp.transpose(kn * jnp.exp(gend - gch))
        for b in range(nseq):
            so_ref[b, h] = (s0_ref[b, h] * jnp.exp(gend[b * t:b * t + 1, :])
                            + _mm(jnp.where(colseq == b, kdec_t, 0.0), uu))
        gt = gate[:, h * HD:(h + 1) * HD]
        o_ref[:, h * HD:(h + 1) * HD] = (_rms(o, ng) * (gt * jax.nn.sigmoid(gt))).astype(bf16)


def _gdn_decode(z, row_off, nb, t, conv_w, a_log, dt_bias, norm_g, s0, prev):
    nseq = CHUNK // t
    assert t == 8 and nb % nseq == 0 and row_off % CHUNK == 0
    rb = row_off // CHUNK
    w3 = 3 * NH * HD
    pad128 = lambda v: jnp.zeros((1, HD), f32).at[0, :NH].set(v.astype(f32))
    return pl.pallas_call(
        functools.partial(_gdns_body, t=t, levels=int(math.log2(t))),
        out_shape=(jax.ShapeDtypeStruct((nb * t, NH * HD), bf16), jax.ShapeDtypeStruct((nb, NH, HD, HD), f32)),
        grid=(nb // nseq,),
        in_specs=[pl.BlockSpec((CHUNK, w3), lambda i: (rb + i, 0)),
                  pl.BlockSpec((CHUNK, w3), lambda i: (i, 0)),
                  pl.BlockSpec((CHUNK, NH * HD), lambda i: (rb + i, 3)),
                  pl.BlockSpec((CHUNK, HD), lambda i: (rb + i, 64)),
                  pl.BlockSpec((4, w3), lambda i: (0, 0)),
                  pl.BlockSpec((1, HD), lambda i: (0, 0)),
                  pl.BlockSpec((1, HD), lambda i: (0, 0)),
                  pl.BlockSpec((1, HD), lambda i: (0, 0)),
                  pl.BlockSpec((nseq, NH, HD, HD), lambda i: (i, 0, 0, 0))],
        out_specs=(pl.BlockSpec((CHUNK, NH * HD), lambda i: (i, 0)),
                   pl.BlockSpec((nseq, NH, HD, HD), lambda i: (i, 0, 0, 0))),
        compiler_params=_cparams(1),
        name="gdn_decode",
    )(z, prev.reshape(nb * t, w3), z, z, conv_w, pad128(a_log), pad128(dt_bias), norm_g.reshape(1, HD), s0)


def _mlstms_body(q_ref, k_ref, v_ref, og_ref, sm_ref, ib_ref, fb_ref, ng_ref, c0_ref, n0_ref, m0_ref,
                 o_ref, co_ref, no_ref, mo_ref, *, t):
    C = CHUNK
    nseq = C // t
    sm = sm_ref[...]
    log_i = sm + ib_ref[...]
    log_f = _log_sigmoid(sm + fb_ref[...])
    m0r = m0_ref[...]
    row, col, same = _block_masks(C, t)
    causal = (col <= row) & same
    bcum = _mm_exact_a(jnp.where(causal, 1.0, 0.0), log_f)
    last_of_row = jnp.where(col == (row // t) * t + (t - 1), 1.0, 0.0)
    last_of_seq = jnp.where(_iota2((nseq, C), 1) == _iota2((nseq, C), 0) * t + (t - 1), 1.0, 0.0)
    seq_rows = jnp.where(_iota2((nseq, C), 1) // t == _iota2((nseq, C), 0), 1.0, 0.0)
    own_seq = _iota2((C, nseq), 1) == _iota2((C, nseq), 0) // t
    colseq = _iota2((HD, C), 1) // t
    lane = _iota2((C, HD), 1)
    ng = ng_ref[...]
    qa, ka, va, oga = q_ref[...], k_ref[...], v_ref[...], og_ref[...]

    for h in range(NH):
        sl = slice(h * HD, (h + 1) * HD)
        q = qa[:, sl]
        k = ka[:, sl] * (HD ** -0.5)
        v = va[:, sl]
        b = bcum[:, 3 * NH + h:3 * NH + h + 1]
        a = log_i[:, 2 * NH + h:2 * NH + h + 1] - b
        arow = jnp.transpose(jnp.broadcast_to(a, (C, C)))
        cm = jnp.max(jnp.where(causal, arow, -jnp.inf), axis=-1, keepdims=True)
        ms = m0r[:, h:h + 1]
        m = jnp.maximum(b + ms, b + cm)
        w_inter = jnp.exp(b + ms - m)
        dmat = jnp.where(causal, jnp.exp(jnp.where(causal, b + arow - m, 0.0)), 0.0)
        sc = _mm(q, k, NT) * dmat
        ends = _mm_exact_a(last_of_row, jnp.where(lane == 0, b, jnp.where(lane == 1, m, 0.0)))
        b_end, m_end = ends[:, 0:1], ends[:, 1:2]
        w_end = jnp.exp(b_end + a - m_end)
        d_end = jnp.exp(b_end + ms - m_end)
        kw = k * w_end
        kw_t = jnp.transpose(kw)
        nh = n0_ref[h]
        qn = jnp.sum(jnp.where(own_seq, _mm(q, nh, NT), 0.0), axis=-1, keepdims=True)
        inter = jnp.concatenate([_mm(q[b_ * t:(b_ + 1) * t], c0_ref[b_, h]) for b_ in range(nseq)], axis=0)
        num = w_inter * inter + _mm(sc, v)
        den = w_inter * qn + jnp.sum(sc, axis=-1, keepdims=True)
        hc = num / (jnp.maximum(jnp.abs(den), jnp.exp(-m)) + MLSTM_EPS)
        for b_ in range(nseq):
            co_ref[b_, h] = d_end[b_ * t:b_ * t + 1, :] * c0_ref[b_, h] + _mm(jnp.where(colseq == b_, kw_t, 0.0), v)
        per_seq = _mm_exact_a(last_of_seq, jnp.where(lane == 0, d_end, jnp.where(lane == 1, m_end, 0.0)))
        no_ref[h] = per_seq[:, 0:1] * nh + _mm_exact_a(seq_rows, kw)
        mo_ref[h] = jnp.broadcast_to(per_seq[:, 1:2], (nseq, HD))
        o_ref[:, sl] = (_rms(hc, ng) * jax.nn.sigmoid(oga[:, sl])).astype(bf16)


def _mlstm_decode(z, row_off, nb, t, gate_bias, norm_g, c0, n0, m0):
    nseq = CHUNK // t
    assert nb % nseq == 0 and row_off % CHUNK == 0
    rb = row_off // CHUNK
    w = NH * HD
    ib = jnp.zeros((1, HD), f32).at[0, 2 * NH:3 * NH].set(gate_bias[0].astype(f32))
    fb = jnp.zeros((1, HD), f32).at[0, 3 * NH:4 * NH].set(gate_bias[1].astype(f32))
    m_rows = jnp.zeros((nb, t, HD), f32).at[:, :, :NH].set(jnp.broadcast_to(m0[:, None, :], (nb, t, NH))).reshape(nb * t, HD)
    zspec = lambda j: pl.BlockSpec((CHUNK, w), lambda i: (rb + i, j))
    hspec = pl.BlockSpec((NH, nseq, HD), lambda i: (0, i, 0))
    out, c_new, n_new, m_new = pl.pallas_call(
        functools.partial(_mlstms_body, t=t),
        out_shape=(jax.ShapeDtypeStruct((nb * t, w), bf16), jax.ShapeDtypeStruct((nb, NH, HD, HD), f32),
                   jax.ShapeDtypeStruct((NH, nb, HD), f32), jax.ShapeDtypeStruct((NH, nb, HD), f32)),
        grid=(nb // nseq,),
        in_specs=[zspec(4), zspec(5), zspec(6), zspec(7),
                  pl.BlockSpec((CHUNK, HD), lambda i: (rb + i, 64)),
                  pl.BlockSpec((1, HD), lambda i: (0, 0)),
                  pl.BlockSpec((1, HD), lambda i: (0, 0)),
                  pl.BlockSpec((1, HD), lambda i: (0, 0)),
                  pl.BlockSpec((nseq, NH, HD, HD), lambda i: (i, 0, 0, 0)),
                  hspec,
                  pl.BlockSpec((CHUNK, HD), lambda i: (i, 0))],
        out_specs=(pl.BlockSpec((CHUNK, w), lambda i: (i, 0)),
                   pl.BlockSpec((nseq, NH, HD, HD), lambda i: (i, 0, 0, 0)), hspec, hspec),
        compiler_params=_cparams(1),
        name="mlstm_decode",
    )(z, z, z, z, z, ib, fb, norm_g.reshape(1, HD), c0, jnp.swapaxes(n0, 0, 1), m_rows)
    return out, c_new, jnp.swapaxes(n_new, 0, 1), jnp.swapaxes(m_new, 0, 1)


_EVEN_SEGMENT_ORDER = (0, 3, 4, 5, 6, 9, 1, 2, 7, 8)


def _even_offsets():
    w = NH * HD
    return np.cumsum([0, 3 * w, NH, NH, w, w, w, w, NH, NH, w])


def _even_perm():
    offs = _even_offsets()
    return np.concatenate([np.arange(offs[i], offs[i + 1]) for i in _EVEN_SEGMENT_ORDER])


def _even_reorder(w_in):
    offs = _even_offsets()
    return jnp.concatenate([w_in[:, offs[i]:offs[i + 1]] for i in _EVEN_SEGMENT_ORDER], axis=1)


SCALE = HD ** -0.5


def _kvnorm_body(sk_ref, wk_ref, g_ref, so_ref, wo_ref):
    g = g_ref[...]
    for src, dst, gi in ((sk_ref, so_ref, 1), (wk_ref, wo_ref, 2)):
        x = src[...]
        for j in range(NKV):
            dst[:, j * HD:(j + 1) * HD] = _rms(x[:, j * HD:(j + 1) * HD], g[gi:gi + 1])


def _kvnorm(z, k_g, *, tm=1024):
    m = z.shape[0]
    w = NKV * HD
    return pl.pallas_call(
        _kvnorm_body,
        out_shape=(jax.ShapeDtypeStruct((m, w), f32), jax.ShapeDtypeStruct((m, w), f32)),
        grid=(m // tm,),
        in_specs=[pl.BlockSpec((tm, w), lambda i: (i, 18)), pl.BlockSpec((tm, w), lambda i: (i, 20)),
                  pl.BlockSpec((3, HD), lambda i: (0, 0))],
        out_specs=(pl.BlockSpec((tm, w), lambda i: (i, 0)), pl.BlockSpec((tm, w), lambda i: (i, 0))),
        compiler_params=_cparams(1),
        name="kvnorm",
    )(z, z, k_g)


SBP_HEADS = 4
SBP_QB = 256


def _sb_block_multi(zs, mask, mstrict, rs):
    n = zs[0].shape[0]
    log_betas, log_rests, parts = [], [], []
    for z in zs:
        sp = jnp.log1p(jnp.exp(-jnp.abs(z)))
        log_betas.append(jnp.minimum(z, 0.0) - sp)
        lr = -(jnp.maximum(z, 0.0) + sp)
        if mask is not None:
            lr = jnp.where(mask, lr, 0.0)
        log_rests.append(lr)
        parts.extend(_split2(lr))
    after = jnp.dot(jnp.concatenate(parts, axis=0), mstrict, preferred_element_type=f32)
    atts, new_rs = [], []
    for h, (lb, lr, r) in enumerate(zip(log_betas, log_rests, rs)):
        att = jnp.exp(lb + after[2 * h * n:(2 * h + 1) * n] + after[(2 * h + 1) * n:(2 * h + 2) * n] + r)
        atts.append(att if mask is None else jnp.where(mask, att, 0.0))
        new_rs.append(r + jnp.sum(lr, axis=-1, keepdims=True))
    return atts, new_rs


def _sbp_body(q_ref, k_ref, v_ref, o_ref):
    i = pl.program_id(2)
    row = _iota2((SBP_QB, HD), 0)
    col = _iota2((SBP_QB, HD), 1)
    mstrict = jnp.where(_iota2((HD, HD), 0) > _iota2((HD, HD), 1), 1.0, 0.0).astype(bf16)
    qs = [q_ref[:, h * HD:(h + 1) * HD].astype(bf16) for h in range(SBP_HEADS)]
    n_kb = (i + 1) * (SBP_QB // HD)

    def block(j, carry, masked):
        off = pl.multiple_of(j * HD, HD)
        mask = ((col + j * HD) < (row + i * SBP_QB)) if masked else None
        zs = [_mm(qs[h], k_ref[pl.ds(off, HD), h * HD:(h + 1) * HD], NT) * SCALE for h in range(SBP_HEADS)]
        atts, rs = _sb_block_multi(zs, mask, mstrict, carry[SBP_HEADS:])
        accs = [carry[h] + _mm(atts[h], v_ref[pl.ds(off, HD), h * HD:(h + 1) * HD]) for h in range(SBP_HEADS)]
        return tuple(accs) + tuple(rs)

    n_diag = SBP_QB // HD
    carry = tuple([jnp.zeros((SBP_QB, HD), f32)] * SBP_HEADS + [jnp.zeros((SBP_QB, 1), f32)] * SBP_HEADS)
    for d in range(n_diag):
        carry = block(n_kb - 1 - d, carry, True)
    res = lax.fori_loop(0, n_kb - n_diag, lambda s, c: block(n_kb - n_diag - 1 - s, c, False), carry)
    for h in range(SBP_HEADS):
        o_ref[:, h * HD:(h + 1) * HD] = res[h].astype(bf16)


def _sb_prompt(z, nb, t):
    nq = t // SBP_QB
    w = SBP_HEADS * HD
    ng = NH // SBP_HEADS
    return pl.pallas_call(
        _sbp_body,
        out_shape=jax.ShapeDtypeStruct((nb * t, NH * HD), bf16),
        grid=(nb, ng, nq),
        in_specs=[pl.BlockSpec((SBP_QB, w), lambda b, h, i: (b * nq + i, h)),
                  pl.BlockSpec((t, w), lambda b, h, i: (b, ng + h)),
                  pl.BlockSpec((t, w), lambda b, h, i: (b, 2 * ng + h))],
        out_specs=pl.BlockSpec((SBP_QB, w), lambda b, h, i: (b * nq + i, h)),
        compiler_params=_cparams(3),
        name="sb_prompt",
    )(z, z, z)


SBS_PAGES = 16


def _head_rows(page_ref, h, n_heads):
    return page_ref[pl.ds(h, PAGE, stride=n_heads), :]


def _pages2d(cache):
    return cache.reshape(cache.shape[0], cache.shape[1] * cache.shape[2], cache.shape[3])


def _sbs_body(pt_ref, q_ref, kn_ref, vn_ref, *rest, n_steps, t):
    del pt_ref
    kp_refs, vp_refs = rest[:SBS_PAGES], rest[SBS_PAGES:2 * SBS_PAGES]
    o_ref, acc_scr, r_scr = rest[2 * SBS_PAGES:]
    s = pl.program_id(1)

    @pl.when(s == 0)
    def _():
        acc_scr[...] = jnp.zeros_like(acc_scr)
        r_scr[...] = jnp.zeros_like(r_scr)

    rows = NH * t
    row = _iota2((rows, HD), 0)
    col = _iota2((rows, HD), 1)
    mrow = _iota2((HD, HD), 0)
    mcol = _iota2((HD, HD), 1)
    mstrict = jnp.where(mrow > mcol, 1.0, 0.0).astype(bf16)
    qa = q_ref[...]

    def load():
        return r_scr[:, 0:1], [acc_scr[h * t:(h + 1) * t, :] for h in range(NH)]

    def store(r, accs):
        r_scr[...] = jnp.broadcast_to(r, r_scr.shape)
        for h in range(NH):
            acc_scr[h * t:(h + 1) * t, :] = accs[h]

    @pl.when(s == 0)
    def _():
        r, accs = load()
        z = jnp.concatenate([_mm(qa[:, h * HD:(h + 1) * HD], _pad_rows(kn_ref[:, h * HD:(h + 1) * HD], HD), NT)
                             for h in range(NH)], axis=0) * SCALE
        (att,), (r,) = _sb_block_multi([z], col < (row % t), mstrict, [r])
        store(r, [accs[h] + _mm(att[h * t:(h + 1) * t], _pad_rows(vn_ref[:, h * HD:(h + 1) * HD], HD))
                  for h in range(NH)])

    slots = range(SBS_PAGES - 1, -1, -1)
    r, accs = load()
    z = jnp.concatenate(
        [_mm(qa[:, h * HD:(h + 1) * HD],
             jnp.concatenate([_head_rows(kp_refs[jj], h, NH).astype(bf16) for jj in slots], axis=0), NT)
         for h in range(NH)], axis=0) * SCALE
    sp = jnp.log1p(jnp.exp(-jnp.abs(z)))
    log_beta = jnp.minimum(z, 0.0) - sp
    log_rest = -(jnp.maximum(z, 0.0) + sp)
    blocks = [log_rest[:, c * HD:(c + 1) * HD] for c in range(SBS_PAGES)]
    after = jnp.dot(jnp.concatenate([x for blk in blocks for x in _split2(blk)], axis=0), mstrict,
                    preferred_element_type=f32)
    atts = [None] * SBS_PAGES
    for c in range(SBS_PAGES - 1, -1, -1):
        aft = after[2 * c * rows:(2 * c + 1) * rows] + after[(2 * c + 1) * rows:(2 * c + 2) * rows]
        atts[c] = jnp.exp(log_beta[:, c * HD:(c + 1) * HD] + aft + r)
        r = r + jnp.sum(blocks[c], axis=-1, keepdims=True)
    att = jnp.concatenate(atts, axis=1)
    store(r, [accs[h] + _mm(att[h * t:(h + 1) * t],
                            jnp.concatenate([_head_rows(vp_refs[jj], h, NH).astype(bf16) for jj in slots], axis=0))
              for h in range(NH)])

    @pl.when(s == n_steps - 1)
    def _():
        for h in range(NH):
            o_ref[:, h * HD:(h + 1) * HD] = acc_scr[h * t:(h + 1) * t, :].astype(bf16)


def _sb_sample(z, row_off, nb, t, cache_k, cache_v, page_table):
    n_pages = page_table.shape[1]
    n_steps = n_pages // SBS_PAGES
    rb = row_off // t
    page = lambda jj: pl.BlockSpec((None, PAGE * NH, HD),
                                   lambda b, s, pt: (pt[b, n_pages - 1 - (s * SBS_PAGES + jj)], 0, 0))
    pages = [page(jj) for jj in range(SBS_PAGES)]
    grid_spec = pltpu.PrefetchScalarGridSpec(
        num_scalar_prefetch=1,
        grid=(nb, n_steps),
        in_specs=[pl.BlockSpec((t, NH * HD), lambda b, s, pt: (rb + b, 0)),
                  pl.BlockSpec((t, NH * HD), lambda b, s, pt: (rb + b, 1)),
                  pl.BlockSpec((t, NH * HD), lambda b, s, pt: (rb + b, 2))] + pages + pages,
        out_specs=pl.BlockSpec((t, NH * HD), lambda b, s, pt: (b, 0)),
        scratch_shapes=[pltpu.VMEM((NH * t, HD), f32), pltpu.VMEM((NH * t, HD), f32)],
    )
    return pl.pallas_call(
        functools.partial(_sbs_body, n_steps=n_steps, t=t),
        out_shape=jax.ShapeDtypeStruct((nb * t, NH * HD), bf16),
        grid_spec=grid_spec,
        compiler_params=_cparams(2),
        name="sb_sample",
    )(page_table, z, z, z, *([_pages2d(cache_k)] * SBS_PAGES), *([_pages2d(cache_v)] * SBS_PAGES))


def _t5_bucket(dist):
    n = jnp.maximum(dist, 0)
    exact = N_BUCKETS // 2
    nf = jnp.maximum(n, 1).astype(f32)
    large = exact + (jnp.log(nf / exact) / math.log(MAX_DISTANCE / exact) * (N_BUCKETS - exact)).astype(i32)
    return jnp.where(n < exact, n, jnp.minimum(large, N_BUCKETS - 1))


def _bias_of_bucket(bucket, rel_ref, h):
    vals = [rel_ref[k, h] for k in range(N_BUCKETS)]
    bit = 1
    while len(vals) > 1:
        on = (bucket & bit) != 0
        vals = [jnp.where(on, vals[2 * n + 1], vals[2 * n]) for n in range(len(vals) // 2)]
        bit *= 2
    return vals[0]


def _softmax_step(m, l, acc, logits, mask, v):
    lg = jnp.where(mask, logits, NEG)
    m_new = jnp.maximum(m, jnp.max(lg, axis=-1, keepdims=True))
    alpha = jnp.exp(m - m_new)
    e = jnp.where(mask, jnp.exp(lg - m_new), 0.0)
    return m_new, alpha * l + jnp.sum(e, axis=-1, keepdims=True), alpha * acc + _mm(e, v)


def _softmax_full(logits, mask):
    lg = jnp.where(mask, logits, NEG)
    e = jnp.where(mask, jnp.exp(lg - jnp.max(lg, axis=-1, keepdims=True)), 0.0)
    return e / jnp.maximum(jnp.sum(e, axis=-1, keepdims=True), 1e-30)


def _block_scores(imp_sel, cur, n_sel):
    rows = imp_sel.shape[0]
    imp_pad = jnp.concatenate([imp_sel, jnp.zeros((rows, HD - imp_sel.shape[1]), f32)], axis=1)
    blk = _iota2((rows, HD), 1)
    forced = jnp.where(blk == cur, 2.0 * FORCE_SCORE, jnp.where(blk == 0, FORCE_SCORE, -FORCE_SCORE))
    score = jnp.where((blk < cur) & (blk > 0), imp_pad, forced)
    return jnp.where(blk < n_sel, score, -jnp.inf), blk


def _select_blocks(imp_sel, cur, n_sel):
    score, blk = _block_scores(imp_sel, cur, n_sel)
    sel = jnp.zeros(score.shape, f32)
    for _ in range(min(TOP_N, n_sel)):
        mx = jnp.max(score, axis=-1, keepdims=True)
        idx = jnp.min(jnp.where(score == mx, blk, 1 << 30), axis=-1, keepdims=True)
        hit = blk == idx
        sel = jnp.where(hit, 1.0, sel)
        score = jnp.where(hit, -jnp.inf, score)
    return sel


def _select_blocks_by_rank(imp_sel, cur, n_sel):
    score, _ = _block_scores(imp_sel, cur, n_sel)
    rows = score.shape[0]
    score_t = jnp.transpose(_pad_rows(score, HD, -jnp.inf))
    j_idx = _iota2((HD, HD), 0)
    i_idx = _iota2((HD, HD), 1)
    out = []
    for r in range(rows):
        s_j = score_t[:, r:r + 1]
        s_i = score[r:r + 1, :]
        ahead = (s_j > s_i) | ((s_j == s_i) & (j_idx < i_idx))
        rank = jnp.sum(jnp.where(ahead, 1.0, 0.0), axis=0, keepdims=True)
        out.append(jnp.where(rank < min(TOP_N, n_sel), 1.0, 0.0))
    return jnp.concatenate(out, axis=0)


def _compress(get_rows, cw_ref, cb, ckg, n_groups_rows):
    del n_groups_rows
    xk = jnp.concatenate([get_rows(0, t).astype(bf16) for t in range(CMP_BLOCK)], axis=1)
    xv = jnp.concatenate([get_rows(1, t).astype(bf16) for t in range(CMP_BLOCK)], axis=1)
    return _rms(_mm(xk, cw_ref[0]) + cb[0:1], ckg), _mm(xv, cw_ref[1]) + cb[1:2]


def _cmp_order(n_cmp, shape, axis):
    c = _iota2(shape, axis)
    half = n_cmp // 2
    return 2 * (c % half) + c // half


def _nsap_body(q_ref, gt_ref, ck0_ref, ck1_ref, cv0_ref, cv1_ref, sk_ref, sv_ref, wk_ref, wv_ref, cw_ref, cb_ref, ckg_ref,
               qg_ref, rel_ref, o_ref, kc_scr, vc_scr, bias_scr, *, t):
    b = pl.program_id(0)
    i = pl.program_id(1)
    n_cmp = t // CMP_BLOCK
    half = n_cmp // 2
    n_sel = t // SEL_BLOCK
    QB = HD
    cmp_refs = ((ck0_ref, ck1_ref), (cv0_ref, cv1_ref))

    @pl.when(i == 0)
    def _():
        for g in range(NKV):
            def get_rows(kind, tt, g=g):
                ref = cmp_refs[kind][g]
                return jnp.concatenate([ref[pl.ds(tt, half, stride=2 * CMP_BLOCK), :],
                                        ref[pl.ds(CMP_BLOCK + tt, half, stride=2 * CMP_BLOCK), :]], axis=0)
            kc, vc = _compress(get_rows, cw_ref, cb_ref[...], ckg_ref[...], n_cmp)
            kc_scr[g] = kc
            vc_scr[g] = vc

    @pl.when((b == 0) & (i == 0))
    def _():
        r_ = _iota2((QB, QB), 0)
        c_ = _iota2((QB, QB), 1)
        for kk in range(3):
            bucket = _t5_bucket(r_ - c_ + QB * kk)
            for h in range(NH):
                bias_scr[kk, h] = _bias_of_bucket(bucket, rel_ref, h)

    qall = q_ref[...]
    qg = qg_ref[...]
    qs = [jnp.concatenate([_rms(qall[:, (g * NREP + r) * HD:(g * NREP + r + 1) * HD], qg) for r in range(NREP)], axis=0)
          for g in range(NKV)]
    rep = lambda x: jnp.concatenate([x] * NREP, axis=0)

    qpos_c = i * QB + _iota2((QB, n_cmp), 0)
    dist_c = qpos_c - (_cmp_order(n_cmp, (QB, n_cmp), 1) * CMP_BLOCK + CMP_BLOCK - 1)
    bucket_c = _t5_bucket(dist_c)
    mask_c = rep(dist_c >= 0)
    cur = (i * QB + _iota2((QB, 1), 0)) // SEL_BLOCK
    o_cmp, imps = [], []
    for g in range(NKV):
        bias = jnp.concatenate([_bias_of_bucket(bucket_c, rel_ref, g * NREP + r) for r in range(NREP)], axis=0)
        p = _softmax_full(_mm(qs[g], kc_scr[g], NT) * SCALE + bias, mask_c)
        o_cmp.append(_mm(p, vc_scr[g]))
        imp = p[0:QB] + p[QB:2 * QB] + p[2 * QB:3 * QB] + p[3 * QB:4 * QB]
        imps.append(imp[:, :half] + imp[:, half:])
    sel_all = _select_blocks(jnp.concatenate(imps, axis=0), jnp.concatenate([cur] * NKV, axis=0), n_sel)
    sels = [sel_all[g * QB:(g + 1) * QB] for g in range(NKV)]

    KP = 2 * QB
    krow = _iota2((QB, KP), 0)
    kcol = _iota2((QB, KP), 1)

    def init():
        return tuple(x for _ in range(NKV) for x in (jnp.full((NREP * QB, 1), NEG, f32), jnp.zeros((NREP * QB, 1), f32),
                                                     jnp.zeros((NREP * QB, HD), f32)))

    def attend(carry, p, k_ref, v_ref, mask_of):
        off = pl.multiple_of(p * KP, KP)
        dist = i * QB + krow - (p * KP + kcol)
        d0 = jnp.clip(i - 2 * p, 0, 2)
        d1 = jnp.clip(i - 2 * p - 1, 0, 2)
        out = []
        for g in range(NKV):
            m, l, acc = carry[3 * g:3 * g + 3]
            bias = jnp.concatenate([jnp.concatenate([bias_scr[d0, g * NREP + r], bias_scr[d1, g * NREP + r]], axis=1)
                                    for r in range(NREP)], axis=0)
            logits = _mm(qs[g], k_ref[pl.ds(off, KP), g * HD:(g + 1) * HD], NT) * SCALE + bias
            out.extend(_softmax_step(m, l, acc, logits, rep(mask_of(g, p, dist)), v_ref[pl.ds(off, KP), g * HD:(g + 1) * HD]))
        return tuple(out)

    def sel_mask(g, p, dist):
        er = _iota2((HD, KP), 0)
        ec = _iota2((HD, KP), 1)
        expand = jnp.where(er == (KP // SEL_BLOCK) * p + ec // SEL_BLOCK, 1.0, 0.0)
        return (_mm(sels[g], expand) > 0.5) & (dist >= 0)

    def win_mask(g, p, dist):
        return (dist >= 0) & (dist < WINDOW)

    last_pair = i // 2
    first_win = jnp.maximum(i - WINDOW // QB, 0) // 2
    c_sel = lax.fori_loop(0, last_pair + 1, lambda p, c: attend(c, p, sk_ref, sv_ref, sel_mask), init())
    c_win = lax.fori_loop(first_win, last_pair + 1, lambda p, c: attend(c, p, wk_ref, wv_ref, win_mask), init())

    gts = jax.nn.sigmoid(gt_ref[...])
    for h in range(NH):
        g, r = divmod(h, NREP)
        sl = slice(r * QB, (r + 1) * QB)
        o_s = c_sel[3 * g + 2][sl] / jnp.maximum(c_sel[3 * g + 1][sl], 1e-30)
        o_w = c_win[3 * g + 2][sl] / jnp.maximum(c_win[3 * g + 1][sl], 1e-30)
        o = gts[:, 3 * h:3 * h + 1] * o_cmp[g][sl] + gts[:, 3 * h + 1:3 * h + 2] * o_s + gts[:, 3 * h + 2:3 * h + 3] * o_w
        o_ref[:, h * HD:(h + 1) * HD] = o.astype(bf16)


def _nsa_prompt(z, sk, wk, nb, t, cmp_w, cmp_b, cmp_k_g, q_g, rel_bias):
    nq = t // HD
    n_cmp = t // CMP_BLOCK
    w = NKV * HD
    full = lambda j: pl.BlockSpec((t, w), lambda b, i: (b, j))
    head = lambda j: pl.BlockSpec((t, HD), lambda b, i: (b, j))
    return pl.pallas_call(
        functools.partial(_nsap_body, t=t),
        out_shape=jax.ShapeDtypeStruct((nb * t, NH * HD), bf16),
        grid=(nb, nq),
        in_specs=[pl.BlockSpec((HD, NH * HD), lambda b, i: (b * nq + i, 3)),
                  pl.BlockSpec((HD, HD), lambda b, i: (b * nq + i, 44)),
                  head(32), head(33), head(34), head(35), full(0), full(19), full(0), full(21),
                  pl.BlockSpec((2, CMP_BLOCK * HD, HD), lambda b, i: (0, 0, 0)),
                  pl.BlockSpec((2, HD), lambda b, i: (0, 0)),
                  pl.BlockSpec((1, HD), lambda b, i: (0, 0)),
                  pl.BlockSpec((1, HD), lambda b, i: (0, 0)),
                  pl.BlockSpec(memory_space=pltpu.SMEM)],
        out_specs=pl.BlockSpec((HD, NH * HD), lambda b, i: (b * nq + i, 0)),
        scratch_shapes=[pltpu.VMEM((NKV, n_cmp, HD), f32), pltpu.VMEM((NKV, n_cmp, HD), f32),
                        pltpu.VMEM((3, NH, HD, HD), f32)],
        compiler_params=_cparams(2),
        name="nsa_prompt",
    )(z, z, z, z, z, z, sk, z, wk, z, cmp_w.astype(bf16), cmp_b, cmp_k_g.reshape(1, HD), q_g.reshape(1, HD), rel_bias)


def _nsa_q_groups(q_ref, qg):
    qall = q_ref[...]
    return [jnp.concatenate([_rms(qall[:, (g * NREP + r) * HD:(g * NREP + r + 1) * HD], qg) for r in range(NREP)], axis=0)
            for g in range(NKV)]


CMP_PAGES = 16


def _nsasa_body(pt_ref, q_ref, *rest, n_pages, t):
    del pt_ref
    ckp_refs, cvp_refs = rest[:CMP_PAGES], rest[CMP_PAGES:2 * CMP_PAGES]
    cw_ref, cb_ref, ckg_ref, qg_ref, rel_ref, oc_ref, sel_ref, xk_scr, xv_scr = rest[2 * CMP_PAGES:]
    p = pl.program_id(1)
    pp = 2 * PAGE
    pr_ = _iota2((pp, pp), 0)
    pc_ = _iota2((pp, pp), 1)
    perm = jnp.where(pc_ == (pr_ % 8) * CMP_BLOCK + pr_ // 8, 1.0, 0.0).astype(bf16)
    for pair in range(CMP_PAGES // 2):
        row0 = pl.multiple_of((p * (CMP_PAGES // 2) + pair) * 8, 8)
        targets = [(refs, scr, g) for refs, scr in ((ckp_refs, xk_scr), (cvp_refs, xv_scr)) for g in range(NKV)]
        x2 = jnp.concatenate(
            [jnp.concatenate([_head_rows(refs[2 * pair], g, NKV), _head_rows(refs[2 * pair + 1], g, NKV)], axis=0)
             for refs, _, g in targets], axis=1).astype(bf16)
        moved = jnp.dot(perm, x2, preferred_element_type=f32)
        for n, (_, scr, g) in enumerate(targets):
            for tt in range(CMP_BLOCK):
                scr[g, pl.ds(row0, 8), tt * HD:(tt + 1) * HD] = moved[8 * tt:8 * tt + 8, n * HD:(n + 1) * HD]

    @pl.when(p == n_pages // CMP_PAGES - 1)
    def _():
        past = n_pages * PAGE
        n_cmp = (past + t) // CMP_BLOCK
        n_sel = -(-(past + t) // SEL_BLOCK)
        cb = cb_ref[...]
        kc = _rms(_mm(jnp.concatenate([xk_scr[g] for g in range(NKV)], axis=0), cw_ref[0]) + cb[0:1], ckg_ref[...])
        vc = _mm(jnp.concatenate([xv_scr[g] for g in range(NKV)], axis=0), cw_ref[1]) + cb[1:2]
        qs = _nsa_q_groups(q_ref, qg_ref[...])
        rows = NREP * t
        tq = _iota2((rows, n_cmp), 0) % t
        dist_c = past + tq - (_iota2((rows, n_cmp), 1) * CMP_BLOCK + CMP_BLOCK - 1)
        bucket_c = _t5_bucket(dist_c)
        mask_c = dist_c >= 0
        cur = (past + _iota2((t, 1), 0)) // SEL_BLOCK
        ratio = SEL_BLOCK // CMP_BLOCK
        pair_sum = jnp.where(_iota2((n_cmp, n_cmp // ratio), 0) // ratio == _iota2((n_cmp, n_cmp // ratio), 1), 1.0, 0.0)
        imps = []
        for g in range(NKV):
            bias = jnp.concatenate([_bias_of_bucket(bucket_c[r * t:(r + 1) * t], rel_ref, g * NREP + r)
                                    for r in range(NREP)], axis=0)
            pr = _softmax_full(_mm(qs[g], kc[g * n_cmp:(g + 1) * n_cmp], NT) * SCALE + bias, mask_c)
            oc_ref[g * rows:(g + 1) * rows, :] = _mm(pr, vc[g * n_cmp:(g + 1) * n_cmp])
            imps.append(pr[0:t] + pr[t:2 * t] + pr[2 * t:3 * t] + pr[3 * t:4 * t])
        imp_sel = _mm_exact_b(jnp.concatenate(imps, axis=0), pair_sum)
        sel_ref[...] = _select_blocks_by_rank(imp_sel, jnp.concatenate([cur] * NKV, axis=0), n_sel)


def _nsa_sample_cmp(z, row_off, nb, t, cache_ck, cache_cv, page_table, cmp_w, cmp_b, cmp_k_g, q_g, rel_bias):
    n_pages = page_table.shape[1]
    assert n_pages % CMP_PAGES == 0 and CMP_PAGES % 2 == 0 and t < CMP_BLOCK
    rb = row_off // t
    page = lambda jj: pl.BlockSpec((None, PAGE * NKV, HD), lambda b, p, pt: (pt[b, p * CMP_PAGES + jj], 0, 0))
    pages = [page(jj) for jj in range(CMP_PAGES)]
    cst = lambda *shape: pl.BlockSpec(shape, lambda b, p, pt: (0,) * len(shape))
    grid_spec = pltpu.PrefetchScalarGridSpec(
        num_scalar_prefetch=1,
        grid=(nb, n_pages // CMP_PAGES),
        in_specs=[pl.BlockSpec((t, NH * HD), lambda b, p, pt: (rb + b, 3))] + pages + pages + [
                  cst(2, CMP_BLOCK * HD, HD), cst(2, HD), cst(1, HD), cst(1, HD),
                  pl.BlockSpec(memory_space=pltpu.SMEM)],
        out_specs=(pl.BlockSpec((None, NH * t, HD), lambda b, p, pt: (b, 0, 0)),
                   pl.BlockSpec((None, NKV * t, HD), lambda b, p, pt: (b, 0, 0))),
        scratch_shapes=[pltpu.VMEM((NKV, n_pages * PAGE // CMP_BLOCK, CMP_BLOCK * HD), f32)] * 2,
    )
    return pl.pallas_call(
        functools.partial(_nsasa_body, n_pages=n_pages, t=t),
        out_shape=(jax.ShapeDtypeStruct((nb, NH * t, HD), f32), jax.ShapeDtypeStruct((nb, NKV * t, HD), f32)),
        grid_spec=grid_spec,
        compiler_params=_cparams(2),
        name="nsa_sample_cmp",
    )(page_table, z, *([_pages2d(cache_ck)] * CMP_PAGES), *([_pages2d(cache_cv)] * CMP_PAGES), cmp_w.astype(bf16), cmp_b,
      cmp_k_g.reshape(1, HD), q_g.reshape(1, HD), rel_bias)


def _nsa_decode_body(pt_ref, q_ref, gt_ref, oc_ref, sel_ref, skn_ref, svn_ref, wkn_ref, wvn_ref, wk_ref, wv_ref,
                     *rest, n_pages, t):
    del pt_ref
    skp_refs, svp_refs = rest[:n_pages], rest[n_pages:2 * n_pages]
    qg_ref, rel_ref, o_ref = rest[2 * n_pages:]
    past = n_pages * PAGE
    rows = NREP * t
    qs = _nsa_q_groups(q_ref, qg_ref[...])
    rep = lambda x: jnp.concatenate([x] * NREP, axis=0)

    def bias_of(dist, g):
        bucket = _t5_bucket(dist)
        return jnp.concatenate([_bias_of_bucket(bucket, rel_ref, g * NREP + r) for r in range(NREP)], axis=0)

    nwin = WINDOW // HD
    dist_w = WINDOW + _iota2((t, WINDOW + HD), 0) - _iota2((t, WINDOW + HD), 1)
    ok_w = (dist_w >= 0) & (dist_w < WINDOW)
    n_far = (n_pages - 1) * HD
    n_sel_keys = (n_pages + 1) * HD
    dist_near = PAGE + _iota2((t, 2 * HD), 0) - _iota2((t, 2 * HD), 1)
    ok_s = (past + _iota2((t, n_sel_keys), 0) - _iota2((t, n_sel_keys), 1)) >= 0
    expand = jnp.where(_iota2((HD, n_sel_keys), 0) == _iota2((HD, n_sel_keys), 1) // SEL_BLOCK, 1.0, 0.0)

    jobs = []
    for g in range(NKV):
        gs = slice(g * HD, (g + 1) * HD)
        ks = [wk_ref[pl.ds(jj * HD * NKV + g, HD, stride=NKV), :] for jj in range(nwin)] + [_pad_rows(wkn_ref[:, gs], HD)]
        vs = [wv_ref[pl.ds(jj * HD * NKV + g, HD, stride=NKV), :] for jj in range(nwin)] + [_pad_rows(wvn_ref[:, gs], HD)]
        jobs.append((g, ks, vs, bias_of(dist_w, g), ok_w))
    for g in range(NKV):
        gs = slice(g * HD, (g + 1) * HD)
        ks = [_head_rows(r, g, NKV) for r in skp_refs] + [_pad_rows(skn_ref[:, gs], HD)]
        vs = [_head_rows(r, g, NKV) for r in svp_refs] + [_pad_rows(svn_ref[:, gs], HD)]
        far = jnp.concatenate([jnp.full((t, 1), rel_ref[N_BUCKETS - 1, g * NREP + r], f32) for r in range(NREP)], axis=0)
        bias = jnp.concatenate([jnp.broadcast_to(far, (rows, n_far)), bias_of(dist_near, g)], axis=1)
        picked = _mm(sel_ref[g * t:(g + 1) * t, :], expand) > 0.5
        jobs.append((g, ks, vs, bias, picked & ok_s))

    logits = [jnp.concatenate([_mm(qs[g], k, NT) for k in ks], axis=1) * SCALE + bias for g, ks, _, bias, _ in jobs]
    probs = [_softmax_full(lg, rep(job[4])) for lg, job in zip(logits, jobs)]
    outs = []
    for p, job in zip(probs, jobs):
        acc = jnp.zeros((rows, HD), f32)
        for n, v in enumerate(job[2]):
            acc = acc + _mm(p[:, n * HD:(n + 1) * HD], v)
        outs.append(acc)

    gts = jax.nn.sigmoid(gt_ref[...])
    for h in range(NH):
        g, r = divmod(h, NREP)
        sl = slice(r * t, (r + 1) * t)
        o = (gts[:, 3 * h:3 * h + 1] * oc_ref[g * rows + r * t:g * rows + (r + 1) * t, :]
             + gts[:, 3 * h + 1:3 * h + 2] * outs[NKV + g][sl] + gts[:, 3 * h + 2:3 * h + 3] * outs[g][sl])
        o_ref[:, h * HD:(h + 1) * HD] = o.astype(bf16)


def _nsa_decode_attn(z, sk, wk, row_off, nb, t, o_cmp, sel, win_k, win_v, cache_sk, cache_sv, page_table, q_g, rel_bias):
    n_pages = page_table.shape[1]
    assert PAGE >= MAX_DISTANCE and t <= HD
    rb = row_off // t
    w = NKV * HD
    pages = [pl.BlockSpec((None, PAGE * NKV, HD), lambda b, pt, jj=jj: (pt[b, jj], 0, 0)) for jj in range(n_pages)]
    rowsp = lambda width, j: pl.BlockSpec((t, width), lambda b, pt: (rb + b, j))
    grid_spec = pltpu.PrefetchScalarGridSpec(
        num_scalar_prefetch=1,
        grid=(nb,),
        in_specs=[rowsp(NH * HD, 3), rowsp(HD, 44),
                  pl.BlockSpec((None, NH * t, HD), lambda b, pt: (b, 0, 0)),
                  pl.BlockSpec((None, NKV * t, HD), lambda b, pt: (b, 0, 0)),
                  rowsp(w, 0), rowsp(w, 19), rowsp(w, 0), rowsp(w, 21),
                  pl.BlockSpec((None, WINDOW * NKV, HD), lambda b, pt: (b, 0, 0)),
                  pl.BlockSpec((None, WINDOW * NKV, HD), lambda b, pt: (b, 0, 0))] + pages + pages + [
                  pl.BlockSpec((1, HD), lambda b, pt: (0, 0)),
                  pl.BlockSpec(memory_space=pltpu.SMEM)],
        out_specs=pl.BlockSpec((t, NH * HD), lambda b, pt: (b, 0)),
    )
    return pl.pallas_call(
        functools.partial(_nsa_decode_body, n_pages=n_pages, t=t),
        out_shape=jax.ShapeDtypeStruct((nb * t, NH * HD), bf16),
        grid_spec=grid_spec,
        compiler_params=_cparams(1),
        name="nsa_sample_attn",
    )(page_table, z, z, o_cmp, sel, sk, z, wk, z, _pages2d(win_k), _pages2d(win_v),
      *([_pages2d(cache_sk)] * n_pages), *([_pages2d(cache_sv)] * n_pages), q_g.reshape(1, HD), rel_bias)


def _even_mixer(z, mp, bp, tp, bs, ts, conv_w, a_log, dt_bias, gdn_g, gate_bias, mlstm_g, s0, conv0, c0, n0, m0):
    w3 = 3 * NH * HD
    go_p, gs_p = _gdn(z, 0, bp, tp, conv_w, a_log, dt_bias, gdn_g)
    prev8 = jnp.concatenate([jnp.zeros((bs, 8 - conv0.shape[1], w3), f32), conv0], axis=1)
    go_s, gs_s = _gdn_decode(z, mp, bs, ts, conv_w, a_log, dt_bias, gdn_g, s0, prev8)
    mh_p, mc_p, mn_p, mm_p = _mlstm(z, 0, bp, tp, gate_bias, mlstm_g)
    mh_s, mc_s, mn_s, mm_s = _mlstm_decode(z, mp, bs, ts, gate_bias, mlstm_g, c0, n0, m0)
    mix = ((go_p, mh_p), (go_s, mh_s))
    keep = conv0.shape[1]
    conv_p = jnp.stack([z[(b + 1) * tp - keep:(b + 1) * tp, :w3] for b in range(bp)])
    conv_s = z[mp:, :w3].reshape(bs, ts, w3)[:, ts - keep:]
    states = (gs_p, gs_s, conv_p, conv_s, mc_p, mc_s, mn_p, mn_s, mm_p[..., 0], mm_s[..., 0])
    return mix, states


def _odd_mixer(z, mp, bp, tp, bs, ts, page_table, caches, win_k0, win_v0, q_g, k_g, cmp_w, cmp_b, rel_bias):
    sb_k, sb_v, cmp_k, cmp_v, sel_k, sel_v = caches
    sk, wk = _kvnorm(z, k_g)
    sb_p = _sb_prompt(z, bp, tp)
    sb_s = _sb_sample(z, mp, bs, ts, sb_k, sb_v, page_table)
    ns_p = _nsa_prompt(z, sk, wk, bp, tp, cmp_w, cmp_b, k_g[0], q_g, rel_bias)
    o_cmp, sel = _nsa_sample_cmp(z, mp, bs, ts, cmp_k, cmp_v, page_table, cmp_w, cmp_b, k_g[0], q_g, rel_bias)
    ns_s = _nsa_decode_attn(z, sk, wk, mp, bs, ts, o_cmp, sel, win_k0, win_v0, sel_k, sel_v, page_table, q_g, rel_bias)
    mix = ((sb_p, ns_p), (sb_s, ns_s))

    w = NH * HD
    kw = NKV * HD

    def rows(arr, lo, width, heads):
        sl = arr[:, lo:lo + width]
        return sl[:mp].reshape(bp, tp, heads, HD), sl[mp:].reshape(bs, ts, heads, HD)

    sbk_p, sbk_s = rows(z, w, w, NH)
    sbv_p, sbv_s = rows(z, 2 * w, w, NH)
    ck_p, ck_s = rows(z, 4 * w, kw, NKV)
    cv_p, cv_s = rows(z, 4 * w + kw, kw, NKV)
    sk_p, sk_s = rows(sk, 0, kw, NKV)
    sv_p, sv_s = rows(z, 4 * w + 3 * kw, kw, NKV)
    wk_p, wk_s = rows(wk, 0, kw, NKV)
    wv_p, wv_s = rows(z, 4 * w + 5 * kw, kw, NKV)
    keep_p = min(WINDOW, tp)
    win = lambda old, new: jnp.concatenate([old, new], axis=1)[:, -min(WINDOW, old.shape[1] + ts):]
    states = (sbk_p, sbk_s, sbv_p, sbv_s, ck_p, ck_s, cv_p, cv_s, sk_p, sk_s, sv_p, sv_s,
              wk_p[:, tp - keep_p:], win(win_k0, wk_s), wv_p[:, tp - keep_p:], win(win_v0, wv_s))
    return mix, states


def kernel(x_prompt, x_sample, state_gdn_s, state_gdn_conv, state_mlstm_c, state_mlstm_n, state_mlstm_m, cache_sb_k, cache_sb_v, cache_nsa_cmp_k, cache_nsa_cmp_v, cache_nsa_sel_k, cache_nsa_sel_v, state_nsa_win_k, state_nsa_win_v, page_table, norm_g, ffn_w_gate, ffn_w_up, ffn_w_down, even_w_in, gdn_conv_w, gdn_a_log, gdn_dt_bias, gdn_norm_g, mlstm_gate_bias, mlstm_norm_g, even_w_out, odd_w_in, nsa_q_norm_g, nsa_k_norm_g, nsa_cmp_w, nsa_cmp_b, odd_w_out, rel_bias):
    bp, tp, d = x_prompt.shape
    bs, ts, _ = x_sample.shape
    mp = bp * tp
    xs = jnp.concatenate([x_prompt.reshape(mp, d), x_sample.reshape(bs * ts, d)], axis=0)
    depth = norm_g.shape[0]
    even_states, odd_states = [], []
    for layer in range(depth):
        j = layer // 2
        ffn = lambda x, n, i: _ffn(x, norm_g[layer, n], ffn_w_gate, ffn_w_up, ffn_w_down, layer, i)
        xs = ffn(xs, 0, 0)
        if layer % 2 == 0:
            w_in = _even_reorder(even_w_in[j])
            w_in = _pad_cols(w_in.astype(bf16), PROJ_TN)
            z = _rms_matmul(xs, norm_g[layer, 1], w_in)
            mix, st = _even_mixer(z, mp, bp, tp, bs, ts, gdn_conv_w[j], gdn_a_log[j], gdn_dt_bias[j], gdn_norm_g[j],
                                  mlstm_gate_bias[j], mlstm_norm_g[j], state_gdn_s[j], state_gdn_conv[j],
                                  state_mlstm_c[j], state_mlstm_n[j], state_mlstm_m[j])
            even_states.append(st)
            w_out = even_w_out[j]
        else:
            w_in = _pad_cols(odd_w_in[j].astype(bf16), PROJ_TN)
            z = _rms_matmul(xs, norm_g[layer, 1], w_in)
            caches = (cache_sb_k[j], cache_sb_v[j], cache_nsa_cmp_k[j], cache_nsa_cmp_v[j], cache_nsa_sel_k[j],
                      cache_nsa_sel_v[j])
            mix, st = _odd_mixer(z, mp, bp, tp, bs, ts, page_table, caches, state_nsa_win_k[j], state_nsa_win_v[j],
                                 nsa_q_norm_g[j], nsa_k_norm_g[j], nsa_cmp_w[j], nsa_cmp_b[j], rel_bias)
            odd_states.append(st)
            w_out = odd_w_out[j]
        xs = _out_proj(mix[0], mix[1], w_out.astype(bf16), xs)
        xs = ffn(xs, 2, 1)
    stack = lambda sts, i: jnp.stack([s[i] for s in sts])
    outs = [xs[:mp].reshape(bp, tp, d), xs[mp:].reshape(bs, ts, d)]
    outs += [stack(even_states, i) for i in range(10)]
    outs += [stack(odd_states, i) for i in range(16)]
    return tuple(outs)
```
